```python
import math
import jax, jax.numpy as jnp
from jax import lax
import numpy as np

D_MODEL = 2048
BATCH = 8
SEQ = 4096
DEPTH = 4

N_MIXERS = 4
EPS = 1e-6
D_FF = 4 * D_MODEL
QBLOCK = 128

A_HEAD_DIM = 128
A_HEADS = D_MODEL // (2 * A_HEAD_DIM)

R_HEADS = D_MODEL // 256
R_KDIM = D_MODEL // R_HEADS
R_VDIM = 2 * R_KDIM
R_CHUNK = 128

M_D_INNER = 2 * D_MODEL
M_HEADDIM = 64
M_HEADS = M_D_INNER // M_HEADDIM
M_GROUPS = 8
M_DSTATE = 128
M_CONV = 4
M_CHUNK = 256
M_CONV_DIM = M_D_INNER + 2 * M_GROUPS * M_DSTATE
M_IN_DIM = M_D_INNER + M_CONV_DIM + M_HEADS

D_HEAD_DIM = 128
D_HEADS = D_MODEL // D_HEAD_DIM
D_PATTERNS = ((128, 1), (512, 4), (2048, 16))
D_SPAN = 128
D_NGROUPS = len(D_PATTERNS)

kernel_name = 'hybrid_interleaved_diff_ret_ssd_dilated'

F32 = jnp.float32


def rmsnorm(x, w):
    xf = x.astype(F32)
    y = xf * lax.rsqrt(jnp.mean(xf * xf, axis=-1, keepdims=True) + EPS)
    return (y * w.astype(F32)).astype(x.dtype)


def diff_attention(h, w_in, w_out, q_norm_w, k_norm_w, lq1, lk1, lq2, lk2, subln_w, layer_idx):
    B_, S, _ = h.shape
    lambda_init = 0.8 - 0.6 * math.exp(-0.3 * layer_idx)
    q, k, v = jnp.split(h @ w_in, [D_MODEL, 2 * D_MODEL], axis=-1)
    q = rmsnorm(q.reshape(B_, S, 2 * A_HEADS, A_HEAD_DIM), q_norm_w).astype(F32)
    k = rmsnorm(k.reshape(B_, S, 2 * A_HEADS, A_HEAD_DIM), k_norm_w).astype(F32)
    v = v.reshape(B_, S, A_HEADS, 2 * A_HEAD_DIM).astype(F32)
    lam = (jnp.exp(jnp.sum(lq1.astype(F32) * lk1.astype(F32)))
           - jnp.exp(jnp.sum(lq2.astype(F32) * lk2.astype(F32))) + lambda_init)
    scale = A_HEAD_DIM ** -0.5
    nb = S // QBLOCK
    qb = q.reshape(B_, nb, QBLOCK, 2 * A_HEADS, A_HEAD_DIM).transpose(1, 0, 2, 3, 4)
    kpos = jnp.arange(S)

    def block(args):
        qi, bi = args
        s = jnp.einsum('bqhd,bkhd->bhqk', qi, k) * scale
        qpos = bi * QBLOCK + jnp.arange(QBLOCK)
        s = jnp.where(kpos[None, :] <= qpos[:, None], s, -jnp.inf)
        p = jax.nn.softmax(s, axis=-1).reshape(B_, A_HEADS, 2, QBLOCK, S)
        a = p[:, :, 0] - lam * p[:, :, 1]
        return jnp.einsum('bhqk,bkhd->bqhd', a, v)

    o = lax.map(block, (qb, jnp.arange(nb)))
    o = o.transpose(1, 0, 2, 3, 4).reshape(B_, S, A_HEADS, 2 * A_HEAD_DIM)
    o = rmsnorm(o, subln_w) * (1.0 - lambda_init)
    return o.reshape(B_, S, D_MODEL) @ w_out


def _rotate_every_two(t):
    t1 = t[..., ::2]
    t2 = t[..., 1::2]
    return jnp.stack([-t2, t1], axis=-1).reshape(t.shape)


def retention(h, w_in, gn_w, w_out):
    B_, S, _ = h.shape
    C = R_CHUNK
    nc = S // C
    proj = (h @ w_in).astype(F32)
    q, k, v, g = jnp.split(proj, [D_MODEL, 2 * D_MODEL, 2 * D_MODEL + R_HEADS * R_VDIM], axis=-1)
    q = q.reshape(B_, S, R_HEADS, R_KDIM)
    k = k.reshape(B_, S, R_HEADS, R_KDIM) * (R_KDIM ** -0.5)
    v = v.reshape(B_, S, R_HEADS, R_VDIM)
    angle = jnp.repeat(1.0 / (10000.0 ** jnp.linspace(0.0, 1.0, R_KDIM // 2, dtype=F32)), 2)
    ang = jnp.arange(S, dtype=F32)[:, None] * angle[None, :]
    sin, cos = jnp.sin(ang)[:, None, :], jnp.cos(ang)[:, None, :]
    q = q * cos + _rotate_every_two(q) * sin
    k = k * cos + _rotate_every_two(k) * sin
    log_g = jnp.log(1.0 - 2.0 ** (-5.0 - jnp.arange(R_HEADS, dtype=F32)))
    idx = jnp.arange(C, dtype=F32)
    rel = idx[:, None] - idx[None, :]
    dmask = jnp.where(rel[None] >= 0, jnp.exp(jnp.maximum(rel, 0.0)[None] * log_g[:, None, None]), 0.0)
    xi = jnp.exp((idx + 1.0)[None, :] * log_g[:, None])
    zeta = jnp.exp((C - 1.0 - idx)[None, :] * log_g[:, None])
    cdecay = jnp.exp(C * log_g)

    def chunks(t):
        return t.reshape(B_, nc, C, R_HEADS, t.shape[-1]).transpose(1, 0, 3, 2, 4)

    def step(R, inp):
        qc, kc, vc = inp
        s = jnp.einsum('bhid,bhjd->bhij', qc, kc) * dmask[None]
        o = (jnp.einsum('bhij,bhjv->bhiv', s, vc)
             + jnp.einsum('bhid,bhdv->bhiv', qc * xi[None, :, :, None], R))
        R = R * cdecay[None, :, None, None] + jnp.einsum('bhjd,bhjv->bhdv', kc * zeta[None, :, :, None], vc)
        return R, o

    R0 = jnp.zeros((B_, R_HEADS, R_KDIM, R_VDIM), F32)
    _, o = lax.scan(step, R0, (chunks(q), chunks(k), chunks(v)))
    o = o.transpose(1, 0, 3, 2, 4).reshape(B_, S, R_HEADS, R_VDIM)
    o = rmsnorm(o, gn_w.reshape(R_HEADS, R_VDIM)).reshape(B_, S, R_HEADS * R_VDIM)
    return (jax.nn.silu(g) * o) @ w_out


def ssd_chunked(xdt, a, Bm, Cm):
    B_, S = xdt.shape[:2]
    pad = (-S) % M_CHUNK
    L = S + pad
    nc = L // M_CHUNK
    Rh = M_HEADS // M_GROUPS
    padt = lambda t: jnp.pad(t, [(0, 0), (0, pad)] + [(0, 0)] * (t.ndim - 2))
    xc = padt(xdt).reshape(B_, nc, M_CHUNK, M_GROUPS, Rh, M_HEADDIM).transpose(1, 0, 2, 3, 4, 5)
    ac = padt(a).reshape(B_, nc, M_CHUNK, M_GROUPS, Rh).transpose(1, 0, 2, 3, 4)
    bc = padt(Bm).reshape(B_, nc, M_CHUNK, M_GROUPS, M_DSTATE).transpose(1, 0, 2, 3, 4)
    cc = padt(Cm).reshape(B_, nc, M_CHUNK, M_GROUPS, M_DSTATE).transpose(1, 0, 2, 3, 4)
    tril = jnp.tril(jnp.ones((M_CHUNK, M_CHUNK), bool))

    def step(state, inp):
        x_, a_, b_, c_ = inp
        acs = jnp.cumsum(a_, axis=1)
        diff = acs[:, :, None] - acs[:, None, :]
        Lmat = jnp.exp(jnp.where(tril[None, :, :, None, None], diff, -jnp.inf))
        cb = jnp.einsum('bign,bjgn->bijg', c_, b_)
        y = (jnp.einsum('bijg,bijgr,bjgrp->bigrp', cb, Lmat, x_)
             + jnp.einsum('bign,bgrpn,bigr->bigrp', c_, state, jnp.exp(acs)))
        decay_end = jnp.exp(acs[:, -1:] - acs)
        state = (state * jnp.exp(acs[:, -1])[..., None, None]
                 + jnp.einsum('bjgn,bjgr,bjgrp->bgrpn', b_, decay_end, x_))
        return state, y

    s0 = jnp.zeros((B_, M_GROUPS, Rh, M_HEADDIM, M_DSTATE), F32)
    _, y = lax.scan(step, s0, (xc, ac, bc, cc))
    y = y.transpose(1, 0, 2, 3, 4, 5).reshape(B_, L, M_HEADS, M_HEADDIM)
    return y[:, :S]


def mamba2(h, w_in, conv_w, conv_b, dt_bias, a_log, d_skip, norm_w, w_out):
    B_, S, _ = h.shape
    zxbcdt = (h @ w_in).astype(F32)
    z, xbc, dt = jnp.split(zxbcdt, [M_D_INNER, M_D_INNER + M_CONV_DIM], axis=-1)
    xbc = lax.conv_general_dilated(
        xbc, conv_w.astype(F32).reshape(M_CONV, 1, M_CONV_DIM), window_strides=(1,),
        padding=[(M_CONV - 1, 0)], dimension_numbers=('NWC', 'WIO', 'NWC'),
        feature_group_count=M_CONV_DIM)
    xbc = jax.nn.silu(xbc + conv_b.astype(F32))
    xs, Bm, Cm = jnp.split(xbc, [M_D_INNER, M_D_INNER + M_GROUPS * M_DSTATE], axis=-1)
    xs = xs.reshape(B_, S, M_HEADS, M_HEADDIM)
    Bm = Bm.reshape(B_, S, M_GROUPS, M_DSTATE)
    Cm = Cm.reshape(B_, S, M_GROUPS, M_DSTATE)
    dt = jax.nn.softplus(dt + dt_bias.astype(F32))
    A = -jnp.exp(a_log.astype(F32))
    y = ssd_chunked(xs * dt[..., None], dt * A, Bm, Cm)
    y = y + xs * d_skip.astype(F32)[:, None]
    y = y.reshape(B_, S, M_D_INNER) * jax.nn.silu(z)
    y = rmsnorm(y.reshape(B_, S, M_GROUPS, M_D_INNER // M_GROUPS),
                norm_w.reshape(M_GROUPS, M_D_INNER // M_GROUPS)).reshape(B_, S, M_D_INNER)
    return y @ w_out


def strided_window_attention(q, k, v, dil):
    B_, S, H, E = q.shape
    Ls = S // dil
    pad = (-Ls) % D_SPAN
    nb = (Ls + pad) // D_SPAN

    def to_blocks(t):
        t = t.reshape(B_, Ls, dil, H, E).transpose(0, 2, 1, 3, 4)
        t = jnp.pad(t, ((0, 0), (0, 0), (0, pad), (0, 0), (0, 0)))
        return t.reshape(B_, dil, nb, D_SPAN, H, E)

    def with_prev(t):
        prev = jnp.pad(t, ((0, 0), (0, 0), (1, 0), (0, 0), (0, 0), (0, 0)))[:, :, :-1]
        return jnp.concatenate([prev, t], axis=3)

    qb = to_blocks(q)
    kk = with_prev(to_blocks(k))
    vv = with_prev(to_blocks(v))
    s = jnp.einsum('bdnqhe,bdnkhe->bdnhqk', qb, kk) * (E ** -0.5)
    qi = jnp.arange(D_SPAN) + D_SPAN
    ki = jnp.arange(2 * D_SPAN)
    rel = qi[:, None] - ki[None, :]
    band = (rel >= 0) & (rel <= D_SPAN)
    has_prev = (jnp.arange(nb)[:, None] > 0) | (ki[None, :] >= D_SPAN)
    mask = band[None] & has_prev[:, None, :]
    s = jnp.where(mask[None, None, :, None], s, -jnp.inf)
    lse = jax.nn.logsumexp(s, axis=-1)
    p = jnp.exp(s - lse[..., None])
    o = jnp.einsum('bdnhqk,bdnkhe->bdnqhe', p, vv)
    o = o.reshape(B_, dil, nb * D_SPAN, H, E)[:, :, :Ls].transpose(0, 2, 1, 3, 4).reshape(B_, S, H, E)
    lse = lse.transpose(0, 1, 2, 4, 3).reshape(B_, dil, nb * D_SPAN, H)[:, :, :Ls]
    lse = lse.transpose(0, 2, 1, 3).reshape(B_, S, H)
    return o, lse


def dilated_attention(h, w_in, q_norm_w, k_norm_w, w_out):
    B_, S, _ = h.shape
    qkv = (h @ w_in).reshape(B_, S, D_NGROUPS, 3, D_HEADS, D_HEAD_DIM)
    outs, lses = [], []
    for g, (window, dil) in enumerate(D_PATTERNS):
        q = rmsnorm(qkv[:, :, g, 0], q_norm_w[g]).astype(F32)
        k = rmsnorm(qkv[:, :, g, 1], k_norm_w[g]).astype(F32)
        v = qkv[:, :, g, 2].astype(F32)
        o, lse = strided_window_attention(q, k, v, dil)
        outs.append(o)
        lses.append(lse)
    wts = jax.nn.softmax(jnp.stack(lses, 0), axis=0)
    o = jnp.einsum('gbsh,gbshe->bshe', wts, jnp.stack(outs, 0))
    return o.reshape(B_, S, D_HEADS * D_HEAD_DIM) @ w_out


def sqrelu_mlp(h, w1, w2):
    return jnp.square(jax.nn.relu(h @ w1)) @ w2


def setup_inputs(seed: int = 0) -> dict:
    key = jax.random.key(seed)
    it = iter(list(jax.random.split(key, 48)))
    nrm = lambda shape, scale: jax.random.normal(next(it), shape, F32) * scale
    gain = lambda shape: 1.0 + 0.02 * jax.random.normal(next(it), shape, F32)
    dt = jnp.exp(jax.random.uniform(next(it), (M_HEADS,), F32) * (math.log(0.1) - math.log(0.001)) + math.log(0.001))
    dt_bias = dt + jnp.log(-jnp.expm1(-dt))
    a_log = jnp.log(jax.random.uniform(next(it), (M_HEADS,), F32, 1.0, 16.0))
    return {
        'x': nrm((BATCH, SEQ, D_MODEL), 1.0),
        'norm1_w': gain((DEPTH, D_MODEL)),
        'norm2_w': gain((DEPTH, D_MODEL)),
        'mlp_w1': nrm((DEPTH, D_MODEL, D_FF), D_MODEL ** -0.5),
        'mlp_w2': nrm((DEPTH, D_FF, D_MODEL), D_FF ** -0.5),
        'a_w_in': nrm((D_MODEL, 3 * D_MODEL), D_MODEL ** -0.5),
        'a_q_norm_w': gain((A_HEAD_DIM,)),
        'a_k_norm_w': gain((A_HEAD_DIM,)),
        'a_lambda_q1': nrm((A_HEAD_DIM,), 0.1),
        'a_lambda_k1': nrm((A_HEAD_DIM,), 0.1),
        'a_lambda_q2': nrm((A_HEAD_DIM,), 0.1),
        'a_lambda_k2': nrm((A_HEAD_DIM,), 0.1),
        'a_subln_w': gain((2 * A_HEAD_DIM,)),
        'a_w_out': nrm((D_MODEL, D_MODEL), D_MODEL ** -0.5),
        'b_w_in': nrm((D_MODEL, 2 * D_MODEL + 2 * R_HEADS * R_VDIM), D_MODEL ** -0.5),
        'b_gn_w': gain((R_HEADS * R_VDIM,)),
        'b_w_out': nrm((R_HEADS * R_VDIM, D_MODEL), (R_HEADS * R_VDIM) ** -0.5),
        'c_w_in': nrm((D_MODEL, M_IN_DIM), D_MODEL ** -0.5),
        'c_conv_w': nrm((M_CONV, M_CONV_DIM), M_CONV ** -0.5),
        'c_conv_b': nrm((M_CONV_DIM,), 0.02),
        'c_dt_bias': dt_bias,
        'c_a_log': a_log,
        'c_d_skip': 1.0 + 0.1 * jax.random.normal(next(it), (M_HEADS,), F32),
        'c_norm_w': gain((M_D_INNER,)),
        'c_w_out': nrm((M_D_INNER, D_MODEL), M_D_INNER ** -0.5),
        'd_w_in': nrm((D_MODEL, D_NGROUPS * 3 * D_HEADS * D_HEAD_DIM), D_MODEL ** -0.5),
        'd_q_norm_w': gain((D_NGROUPS, D_HEAD_DIM)),
        'd_k_norm_w': gain((D_NGROUPS, D_HEAD_DIM)),
        'd_w_out': nrm((D_HEADS * D_HEAD_DIM, D_MODEL), (D_HEADS * D_HEAD_DIM) ** -0.5),
    }


def reference(x, norm1_w, norm2_w, mlp_w1, mlp_w2,
              a_w_in, a_q_norm_w, a_k_norm_w, a_lambda_q1, a_lambda_k1, a_lambda_q2, a_lambda_k2,
              a_subln_w, a_w_out,
              b_w_in, b_gn_w, b_w_out,
              c_w_in, c_conv_w, c_conv_b, c_dt_bias, c_a_log, c_d_skip, c_norm_w, c_w_out,
              d_w_in, d_q_norm_w, d_k_norm_w, d_w_out):
    for i in range(DEPTH):
        h = rmsnorm(x, norm1_w[i])
        m = i % N_MIXERS
        if m == 0:
            out = diff_attention(h, a_w_in, a_w_out, a_q_norm_w, a_k_norm_w, a_lambda_q1,
                                 a_lambda_k1, a_lambda_q2, a_lambda_k2, a_subln_w, i)
        elif m == 1:
            out = retention(h, b_w_in, b_gn_w, b_w_out)
        elif m == 2:
            out = mamba2(h, c_w_in, c_conv_w, c_conv_b, c_dt_bias, c_a_log, c_d_skip, c_norm_w, c_w_out)
        else:
            out = dilated_attention(h, d_w_in, d_q_norm_w, d_k_norm_w, d_w_out)
        x = x + out.astype(x.dtype)
        h = rmsnorm(x, norm2_w[i])
        x = x + sqrelu_mlp(h, mlp_w1[i], mlp_w2[i]).astype(x.dtype)
    return x
```

```python
import functools
import math

import jax
import jax.numpy as jnp
from jax import lax
from jax.experimental import pallas as pl
from jax.experimental.pallas import tpu as pltpu

F32 = jnp.float32
BF16 = jnp.bfloat16
HIGHEST = lax.Precision.HIGHEST

EPS = 1e-6
D_MODEL = 2048
D_FF = 4 * D_MODEL
LANES = 128

A_HEAD_DIM = 128
A_HEADS = 8
R_HEADS = 8
R_KDIM = 256
R_VDIM = 512
M_D_INNER = 4096
M_HEADDIM = 64
M_HEADS = 64
M_GROUPS = 8
M_DSTATE = 128
M_CONV = 4
M_GROUP_W = M_D_INNER // M_GROUPS
M_HPG = M_HEADS // M_GROUPS
D_HEADS = 16
D_HEAD_DIM = 128
D_PATTERNS = ((128, 1), (512, 4), (2048, 16))
D_SPAN = 128

VMEM_LIMIT_BYTES = 56 * 1024 * 1024
NEG_BIG = -1e30

NT_DIMS = (((1,), (1,)), ((), ()))
TN_DIMS = (((0,), (0,)), ((), ()))


def _params(*sem):
    return pltpu.CompilerParams(dimension_semantics=sem, vmem_limit_bytes=VMEM_LIMIT_BYTES)


def _silu(v):
    return v * (1.0 / (1.0 + jnp.exp(-v)))


def _rms_rows(v, w):
    ms = jnp.mean(v * v, axis=-1, keepdims=True)
    return v * lax.rsqrt(ms + EPS) * w


def _norm_matmul_kernel(x_ref, nw_ref, w_ref, cw_ref, o_ref, h_sc, *, tn, head_norm, period, count):
    j = pl.program_id(1)

    @pl.when(j == 0)
    def _():
        h_sc[...] = _rms_rows(x_ref[...], nw_ref[...]).astype(BF16)

    acc = jnp.dot(h_sc[...], w_ref[...], preferred_element_type=F32)
    if not head_norm:
        o_ref[...] = acc.astype(o_ref.dtype)
        return
    is_norm = (j % period) < count

    @pl.when(is_norm)
    def _():
        for c in range(tn // LANES):
            sl = slice(c * LANES, (c + 1) * LANES)
            o_ref[:, sl] = _rms_rows(acc[:, sl], cw_ref[:, sl]).astype(o_ref.dtype)

    @pl.when(jnp.logical_not(is_norm))
    def _():
        o_ref[...] = acc.astype(o_ref.dtype)


def norm_matmul(x, nw, w, colw=None, *, tm=1024, tn=1024, out_dtype=BF16, norm_cols_per_2048=0):
    T, K = x.shape
    N = w.shape[1]
    tm = min(tm, T)
    tn = min(tn, N)
    head_norm = colw is not None
    if colw is None:
        colw = jnp.ones((1, N), F32)
    period = 3 * D_MODEL // tn
    count = 2 * D_MODEL // tn
    kern = functools.partial(_norm_matmul_kernel, tn=tn, head_norm=head_norm, period=period, count=count)
    return pl.pallas_call(
        kern,
        grid=(T // tm, N // tn),
        in_specs=[
            pl.BlockSpec((tm, K), lambda i, j: (i, 0)),
            pl.BlockSpec((1, K), lambda i, j: (0, 0)),
            pl.BlockSpec((K, tn), lambda i, j: (0, j)),
            pl.BlockSpec((1, tn), lambda i, j: (0, j)),
        ],
        out_specs=pl.BlockSpec((tm, tn), lambda i, j: (i, j)),
        out_shape=jax.ShapeDtypeStruct((T, N), out_dtype),
        scratch_shapes=[pltpu.VMEM((tm, K), BF16)],
        compiler_params=_params("parallel", "arbitrary"),
        name="norm_matmul",
    )(x, nw.reshape(1, K), w, colw)


def _matmul_residual_kernel(a_ref, w_ref, r_ref, o_ref):
    o_ref[...] = r_ref[...] + jnp.dot(a_ref[...], w_ref[...], preferred_element_type=F32)


def matmul_residual(a, w, res, *, tm=512, tn=1024):
    T, K = a.shape
    N = w.shape[1]
    tm = min(tm, T)
    return pl.pallas_call(
        _matmul_residual_kernel,
        grid=(T // tm, N // tn),
        in_specs=[
            pl.BlockSpec((tm, K), lambda i, j: (i, 0)),
            pl.BlockSpec((K, tn), lambda i, j: (0, j)),
            pl.BlockSpec((tm, tn), lambda i, j: (i, j)),
        ],
        out_specs=pl.BlockSpec((tm, tn), lambda i, j: (i, j)),
        out_shape=jax.ShapeDtypeStruct((T, N), F32),
        compiler_params=_params("parallel", "arbitrary"),
        name="matmul_residual",
    )(a, w, res)


def _mlp_kernel(x_ref, nw_ref, w1_ref, w2_ref, o_ref, h_sc):
    f = pl.program_id(1)

    @pl.when(f == 0)
    def _():
        x = x_ref[...]
        h_sc[...] = _rms_rows(x, nw_ref[...]).astype(BF16)
        o_ref[...] = x

    u = jnp.dot(h_sc[...], w1_ref[...], preferred_element_type=F32)
    u = jnp.square(jnp.maximum(u, 0.0)).astype(BF16)
    o_ref[...] += jnp.dot(u, w2_ref[...], preferred_element_type=F32)


def mlp(x, nw, w1, w2, *, tm=512, tf=1024):
    T, D = x.shape
    FF = w1.shape[1]
    tm = min(tm, T)
    return pl.pallas_call(
        _mlp_kernel,
        grid=(T // tm, FF // tf),
        in_specs=[
            pl.BlockSpec((tm, D), lambda i, f: (i, 0)),
            pl.BlockSpec((1, D), lambda i, f: (0, 0)),
            pl.BlockSpec((D, tf), lambda i, f: (0, f)),
            pl.BlockSpec((tf, D), lambda i, f: (f, 0)),
        ],
        out_specs=pl.BlockSpec((tm, D), lambda i, f: (i, 0)),
        out_shape=jax.ShapeDtypeStruct((T, D), F32),
        scratch_shapes=[pltpu.VMEM((tm, D), BF16)],
        compiler_params=_params("parallel", "arbitrary"),
        name="mlp",
    )(x, nw.reshape(1, D), w1, w2)


def _diff_attn_kernel(q_ref, k_ref, v_ref, lq1_ref, lk1_ref, lq2_ref, lk2_ref, sw_ref, o_ref,
                      *, tq, lambda_init):
    qi = pl.program_id(2)
    lam = (jnp.exp(jnp.sum(lq1_ref[...] * lk1_ref[...], axis=-1, keepdims=True))
           - jnp.exp(jnp.sum(lq2_ref[...] * lk2_ref[...], axis=-1, keepdims=True)) + lambda_init)
    row = lax.broadcasted_iota(jnp.int32, (tq, tq), 0)
    col = lax.broadcasted_iota(jnp.int32, (tq, tq), 1)
    causal = col <= row

    def attend(q, k, v, carry, mask):
        mx, l, acc = carry
        s = lax.dot_general(q, k, NT_DIMS, preferred_element_type=F32)
        if mask:
            s = jnp.where(causal, s, NEG_BIG)
        m_new = jnp.maximum(mx, jnp.max(s, axis=-1, keepdims=True))
        p = jnp.exp(s - m_new)
        alpha = jnp.exp(mx - m_new)
        l = alpha * l + jnp.sum(p, axis=-1, keepdims=True)
        acc = alpha * acc + jnp.dot(p.astype(BF16), v, preferred_element_type=F32)
        return m_new, l, acc

    outs = []
    for m in range(2):
        hs = slice(m * A_HEAD_DIM, (m + 1) * A_HEAD_DIM)
        q = q_ref[0, :, hs]

        def body(j, carry, hs=hs, q=q):
            start = pl.multiple_of(j * tq, tq)
            return attend(q, k_ref[0, pl.ds(start, tq), hs], v_ref[0, pl.ds(start, tq), :], carry, False)

        init = (jnp.full((tq, 1), NEG_BIG, F32), jnp.zeros((tq, 1), F32), jnp.zeros((tq, 2 * A_HEAD_DIM), F32))
        carry = lax.fori_loop(0, qi, body, init)
        start = pl.multiple_of(qi * tq, tq)
        _, l, acc = attend(q, k_ref[0, pl.ds(start, tq), hs], v_ref[0, pl.ds(start, tq), :], carry, True)
        outs.append(acc * (1.0 / l))
    o = outs[0] - lam * outs[1]
    o_ref[0] = (_rms_rows(o, sw_ref[...]) * (1.0 - lambda_init)).astype(o_ref.dtype)


def diff_attention(qkv, lq1, lk1, lq2, lk2, subln_w, *, layer_idx, tq=512):
    B, S, _ = qkv.shape
    tq = min(tq, S)
    lambda_init = 0.8 - 0.6 * math.exp(-0.3 * layer_idx)
    pw = 2 * A_HEAD_DIM
    vec = lambda a: a.reshape(1, -1).astype(F32)
    small = lambda n: pl.BlockSpec((1, n), lambda b, h, i: (0, 0))
    kern = functools.partial(_diff_attn_kernel, tq=tq, lambda_init=lambda_init)
    return pl.pallas_call(
        kern,
        grid=(B, A_HEADS, S // tq),
        in_specs=[
            pl.BlockSpec((1, tq, pw), lambda b, h, i: (b, i, h)),
            pl.BlockSpec((1, S, pw), lambda b, h, i: (b, 0, A_HEADS + h)),
            pl.BlockSpec((1, S, pw), lambda b, h, i: (b, 0, 2 * A_HEADS + h)),
            small(A_HEAD_DIM), small(A_HEAD_DIM), small(A_HEAD_DIM), small(A_HEAD_DIM), small(pw),
        ],
        out_specs=pl.BlockSpec((1, tq, pw), lambda b, h, i: (b, i, h)),
        out_shape=jax.ShapeDtypeStruct((B, S, D_MODEL), BF16),
        compiler_params=_params("parallel", "parallel", "arbitrary"),
        name="diff_attention",
    )(qkv, qkv, qkv, vec(lq1), vec(lk1), vec(lq2), vec(lk2), vec(subln_w))


def _retention_kernel(q_ref, k_ref, v_ref, g_ref, cos_ref, sin_ref, dm_ref, xi_ref, zeta_ref, cd_ref,
                      gw_ref, o_ref, r_sc, *, tr, chunk):
    @pl.when(pl.program_id(2) == 0)
    def _():
        r_sc[...] = jnp.zeros_like(r_sc)

    half = R_KDIM // 2
    dmask = dm_ref[0]
    xi = xi_ref[0]
    zeta = zeta_ref[0]
    cdecay = cd_ref[0, 0:1, 0:1]
    gw = gw_ref[0]

    def rotate(t, c, s):
        te, to = t[:, :half], t[:, half:]
        return te * c - to * s, to * c + te * s

    for ci in range(tr // chunk):
        rows = slice(ci * chunk, (ci + 1) * chunk)
        c = cos_ref[rows, :]
        s = sin_ref[rows, :]
        qe, qo = rotate(q_ref[0, rows, :].astype(F32), c, s)
        ke, ko = rotate(k_ref[0, rows, :].astype(F32) * (R_KDIM ** -0.5), c, s)
        v = v_ref[0, rows, :]
        q_r = jnp.concatenate([qe, qo], axis=1).astype(BF16)
        k_r = jnp.concatenate([ke, ko], axis=1).astype(BF16)
        q_x = jnp.concatenate([qe * xi, qo * xi], axis=1).astype(BF16)
        k_z = jnp.concatenate([ke * zeta, ko * zeta], axis=1).astype(BF16)
        r_old = r_sc[...]
        sc = lax.dot_general(q_r, k_r, NT_DIMS, preferred_element_type=F32) * dmask
        o = (jnp.dot(sc.astype(BF16), v, preferred_element_type=F32)
             + jnp.dot(q_x, r_old.astype(BF16), preferred_element_type=F32))
        r_sc[...] = r_old * cdecay + lax.dot_general(k_z, v, TN_DIMS, preferred_element_type=F32)
        gate = _silu(g_ref[0, rows, :].astype(F32))
        o_ref[0, rows, :] = (gate * _rms_rows(o, gw)).astype(o_ref.dtype)


def retention(proj, gn_w, *, tr=1024, chunk=256):
    B, S, _ = proj.shape
    tr = min(tr, S)
    chunk = min(chunk, tr)
    half = R_KDIM // 2
    angle = 1.0 / (10000.0 ** jnp.linspace(0.0, 1.0, half, dtype=F32))
    ang = jnp.arange(S, dtype=F32)[:, None] * angle[None, :]
    cos, sin = jnp.cos(ang), jnp.sin(ang)
    log_g = jnp.log(1.0 - 2.0 ** (-5.0 - jnp.arange(R_HEADS, dtype=F32)))
    idx = jnp.arange(chunk, dtype=F32)
    rel = idx[:, None] - idx[None, :]
    dmask = jnp.where(rel[None] >= 0, jnp.exp(jnp.maximum(rel, 0.0)[None] * log_g[:, None, None]), 0.0)
    xi = jnp.exp((idx + 1.0)[None, :] * log_g[:, None])
    zeta = jnp.exp((chunk - 1.0 - idx)[None, :] * log_g[:, None])
    cdecay = jnp.exp(chunk * log_g)
    bc = lambda t: jnp.broadcast_to(t[:, :, None], (R_HEADS, chunk, half))
    cd = jnp.broadcast_to(cdecay[:, None, None], (R_HEADS, 8, LANES))
    nq = D_MODEL // R_KDIM
    nv = 2 * D_MODEL // R_VDIM
    kern = functools.partial(_retention_kernel, tr=tr, chunk=chunk)
    head = lambda shape: pl.BlockSpec(shape, lambda b, h, i: (h, 0, 0))
    return pl.pallas_call(
        kern,
        grid=(B, R_HEADS, S // tr),
        in_specs=[
            pl.BlockSpec((1, tr, R_KDIM), lambda b, h, i: (b, i, h)),
            pl.BlockSpec((1, tr, R_KDIM), lambda b, h, i: (b, i, nq + h)),
            pl.BlockSpec((1, tr, R_VDIM), lambda b, h, i: (b, i, nv + h)),
            pl.BlockSpec((1, tr, R_VDIM), lambda b, h, i: (b, i, nv + R_HEADS + h)),
            pl.BlockSpec((tr, half), lambda b, h, i: (i, 0)),
            pl.BlockSpec((tr, half), lambda b, h, i: (i, 0)),
            head((1, chunk, chunk)), head((1, chunk, half)), head((1, chunk, half)),
            head((1, 8, LANES)), head((1, 1, R_VDIM)),
        ],
        out_specs=pl.BlockSpec((1, tr, R_VDIM), lambda b, h, i: (b, i, h)),
        out_shape=jax.ShapeDtypeStruct((B, S, R_HEADS * R_VDIM), BF16),
        scratch_shapes=[pltpu.VMEM((R_KDIM, R_VDIM), F32)],
        compiler_params=_params("parallel", "parallel", "arbitrary"),
        name="retention",
    )(proj, proj, proj, proj, cos, sin, dmask, bc(xi), bc(zeta), cd, gn_w.reshape(R_HEADS, 1, R_VDIM).astype(F32))


def _mamba_kernel(z_ref, x_ref, b_ref, c_ref, dt_ref, wx_ref, wb_ref, wc_ref, bx_ref, bb_ref, bc_ref,
                  dtb_ref, alog_ref, dsk_ref, nw_ref, o_ref,
                  st_sc, tx_sc, tb_sc, tc_sc, *, chunk):
    L = chunk

    @pl.when(pl.program_id(2) == 0)
    def _():
        st_sc[...] = jnp.zeros_like(st_sc)
        tx_sc[...] = jnp.zeros_like(tx_sc)
        tb_sc[...] = jnp.zeros_like(tb_sc)
        tc_sc[...] = jnp.zeros_like(tc_sc)

    def conv_silu(cur_ref, tail_sc, w_ref, bias_ref):
        cur = cur_ref[0].astype(F32)
        ext = jnp.concatenate([tail_sc[...], cur], axis=0)
        tail_sc[...] = cur[L - 8:, :]
        w = w_ref[...]
        acc = bias_ref[...] + cur * w[M_CONV - 1:M_CONV, :]
        for kk in range(M_CONV - 1):
            sh = M_CONV - 1 - kk
            acc = acc + ext[8 - sh:8 - sh + L, :] * w[kk:kk + 1, :]
        return _silu(acc)

    xs = conv_silu(x_ref, tx_sc, wx_ref, bx_ref)
    bm = conv_silu(b_ref, tb_sc, wb_ref, bb_ref)
    cm = conv_silu(c_ref, tc_sc, wc_ref, bc_ref)

    raw = dt_ref[0, 0] + dtb_ref[0]
    dt_t = jnp.maximum(raw, 0.0) + jnp.log(1.0 + jnp.exp(-jnp.abs(raw)))
    a_t = dt_t * (-jnp.exp(alog_ref[0]))

    ri = lax.broadcasted_iota(jnp.int32, (L, L), 0)
    ci = lax.broadcasted_iota(jnp.int32, (L, L), 1)
    tril = ci <= ri
    eye = (ri == ci).astype(F32)
    lower = tril.astype(F32)
    upper = (ri <= ci).astype(F32)
    er = lax.broadcasted_iota(jnp.int32, (M_HPG, M_GROUP_W), 0)
    ec = lax.broadcasted_iota(jnp.int32, (M_HPG, M_GROUP_W), 1)
    expand = ((ec >= er * M_HEADDIM) & (ec < (er + 1) * M_HEADDIM)).astype(F32)

    f32dot = functools.partial(jnp.dot, preferred_element_type=F32, precision=HIGHEST)
    acs_row = f32dot(a_t, upper)
    acs_col = lax.dot_general(lower, a_t, NT_DIMS, preferred_element_type=F32, precision=HIGHEST)
    dt_col = lax.dot_general(eye, dt_t, NT_DIMS, preferred_element_type=F32, precision=HIGHEST)
    acs_x = f32dot(acs_col, expand)
    dt_x = f32dot(dt_col, expand)

    xdt = xs * dt_x
    cb = lax.dot_general(cm.astype(BF16), bm.astype(BF16), NT_DIMS, preferred_element_type=F32)
    xdt_b = xdt.astype(BF16)
    parts = []
    for r in range(M_HPG):
        diff = acs_col[:, r:r + 1] - acs_row[r:r + 1, :]
        lmat = jnp.exp(jnp.where(tril, diff, NEG_BIG))
        mr = (cb * lmat).astype(BF16)
        parts.append(jnp.dot(mr, xdt_b[:, r * M_HEADDIM:(r + 1) * M_HEADDIM], preferred_element_type=F32))
    y = jnp.concatenate(parts, axis=1)

    state = st_sc[...]
    y = y + jnp.exp(acs_x) * jnp.dot(cm.astype(BF16), state.astype(BF16), preferred_element_type=F32)
    last = acs_x[L - 1:L, :]
    decay_end = jnp.exp(last - acs_x)
    st_sc[...] = state * jnp.exp(last) + lax.dot_general(
        bm.astype(BF16), (xdt * decay_end).astype(BF16), TN_DIMS, preferred_element_type=F32)

    y = y + xs * dsk_ref[...]
    y = y * _silu(z_ref[0].astype(F32))
    o_ref[0] = _rms_rows(y, nw_ref[...]).astype(o_ref.dtype)


def mamba_ssd(zx, dt_raw, conv_w, conv_b, dt_bias, a_log, d_skip, norm_w, *, chunk=256):
    B, S, _ = zx.shape
    chunk = min(chunk, S)
    G, W, N = M_GROUPS, M_GROUP_W, M_DSTATE
    dt_t = dt_raw[:, :, :M_HEADS].reshape(B, S, G, M_HPG).transpose(0, 2, 3, 1)
    xoff = M_D_INNER // W
    boff = 2 * M_D_INNER // N
    coff = boff + G
    cwb = M_D_INNER // N
    conv_w = conv_w.astype(F32)
    conv_b = conv_b.reshape(1, -1).astype(F32)
    per_head = lambda t: t.reshape(G, M_HPG, 1).astype(F32)
    dsk_x = jnp.repeat(d_skip.astype(F32), M_HEADDIM).reshape(1, M_D_INNER)
    kern = functools.partial(_mamba_kernel, chunk=chunk)
    return pl.pallas_call(
        kern,
        grid=(B, G, S // chunk),
        in_specs=[
            pl.BlockSpec((1, chunk, W), lambda b, g, c: (b, c, g)),
            pl.BlockSpec((1, chunk, W), lambda b, g, c: (b, c, xoff + g)),
            pl.BlockSpec((1, chunk, N), lambda b, g, c: (b, c, boff + g)),
            pl.BlockSpec((1, chunk, N), lambda b, g, c: (b, c, coff + g)),
            pl.BlockSpec((1, 1, M_HPG, chunk), lambda b, g, c: (b, g, 0, c)),
            pl.BlockSpec((M_CONV, W), lambda b, g, c: (0, g)),
            pl.BlockSpec((M_CONV, N), lambda b, g, c: (0, cwb + g)),
            pl.BlockSpec((M_CONV, N), lambda b, g, c: (0, cwb + G + g)),
            pl.BlockSpec((1, W), lambda b, g, c: (0, g)),
            pl.BlockSpec((1, N), lambda b, g, c: (0, cwb + g)),
            pl.BlockSpec((1, N), lambda b, g, c: (0, cwb + G + g)),
            pl.BlockSpec((1, M_HPG, 1), lambda b, g, c: (g, 0, 0)),
            pl.BlockSpec((1, M_HPG, 1), lambda b, g, c: (g, 0, 0)),
            pl.BlockSpec((1, W), lambda b, g, c: (0, g)),
            pl.BlockSpec((1, W), lambda b, g, c: (0, g)),
        ],
        out_specs=pl.BlockSpec((1, chunk, W), lambda b, g, c: (b, c, g)),
        out_shape=jax.ShapeDtypeStruct((B, S, M_D_INNER), BF16),
        scratch_shapes=[
            pltpu.VMEM((N, W), F32),
            pltpu.VMEM((8, W), F32),
            pltpu.VMEM((8, N), F32),
            pltpu.VMEM((8, N), F32),
        ],
        compiler_params=_params("parallel", "parallel", "arbitrary"),
        name="mamba_ssd",
    )(zx, zx, zx, zx, dt_t, conv_w, conv_w, conv_w, conv_b, conv_b, conv_b,
      per_head(dt_bias), per_head(a_log), dsk_x, norm_w.reshape(1, M_D_INNER).astype(F32))


def _dilated_kernel(q_ref, kp_ref, kc_ref, vp_ref, vc_ref, o_ref, lse_ref):
    n = pl.program_id(2)
    T = D_SPAN
    ri = lax.broadcasted_iota(jnp.int32, (T, T), 0)
    ci = lax.broadcasted_iota(jnp.int32, (T, T), 1)
    prev_ok = (ci >= ri) & (n > 0)
    cur_ok = ci <= ri
    lane = lax.broadcasted_iota(jnp.int32, (T, LANES), 1)
    lse_tile = jnp.zeros((T, LANES), F32)
    for h in range(D_HEADS):
        hs = slice(h * D_HEAD_DIM, (h + 1) * D_HEAD_DIM)
        q = q_ref[0, :, hs]
        sp = jnp.where(prev_ok, lax.dot_general(q, kp_ref[0, :, hs], NT_DIMS, preferred_element_type=F32), NEG_BIG)
        sc = jnp.where(cur_ok, lax.dot_general(q, kc_ref[0, :, hs], NT_DIMS, preferred_element_type=F32), NEG_BIG)
        mx = jnp.maximum(jnp.max(sp, axis=-1, keepdims=True), jnp.max(sc, axis=-1, keepdims=True))
        pp = jnp.exp(sp - mx)
        pc = jnp.exp(sc - mx)
        l = jnp.sum(pp, axis=-1, keepdims=True) + jnp.sum(pc, axis=-1, keepdims=True)
        o = (jnp.dot(pp.astype(BF16), vp_ref[0, :, hs], preferred_element_type=F32)
             + jnp.dot(pc.astype(BF16), vc_ref[0, :, hs], preferred_element_type=F32))
        o_ref[0, :, hs] = (o * (1.0 / l)).astype(o_ref.dtype)
        lse_tile = jnp.where(lane == h, mx + jnp.log(l), lse_tile)
    lse_ref[0] = lse_tile


def dilated_group(qkv, g, dil):
    B, S, C = qkv.shape
    ls = S // dil
    nb = ls // D_SPAN
    wide = D_HEADS * D_HEAD_DIM
    qv = qkv.reshape(B, ls, dil * C)
    per_row = C // wide
    base = g * 3
    cur = lambda t: pl.BlockSpec((1, D_SPAN, wide), lambda b, r, n: (b, n, r * per_row + base + t))
    prev = lambda t: pl.BlockSpec((1, D_SPAN, wide), lambda b, r, n: (b, jnp.maximum(n - 1, 0), r * per_row + base + t))
    o, lse = pl.pallas_call(
        _dilated_kernel,
        grid=(B, dil, nb),
        in_specs=[cur(0), prev(1), cur(1), prev(2), cur(2)],
        out_specs=[
            pl.BlockSpec((1, D_SPAN, wide), lambda b, r, n: (b, n, r)),
            pl.BlockSpec((1, D_SPAN, LANES), lambda b, r, n: (b, n, r)),
        ],
        out_shape=[
            jax.ShapeDtypeStruct((B, ls, dil * wide), BF16),
            jax.ShapeDtypeStruct((B, ls, dil * LANES), F32),
        ],
        compiler_params=_params("parallel", "parallel", "arbitrary"),
        name="dilated_attention",
    )(qv, qv, qv, qv, qv)
    return o.reshape(B * S, wide), lse.reshape(B * S, LANES)


def _combine_kernel(o0_ref, o1_ref, o2_ref, l0_ref, l1_ref, l2_ref, o_ref):
    l0, l1, l2 = l0_ref[...], l1_ref[...], l2_ref[...]
    mx = jnp.maximum(jnp.maximum(l0, l1), l2)
    e0, e1, e2 = jnp.exp(l0 - mx), jnp.exp(l1 - mx), jnp.exp(l2 - mx)
    inv = 1.0 / (e0 + e1 + e2)
    ws = (e0 * inv, e1 * inv, e2 * inv)
    for h in range(D_HEADS):
        hs = slice(h * D_HEAD_DIM, (h + 1) * D_HEAD_DIM)
        acc = ws[0][:, h:h + 1] * o0_ref[:, hs].astype(F32)
        acc = acc + ws[1][:, h:h + 1] * o1_ref[:, hs].astype(F32)
        acc = acc + ws[2][:, h:h + 1] * o2_ref[:, hs].astype(F32)
        o_ref[:, hs] = acc.astype(o_ref.dtype)


def combine_groups(outs, lses, *, tm=512):
    T, W = outs[0].shape
    tm = min(tm, T)
    wide = pl.BlockSpec((tm, W), lambda i: (i, 0))
    narrow = pl.BlockSpec((tm, LANES), lambda i: (i, 0))
    return pl.pallas_call(
        _combine_kernel,
        grid=(T // tm,),
        in_specs=[wide, wide, wide, narrow, narrow, narrow],
        out_specs=wide,
        out_shape=jax.ShapeDtypeStruct((T, W), BF16),
        compiler_params=_params("parallel"),
        name="combine_groups",
    )(*outs, *lses)


def _deinterleave_heads(w, heads, dim):
    k = w.shape[0]
    return w.reshape(k, heads, dim // 2, 2).transpose(0, 1, 3, 2).reshape(k, heads * dim)


def mixer_a(xr, B, S, nw, a_w_in, a_q_norm_w, a_k_norm_w, lq1, lk1, lq2, lk2, a_subln_w, a_w_out, *, layer_idx):
    T, D = xr.shape
    colw = jnp.concatenate([
        jnp.tile(a_q_norm_w.astype(F32) * (A_HEAD_DIM ** -0.5), 2 * A_HEADS),
        jnp.tile(a_k_norm_w.astype(F32), 2 * A_HEADS),
        jnp.ones((D,), F32)]).reshape(1, 3 * D)
    qkv = norm_matmul(xr, nw, a_w_in.astype(BF16), colw)
    o = diff_attention(qkv.reshape(B, S, 3 * D), lq1, lk1, lq2, lk2, a_subln_w, layer_idx=layer_idx)
    return matmul_residual(o.reshape(T, D), a_w_out.astype(BF16), xr)


def mixer_b(xr, B, S, nw, b_w_in, b_gn_w, b_w_out):
    T, D = xr.shape
    w_in = jnp.concatenate([
        _deinterleave_heads(b_w_in[:, :D], R_HEADS, R_KDIM),
        _deinterleave_heads(b_w_in[:, D:2 * D], R_HEADS, R_KDIM),
        b_w_in[:, 2 * D:]], axis=1)
    proj = norm_matmul(xr, nw, w_in.astype(BF16))
    o = retention(proj.reshape(B, S, -1), b_gn_w)
    return matmul_residual(o.reshape(T, -1), b_w_out.astype(BF16), xr)


def mixer_c(xr, B, S, nw, c_w_in, c_conv_w, c_conv_b, c_dt_bias, c_a_log, c_d_skip, c_norm_w, c_w_out):
    T, D = xr.shape
    n_main = 2 * M_D_INNER + 2 * M_GROUPS * M_DSTATE
    zx = norm_matmul(xr, nw, c_w_in[:, :n_main].astype(BF16))
    w_dt = jnp.pad(c_w_in[:, n_main:], ((0, 0), (0, LANES - M_HEADS)))
    dt_raw = norm_matmul(xr, nw, w_dt.astype(BF16), tn=LANES, out_dtype=F32)
    y = mamba_ssd(zx.reshape(B, S, n_main), dt_raw.reshape(B, S, LANES), c_conv_w, c_conv_b, c_dt_bias,
                  c_a_log, c_d_skip, c_norm_w)
    return matmul_residual(y.reshape(T, M_D_INNER), c_w_out.astype(BF16), xr)


def mixer_d(xr, B, S, nw, d_w_in, d_q_norm_w, d_k_norm_w, d_w_out):
    T, D = xr.shape
    scale = D_HEAD_DIM ** -0.5
    colw = jnp.concatenate([
        jnp.concatenate([jnp.tile(d_q_norm_w[g].astype(F32) * scale, D_HEADS),
                         jnp.tile(d_k_norm_w[g].astype(F32), D_HEADS),
                         jnp.ones((D,), F32)])
        for g in range(len(D_PATTERNS))]).reshape(1, -1)
    qkv = norm_matmul(xr, nw, d_w_in.astype(BF16), colw).reshape(B, S, -1)
    outs, lses = zip(*[dilated_group(qkv, g, dil) for g, (_, dil) in enumerate(D_PATTERNS)])
    o = combine_groups(outs, lses)
    return matmul_residual(o, d_w_out.astype(BF16), xr)


@jax.jit
def kernel(x, norm1_w, norm2_w, mlp_w1, mlp_w2, a_w_in, a_q_norm_w, a_k_norm_w, a_lambda_q1, a_lambda_k1, a_lambda_q2, a_lambda_k2, a_subln_w, a_w_out, b_w_in, b_gn_w, b_w_out, c_w_in, c_conv_w, c_conv_b, c_dt_bias, c_a_log, c_d_skip, c_norm_w, c_w_out, d_w_in, d_q_norm_w, d_k_norm_w, d_w_out):
    B, S, D = x.shape
    xr = x.reshape(B * S, D)
    ffn = lambda t, i: mlp(t, norm2_w[i], mlp_w1[i].astype(BF16), mlp_w2[i].astype(BF16))
    xr = mixer_a(xr, B, S, norm1_w[0], a_w_in, a_q_norm_w, a_k_norm_w, a_lambda_q1, a_lambda_k1,
                 a_lambda_q2, a_lambda_k2, a_subln_w, a_w_out, layer_idx=0)
    xr = ffn(xr, 0)
    xr = mixer_b(xr, B, S, norm1_w[1], b_w_in, b_gn_w, b_w_out)
    xr = ffn(xr, 1)
    xr = mixer_c(xr, B, S, norm1_w[2], c_w_in, c_conv_w, c_conv_b, c_dt_bias, c_a_log, c_d_skip, c_norm_w,
                 c_w_out)
    xr = ffn(xr, 2)
    xr = mixer_d(xr, B, S, norm1_w[3], d_w_in, d_q_norm_w, d_k_norm_w, d_w_out)
    xr = ffn(xr, 3)
    return xr.reshape(B, S, D)
```

```python
import functools
import math

import jax
import jax.numpy as jnp
from jax import lax
from jax.experimental import pallas as pl
from jax.experimental.pallas import tpu as pltpu

F32 = jnp.float32
BF16 = jnp.bfloat16

EPS = 1e-6
D_MODEL = 2048
D_FF = 4 * D_MODEL
LANES = 128

A_HEAD_DIM = 128
A_HEADS = 8
R_HEADS = 8
R_KDIM = 256
R_VDIM = 512
M_D_INNER = 4096
M_HEADDIM = 64
M_HEADS = 64
M_GROUPS = 8
M_DSTATE = 128
M_CONV = 4
M_GROUP_W = M_D_INNER // M_GROUPS
M_HPG = M_HEADS // M_GROUPS
D_HEADS = 16
D_HEAD_DIM = 128
D_PATTERNS = ((128, 1), (512, 4), (2048, 16))
D_SPAN = 128

VMEM_LIMIT_BYTES = 56 * 1024 * 1024
NEG_BIG = -1e30

NT_DIMS = (((1,), (1,)), ((), ()))
TN_DIMS = (((0,), (0,)), ((), ()))


def _params(*sem):
    return pltpu.CompilerParams(dimension_semantics=sem, vmem_limit_bytes=VMEM_LIMIT_BYTES)


def _silu(v):
    return v * (1.0 / (1.0 + jnp.exp(-v)))


def _rms_rows(v, w):
    ms = jnp.mean(v * v, axis=-1, keepdims=True)
    return v * lax.rsqrt(ms + EPS) * w


def _split3(v):
    p1 = v.astype(BF16)
    r1 = v - p1.astype(F32)
    p2 = r1.astype(BF16)
    return p1, p2, (r1 - p2.astype(F32)).astype(BF16)


def _norm_matmul_kernel(x_ref, nw_ref, w_ref, cw_ref, o_ref, h_sc, *, tn, head_norm, period, count):
    j = pl.program_id(1)

    @pl.when(j == 0)
    def _():
        h_sc[...] = _rms_rows(x_ref[...], nw_ref[...]).astype(BF16)

    acc = jnp.dot(h_sc[...], w_ref[...], preferred_element_type=F32)
    if not head_norm:
        o_ref[...] = acc.astype(o_ref.dtype)
        return
    is_norm = (j % period) < count

    @pl.when(is_norm)
    def _():
        for c in range(tn // LANES):
            sl = slice(c * LANES, (c + 1) * LANES)
            o_ref[:, sl] = _rms_rows(acc[:, sl], cw_ref[:, sl]).astype(o_ref.dtype)

    @pl.when(jnp.logical_not(is_norm))
    def _():
        o_ref[...] = acc.astype(o_ref.dtype)


def norm_matmul(x, nw, w, colw=None, *, tm=1024, tn=1024, out_dtype=BF16, norm_cols_per_2048=0):
    T, K = x.shape
    N = w.shape[1]
    tm = min(tm, T)
    tn = min(tn, N)
    head_norm = colw is not None
    if colw is None:
        colw = jnp.ones((1, N), F32)
    period = 3 * D_MODEL // tn
    count = 2 * D_MODEL // tn
    kern = functools.partial(_norm_matmul_kernel, tn=tn, head_norm=head_norm, period=period, count=count)
    return pl.pallas_call(
        kern,
        grid=(T // tm, N // tn),
        in_specs=[
            pl.BlockSpec((tm, K), lambda i, j: (i, 0)),
            pl.BlockSpec((1, K), lambda i, j: (0, 0)),
            pl.BlockSpec((K, tn), lambda i, j: (0, j)),
            pl.BlockSpec((1, tn), lambda i, j: (0, j)),
        ],
        out_specs=pl.BlockSpec((tm, tn), lambda i, j: (i, j)),
        out_shape=jax.ShapeDtypeStruct((T, N), out_dtype),
        scratch_shapes=[pltpu.VMEM((tm, K), BF16)],
        compiler_params=_params("parallel", "arbitrary"),
        name="norm_matmul",
    )(x, nw.reshape(1, K), w, colw)


def _matmul_residual_kernel(a_ref, w_ref, r_ref, o_ref):
    o_ref[...] = r_ref[...] + jnp.dot(a_ref[...], w_ref[...], preferred_element_type=F32)


def matmul_residual(a, w, res, *, tm=512, tn=1024):
    T, K = a.shape
    N = w.shape[1]
    tm = min(tm, T)
    return pl.pallas_call(
        _matmul_residual_kernel,
        grid=(T // tm, N // tn),
        in_specs=[
            pl.BlockSpec((tm, K), lambda i, j: (i, 0)),
            pl.BlockSpec((K, tn), lambda i, j: (0, j)),
            pl.BlockSpec((tm, tn), lambda i, j: (i, j)),
        ],
        out_specs=pl.BlockSpec((tm, tn), lambda i, j: (i, j)),
        out_shape=jax.ShapeDtypeStruct((T, N), F32),
        compiler_params=_params("parallel", "arbitrary"),
        name="matmul_residual",
    )(a, w, res)


def _mlp_kernel(x_ref, nw_ref, w1_ref, w2_ref, o_ref, h_sc):
    f = pl.program_id(1)

    @pl.when(f == 0)
    def _():
        x = x_ref[...]
        h_sc[...] = _rms_rows(x, nw_ref[...]).astype(BF16)
        o_ref[...] = x

    u = jnp.dot(h_sc[...], w1_ref[...], preferred_element_type=F32)
    u = jnp.square(jnp.maximum(u, 0.0)).astype(BF16)
    o_ref[...] += jnp.dot(u, w2_ref[...], preferred_element_type=F32)


def mlp(x, nw, w1, w2, *, tm=512, tf=1024):
    T, D = x.shape
    FF = w1.shape[1]
    tm = min(tm, T)
    return pl.pallas_call(
        _mlp_kernel,
        grid=(T // tm, FF // tf),
        in_specs=[
            pl.BlockSpec((tm, D), lambda i, f: (i, 0)),
            pl.BlockSpec((1, D), lambda i, f: (0, 0)),
            pl.BlockSpec((D, tf), lambda i, f: (0, f)),
            pl.BlockSpec((tf, D), lambda i, f: (f, 0)),
        ],
        out_specs=pl.BlockSpec((tm, D), lambda i, f: (i, 0)),
        out_shape=jax.ShapeDtypeStruct((T, D), F32),
        scratch_shapes=[pltpu.VMEM((tm, D), BF16)],
        compiler_params=_params("parallel", "arbitrary"),
        name="mlp",
    )(x, nw.reshape(1, D), w1, w2)


def _diff_attn_kernel(q_ref, k_ref, v_ref, lq1_ref, lk1_ref, lq2_ref, lk2_ref, sw_ref, o_ref,
                      m_sc, l_sc, acc_sc, *, tq, lambda_init):
    qi = pl.program_id(2)
    m_sc[...] = jnp.full_like(m_sc, NEG_BIG)
    l_sc[...] = jnp.zeros_like(l_sc)
    acc_sc[...] = jnp.zeros_like(acc_sc)
    row = lax.broadcasted_iota(jnp.int32, (tq, tq), 0)
    col = lax.broadcasted_iota(jnp.int32, (tq, tq), 1)

    def scores(m, start):
        hs = slice(m * A_HEAD_DIM, (m + 1) * A_HEAD_DIM)
        return lax.dot_general(q_ref[0, :, hs], k_ref[0, pl.ds(start, tq), hs], NT_DIMS,
                               preferred_element_type=F32)

    def update(m, s, start, masked):
        if masked:
            s = jnp.where(col <= row, s, NEG_BIG)
        m_prev = m_sc[m]
        m_new = jnp.maximum(m_prev, jnp.max(s, axis=-1, keepdims=True))
        alpha = jnp.exp2(m_prev - m_new)
        p = jnp.exp2(s - m_new)
        m_sc[m] = m_new
        l_sc[m] = alpha * l_sc[m] + jnp.sum(p, axis=-1, keepdims=True)
        acc_sc[m] = alpha * acc_sc[m] + jnp.dot(p.astype(BF16), v_ref[0, pl.ds(start, tq), :],
                                                preferred_element_type=F32)

    def attend(j, masked):
        start = pl.multiple_of(j * tq, tq)
        s0 = scores(0, start)
        s1 = scores(1, start)
        update(0, s0, start, masked)
        update(1, s1, start, masked)

    def body(j, carry):
        attend(j, False)
        return carry

    lax.fori_loop(0, qi, body, 0)
    attend(qi, True)
    lam = (jnp.exp(jnp.sum(lq1_ref[...] * lk1_ref[...], axis=-1, keepdims=True))
           - jnp.exp(jnp.sum(lq2_ref[...] * lk2_ref[...], axis=-1, keepdims=True)) + lambda_init)
    o = acc_sc[0] * (1.0 / l_sc[0]) - lam * (acc_sc[1] * (1.0 / l_sc[1]))
    o_ref[0] = (_rms_rows(o, sw_ref[...]) * (1.0 - lambda_init)).astype(o_ref.dtype)


def diff_attention(qkv, lq1, lk1, lq2, lk2, subln_w, *, layer_idx, tq=512):
    B, S, _ = qkv.shape
    tq = min(tq, S)
    lambda_init = 0.8 - 0.6 * math.exp(-0.3 * layer_idx)
    pw = 2 * A_HEAD_DIM
    vec = lambda a: a.reshape(1, -1).astype(F32)
    small = lambda n: pl.BlockSpec((1, n), lambda b, h, i: (0, 0))
    kern = functools.partial(_diff_attn_kernel, tq=tq, lambda_init=lambda_init)
    return pl.pallas_call(
        kern,
        grid=(B, A_HEADS, S // tq),
        in_specs=[
            pl.BlockSpec((1, tq, pw), lambda b, h, i: (b, i, h)),
            pl.BlockSpec((1, S, pw), lambda b, h, i: (b, 0, A_HEADS + h)),
            pl.BlockSpec((1, S, pw), lambda b, h, i: (b, 0, 2 * A_HEADS + h)),
            small(A_HEAD_DIM), small(A_HEAD_DIM), small(A_HEAD_DIM), small(A_HEAD_DIM), small(pw),
        ],
        out_specs=pl.BlockSpec((1, tq, pw), lambda b, h, i: (b, i, h)),
        out_shape=jax.ShapeDtypeStruct((B, S, D_MODEL), BF16),
        scratch_shapes=[
            pltpu.VMEM((2, tq, 1), F32),
            pltpu.VMEM((2, tq, 1), F32),
            pltpu.VMEM((2, tq, pw), F32),
        ],
        compiler_params=_params("parallel", "parallel", "arbitrary"),
        name="diff_attention",
    )(qkv, qkv, qkv, vec(lq1), vec(lk1), vec(lq2), vec(lk2), vec(subln_w))


def _retention_kernel(q_ref, k_ref, v_ref, g_ref, cos_ref, sin_ref, dm_ref, xi_ref, zeta_ref, cd_ref,
                      gw_ref, o_ref, r_sc, *, tr, chunk):
    @pl.when(pl.program_id(2) == 0)
    def _():
        r_sc[...] = jnp.zeros_like(r_sc)

    half = R_KDIM // 2
    dmask = dm_ref[0]
    xi = xi_ref[0]
    zeta = zeta_ref[0]
    cdecay = cd_ref[0, 0:1, 0:1]
    gw = gw_ref[0]

    def rotate(t, c, s):
        te, to = t[:, :half], t[:, half:]
        return te * c - to * s, to * c + te * s

    for ci in range(tr // chunk):
        rows = slice(ci * chunk, (ci + 1) * chunk)
        c = cos_ref[rows, :]
        s = sin_ref[rows, :]
        qe, qo = rotate(q_ref[0, rows, :].astype(F32), c, s)
        ke, ko = rotate(k_ref[0, rows, :].astype(F32) * (R_KDIM ** -0.5), c, s)
        v = v_ref[0, rows, :]
        q_r = jnp.concatenate([qe, qo], axis=1).astype(BF16)
        k_r = jnp.concatenate([ke, ko], axis=1).astype(BF16)
        q_x = jnp.concatenate([qe * xi, qo * xi], axis=1).astype(BF16)
        k_z = jnp.concatenate([ke * zeta, ko * zeta], axis=1).astype(BF16)
        r_old = r_sc[...]
        sc = lax.dot_general(q_r, k_r, NT_DIMS, preferred_element_type=F32) * dmask
        o = (jnp.dot(sc.astype(BF16), v, preferred_element_type=F32)
             + jnp.dot(q_x, r_old.astype(BF16), preferred_element_type=F32))
        r_sc[...] = r_old * cdecay + lax.dot_general(k_z, v, TN_DIMS, preferred_element_type=F32)
        gate = _silu(g_ref[0, rows, :].astype(F32))
        o_ref[0, rows, :] = (gate * _rms_rows(o, gw)).astype(o_ref.dtype)


def retention(proj, gn_w, *, tr=1024, chunk=256):
    B, S, _ = proj.shape
    tr = min(tr, S)
    chunk = min(chunk, tr)
    half = R_KDIM // 2
    angle = 1.0 / (10000.0 ** jnp.linspace(0.0, 1.0, half, dtype=F32))
    ang = jnp.arange(S, dtype=F32)[:, None] * angle[None, :]
    cos, sin = jnp.cos(ang), jnp.sin(ang)
    log_g = jnp.log(1.0 - 2.0 ** (-5.0 - jnp.arange(R_HEADS, dtype=F32)))
    idx = jnp.arange(chunk, dtype=F32)
    rel = idx[:, None] - idx[None, :]
    dmask = jnp.where(rel[None] >= 0, jnp.exp(jnp.maximum(rel, 0.0)[None] * log_g[:, None, None]), 0.0)
    xi = jnp.exp((idx + 1.0)[None, :] * log_g[:, None])
    zeta = jnp.exp((chunk - 1.0 - idx)[None, :] * log_g[:, None])
    cdecay = jnp.exp(chunk * log_g)
    bc = lambda t: jnp.broadcast_to(t[:, :, None], (R_HEADS, chunk, half))
    cd = jnp.broadcast_to(cdecay[:, None, None], (R_HEADS, 8, LANES))
    nq = D_MODEL // R_KDIM
    nv = 2 * D_MODEL // R_VDIM
    kern = functools.partial(_retention_kernel, tr=tr, chunk=chunk)
    head = lambda shape: pl.BlockSpec(shape, lambda b, h, i: (h, 0, 0))
    return pl.pallas_call(
        kern,
        grid=(B, R_HEADS, S // tr),
        in_specs=[
            pl.BlockSpec((1, tr, R_KDIM), lambda b, h, i: (b, i, h)),
            pl.BlockSpec((1, tr, R_KDIM), lambda b, h, i: (b, i, nq + h)),
            pl.BlockSpec((1, tr, R_VDIM), lambda b, h, i: (b, i, nv + h)),
            pl.BlockSpec((1, tr, R_VDIM), lambda b, h, i: (b, i, nv + R_HEADS + h)),
            pl.BlockSpec((tr, half), lambda b, h, i: (i, 0)),
            pl.BlockSpec((tr, half), lambda b, h, i: (i, 0)),
            head((1, chunk, chunk)), head((1, chunk, half)), head((1, chunk, half)),
            head((1, 8, LANES)), head((1, 1, R_VDIM)),
        ],
        out_specs=pl.BlockSpec((1, tr, R_VDIM), lambda b, h, i: (b, i, h)),
        out_shape=jax.ShapeDtypeStruct((B, S, R_HEADS * R_VDIM), BF16),
        scratch_shapes=[pltpu.VMEM((R_KDIM, R_VDIM), F32)],
        compiler_params=_params("parallel", "parallel", "arbitrary"),
        name="retention",
    )(proj, proj, proj, proj, cos, sin, dmask, bc(xi), bc(zeta), cd, gn_w.reshape(R_HEADS, 1, R_VDIM).astype(F32))


def _mamba_kernel(z_ref, x_ref, b_ref, c_ref, dt_ref, wx_ref, wb_ref, wc_ref, bx_ref, bb_ref, bc_ref,
                  dtb_ref, alog_ref, dsk_ref, nw_ref, o_ref,
                  st_sc, tx_sc, tb_sc, tc_sc, *, chunk):
    L = chunk

    @pl.when(pl.program_id(2) == 0)
    def _():
        st_sc[...] = jnp.zeros_like(st_sc)
        tx_sc[...] = jnp.zeros_like(tx_sc)
        tb_sc[...] = jnp.zeros_like(tb_sc)
        tc_sc[...] = jnp.zeros_like(tc_sc)

    def conv_silu(cur_ref, tail_sc, w_ref, bias_ref):
        cur = cur_ref[0].astype(F32)
        ext = jnp.concatenate([tail_sc[...], cur], axis=0)
        tail_sc[...] = cur[L - 8:, :]
        w = w_ref[...]
        acc = bias_ref[...] + cur * w[M_CONV - 1:M_CONV, :]
        for kk in range(M_CONV - 1):
            sh = M_CONV - 1 - kk
            acc = acc + ext[8 - sh:8 - sh + L, :] * w[kk:kk + 1, :]
        return _silu(acc)

    xs = conv_silu(x_ref, tx_sc, wx_ref, bx_ref)
    bm = conv_silu(b_ref, tb_sc, wb_ref, bb_ref)
    cm = conv_silu(c_ref, tc_sc, wc_ref, bc_ref)

    raw = dt_ref[0, 0] + dtb_ref[0]
    dt_t = jnp.maximum(raw, 0.0) + jnp.log(1.0 + jnp.exp(-jnp.abs(raw)))
    a_t = dt_t * (-jnp.exp(alog_ref[0]))

    ri = lax.broadcasted_iota(jnp.int32, (L, L), 0)
    ci = lax.broadcasted_iota(jnp.int32, (L, L), 1)
    tril = ci <= ri
    eye = jnp.where(ri == ci, 1.0, 0.0).astype(BF16)
    lower = jnp.where(tril, 1.0, 0.0).astype(BF16)
    upper = jnp.where(ri <= ci, 1.0, 0.0).astype(BF16)

    def exact_dot(mat01, v, dims):
        parts = _split3(v)
        if dims is None:
            return sum(jnp.dot(p, mat01, preferred_element_type=F32) for p in parts)
        return sum(lax.dot_general(mat01, p, dims, preferred_element_type=F32) for p in parts)

    def per_column(t):
        return jnp.concatenate([jnp.broadcast_to(t[r:r + 1, :], (M_HEADDIM, L)) for r in range(M_HPG)], axis=0)

    acs_row = exact_dot(upper, a_t, None)
    acs_x = exact_dot(lower, per_column(a_t), NT_DIMS)
    dt_x = exact_dot(eye, per_column(dt_t), NT_DIMS)

    xdt = xs * dt_x
    cb = lax.dot_general(cm.astype(BF16), bm.astype(BF16), NT_DIMS, preferred_element_type=F32)
    xdt_b = xdt.astype(BF16)
    parts = []
    for r in range(M_HPG):
        diff = acs_x[:, r * M_HEADDIM:r * M_HEADDIM + 1] - acs_row[r:r + 1, :]
        lmat = jnp.exp(jnp.where(tril, diff, NEG_BIG))
        mr = (cb * lmat).astype(BF16)
        parts.append(jnp.dot(mr, xdt_b[:, r * M_HEADDIM:(r + 1) * M_HEADDIM], preferred_element_type=F32))
    y = jnp.concatenate(parts, axis=1)

    state = st_sc[...]
    y = y + jnp.exp(acs_x) * jnp.dot(cm.astype(BF16), state.astype(BF16), preferred_element_type=F32)
    last = acs_x[L - 1:L, :]
    decay_end = jnp.exp(last - acs_x)
    st_sc[...] = state * jnp.exp(last) + lax.dot_general(
        bm.astype(BF16), (xdt * decay_end).astype(BF16), TN_DIMS, preferred_element_type=F32)

    y = y + xs * dsk_ref[...]
    y = y * _silu(z_ref[0].astype(F32))
    o_ref[0] = _rms_rows(y, nw_ref[...]).astype(o_ref.dtype)


def mamba_ssd(zx, dt_raw, conv_w, conv_b, dt_bias, a_log, d_skip, norm_w, *, chunk=256):
    B, S, _ = zx.shape
    chunk = min(chunk, S)
    G, W, N = M_GROUPS, M_GROUP_W, M_DSTATE
    dt_t = dt_raw[:, :, :M_HEADS].reshape(B, S, G, M_HPG).transpose(0, 2, 3, 1)
    xoff = M_D_INNER // W
    boff = 2 * M_D_INNER // N
    coff = boff + G
    cwb = M_D_INNER // N
    conv_w = conv_w.astype(F32)
    conv_b = conv_b.reshape(1, -1).astype(F32)
    per_head = lambda t: t.reshape(G, M_HPG, 1).astype(F32)
    dsk_x = jnp.repeat(d_skip.astype(F32), M_HEADDIM).reshape(1, M_D_INNER)
    kern = functools.partial(_mamba_kernel, chunk=chunk)
    return pl.pallas_call(
        kern,
        grid=(B, G, S // chunk),
        in_specs=[
            pl.BlockSpec((1, chunk, W), lambda b, g, c: (b, c, g)),
            pl.BlockSpec((1, chunk, W), lambda b, g, c: (b, c, xoff + g)),
            pl.BlockSpec((1, chunk, N), lambda b, g, c: (b, c, boff + g)),
            pl.BlockSpec((1, chunk, N), lambda b, g, c: (b, c, coff + g)),
            pl.BlockSpec((1, 1, M_HPG, chunk), lambda b, g, c: (b, g, 0, c)),
            pl.BlockSpec((M_CONV, W), lambda b, g, c: (0, g)),
            pl.BlockSpec((M_CONV, N), lambda b, g, c: (0, cwb + g)),
            pl.BlockSpec((M_CONV, N), lambda b, g, c: (0, cwb + G + g)),
            pl.BlockSpec((1, W), lambda b, g, c: (0, g)),
            pl.BlockSpec((1, N), lambda b, g, c: (0, cwb + g)),
            pl.BlockSpec((1, N), lambda b, g, c: (0, cwb + G + g)),
            pl.BlockSpec((1, M_HPG, 1), lambda b, g, c: (g, 0, 0)),
            pl.BlockSpec((1, M_HPG, 1), lambda b, g, c: (g, 0, 0)),
            pl.BlockSpec((1, W), lambda b, g, c: (0, g)),
            pl.BlockSpec((1, W), lambda b, g, c: (0, g)),
        ],
        out_specs=pl.BlockSpec((1, chunk, W), lambda b, g, c: (b, c, g)),
        out_shape=jax.ShapeDtypeStruct((B, S, M_D_INNER), BF16),
        scratch_shapes=[
            pltpu.VMEM((N, W), F32),
            pltpu.VMEM((8, W), F32),
            pltpu.VMEM((8, N), F32),
            pltpu.VMEM((8, N), F32),
        ],
        compiler_params=_params("parallel", "parallel", "arbitrary"),
        name="mamba_ssd",
    )(zx, zx, zx, zx, dt_t, conv_w, conv_w, conv_w, conv_b, conv_b, conv_b,
      per_head(dt_bias), per_head(a_log), dsk_x, norm_w.reshape(1, M_D_INNER).astype(F32))


def _dilated_kernel(q_ref, kp_ref, kc_ref, vp_ref, vc_ref, o_ref, lse_ref, q_sc, k_sc, v_sc, o_sc, lse_sc,
                    *, dil, tiles):
    n = pl.program_id(1)
    h = pl.program_id(2)
    T = D_SPAN
    span = T * dil
    q_sc[...] = q_ref[0].astype(F32)
    k_sc[0:span, :] = kp_ref[0].astype(F32)
    k_sc[span:, :] = kc_ref[0].astype(F32)
    v_sc[0:span, :] = vp_ref[0].astype(F32)
    v_sc[span:, :] = vc_ref[0].astype(F32)

    @pl.when(h == 0)
    def _():
        lse_sc[...] = jnp.zeros_like(lse_sc)

    ri = lax.broadcasted_iota(jnp.int32, (T, T), 0)
    ci = lax.broadcasted_iota(jnp.int32, (T, T), 1)
    lane = lax.broadcasted_iota(jnp.int32, (T, LANES), 1)

    def tile(idx, carry):
        i = idx // dil
        start = i * span + idx % dil
        rows = lambda ref, off: ref[pl.ds(start + off, T, stride=dil), :]
        q = rows(q_sc, 0).astype(BF16)
        has_prev = jnp.logical_or(n > 0, i > 0)
        sp = lax.dot_general(q, rows(k_sc, 0).astype(BF16), NT_DIMS, preferred_element_type=F32)
        sc = lax.dot_general(q, rows(k_sc, span).astype(BF16), NT_DIMS, preferred_element_type=F32)
        sp = jnp.where((ci >= ri) & has_prev, sp, NEG_BIG)
        sc = jnp.where(ci <= ri, sc, NEG_BIG)
        mx = jnp.maximum(jnp.max(sp, axis=-1, keepdims=True), jnp.max(sc, axis=-1, keepdims=True))
        pp = jnp.exp(sp - mx)
        pc = jnp.exp(sc - mx)
        l = jnp.sum(pp, axis=-1, keepdims=True) + jnp.sum(pc, axis=-1, keepdims=True)
        o = (jnp.dot(pp.astype(BF16), rows(v_sc, 0).astype(BF16), preferred_element_type=F32)
             + jnp.dot(pc.astype(BF16), rows(v_sc, span).astype(BF16), preferred_element_type=F32))
        o_sc[pl.ds(start, T, stride=dil), :] = o * (1.0 / l)
        lse_sc[pl.ds(start, T, stride=dil), :] = jnp.where(lane == h, mx + jnp.log(l), rows(lse_sc, 0))
        return carry

    lax.fori_loop(0, dil * tiles, tile, 0)
    o_ref[0] = o_sc[...].astype(o_ref.dtype)

    @pl.when(h == D_HEADS - 1)
    def _():
        lse_ref[0] = lse_sc[...]


def dilated_group(qkv, g, dil, *, tiles):
    B, S, C = qkv.shape
    span = D_SPAN * dil
    tb = span * tiles
    hd = D_HEAD_DIM
    col = lambda t: (g * 3 + t) * D_HEADS
    cur = lambda t: pl.BlockSpec((1, tb, hd), lambda b, n, h: (b, n, col(t) + h))
    prev = lambda t: pl.BlockSpec((1, span, hd), lambda b, n, h: (b, jnp.maximum(n * tiles - 1, 0), col(t) + h))
    kern = functools.partial(_dilated_kernel, dil=dil, tiles=tiles)
    o, lse = pl.pallas_call(
        kern,
        grid=(B, S // tb, D_HEADS),
        in_specs=[cur(0), prev(1), cur(1), prev(2), cur(2)],
        out_specs=[
            pl.BlockSpec((1, tb, hd), lambda b, n, h: (b, n, h)),
            pl.BlockSpec((1, tb, LANES), lambda b, n, h: (b, n, 0)),
        ],
        out_shape=[
            jax.ShapeDtypeStruct((B, S, D_HEADS * hd), BF16),
            jax.ShapeDtypeStruct((B, S, LANES), F32),
        ],
        scratch_shapes=[
            pltpu.VMEM((tb, hd), F32),
            pltpu.VMEM((span + tb, hd), F32),
            pltpu.VMEM((span + tb, hd), F32),
            pltpu.VMEM((tb, hd), F32),
            pltpu.VMEM((tb, LANES), F32),
        ],
        compiler_params=_params("parallel", "parallel", "arbitrary"),
        name="dilated_attention",
    )(qkv, qkv, qkv, qkv, qkv)
    return o.reshape(B * S, D_HEADS * hd), lse.reshape(B * S, LANES)


def _combine_kernel(o0_ref, o1_ref, o2_ref, l0_ref, l1_ref, l2_ref, o_ref):
    l0, l1, l2 = l0_ref[...], l1_ref[...], l2_ref[...]
    mx = jnp.maximum(jnp.maximum(l0, l1), l2)
    e0, e1, e2 = jnp.exp(l0 - mx), jnp.exp(l1 - mx), jnp.exp(l2 - mx)
    inv = 1.0 / (e0 + e1 + e2)
    ws = (e0 * inv, e1 * inv, e2 * inv)
    for h in range(D_HEADS):
        hs = slice(h * D_HEAD_DIM, (h + 1) * D_HEAD_DIM)
        acc = ws[0][:, h:h + 1] * o0_ref[:, hs].astype(F32)
        acc = acc + ws[1][:, h:h + 1] * o1_ref[:, hs].astype(F32)
        acc = acc + ws[2][:, h:h + 1] * o2_ref[:, hs].astype(F32)
        o_ref[:, hs] = acc.astype(o_ref.dtype)


def combine_groups(outs, lses, *, tm=512):
    T, W = outs[0].shape
    tm = min(tm, T)
    wide = pl.BlockSpec((tm, W), lambda i: (i, 0))
    narrow = pl.BlockSpec((tm, LANES), lambda i: (i, 0))
    return pl.pallas_call(
        _combine_kernel,
        grid=(T // tm,),
        in_specs=[wide, wide, wide, narrow, narrow, narrow],
        out_specs=wide,
        out_shape=jax.ShapeDtypeStruct((T, W), BF16),
        compiler_params=_params("parallel"),
        name="combine_groups",
    )(*outs, *lses)


def _deinterleave_heads(w, heads, dim):
    k = w.shape[0]
    return w.reshape(k, heads, dim // 2, 2).transpose(0, 1, 3, 2).reshape(k, heads * dim)


def mixer_a(xr, B, S, nw, a_w_in, a_q_norm_w, a_k_norm_w, lq1, lk1, lq2, lk2, a_subln_w, a_w_out, *, layer_idx):
    T, D = xr.shape
    colw = jnp.concatenate([
        jnp.tile(a_q_norm_w.astype(F32) * (A_HEAD_DIM ** -0.5 * math.log2(math.e)), 2 * A_HEADS),
        jnp.tile(a_k_norm_w.astype(F32), 2 * A_HEADS),
        jnp.ones((D,), F32)]).reshape(1, 3 * D)
    qkv = norm_matmul(xr, nw, a_w_in.astype(BF16), colw)
    o = diff_attention(qkv.reshape(B, S, 3 * D), lq1, lk1, lq2, lk2, a_subln_w, layer_idx=layer_idx)
    return matmul_residual(o.reshape(T, D), a_w_out.astype(BF16), xr)


def mixer_b(xr, B, S, nw, b_w_in, b_gn_w, b_w_out):
    T, D = xr.shape
    w_in = jnp.concatenate([
        _deinterleave_heads(b_w_in[:, :D], R_HEADS, R_KDIM),
        _deinterleave_heads(b_w_in[:, D:2 * D], R_HEADS, R_KDIM),
        b_w_in[:, 2 * D:]], axis=1)
    proj = norm_matmul(xr, nw, w_in.astype(BF16))
    o = retention(proj.reshape(B, S, -1), b_gn_w)
    return matmul_residual(o.reshape(T, -1), b_w_out.astype(BF16), xr)


def mixer_c(xr, B, S, nw, c_w_in, c_conv_w, c_conv_b, c_dt_bias, c_a_log, c_d_skip, c_norm_w, c_w_out):
    T, D = xr.shape
    n_main = 2 * M_D_INNER + 2 * M_GROUPS * M_DSTATE
    zx = norm_matmul(xr, nw, c_w_in[:, :n_main].astype(BF16))
    w_dt = jnp.pad(c_w_in[:, n_main:], ((0, 0), (0, LANES - M_HEADS)))
    dt_raw = norm_matmul(xr, nw, w_dt.astype(BF16), tn=LANES, out_dtype=F32)
    y = mamba_ssd(zx.reshape(B, S, n_main), dt_raw.reshape(B, S, LANES), c_conv_w, c_conv_b, c_dt_bias,
                  c_a_log, c_d_skip, c_norm_w)
    return matmul_residual(y.reshape(T, M_D_INNER), c_w_out.astype(BF16), xr)


def mixer_d(xr, B, S, nw, d_w_in, d_q_norm_w, d_k_norm_w, d_w_out):
    T, D = xr.shape
    scale = D_HEAD_DIM ** -0.5
    colw = jnp.concatenate([
        jnp.concatenate([jnp.tile(d_q_norm_w[g].astype(F32) * scale, D_HEADS),
                         jnp.tile(d_k_norm_w[g].astype(F32), D_HEADS),
                         jnp.ones((D,), F32)])
        for g in range(len(D_PATTERNS))]).reshape(1, -1)
    qkv = norm_matmul(xr, nw, d_w_in.astype(BF16), colw).reshape(B, S, -1)
    outs, lses = zip(*[dilated_group(qkv, g, dil, tiles=max(1, 4 // dil)) for g, (_, dil) in enumerate(D_PATTERNS)])
    o = combine_groups(outs, lses)
    return matmul_residual(o, d_w_out.astype(BF16), xr)


@jax.jit
def kernel(x, norm1_w, norm2_w, mlp_w1, mlp_w2, a_w_in, a_q_norm_w, a_k_norm_w, a_lambda_q1, a_lambda_k1, a_lambda_q2, a_lambda_k2, a_subln_w, a_w_out, b_w_in, b_gn_w, b_w_out, c_w_in, c_conv_w, c_conv_b, c_dt_bias, c_a_log, c_d_skip, c_norm_w, c_w_out, d_w_in, d_q_norm_w, d_k_norm_w, d_w_out):
    B, S, D = x.shape
    xr = x.reshape(B * S, D)
    ffn = lambda t, i: mlp(t, norm2_w[i], mlp_w1[i].astype(BF16), mlp_w2[i].astype(BF16))
    xr = mixer_a(xr, B, S, norm1_w[0], a_w_in, a_q_norm_w, a_k_norm_w, a_lambda_q1, a_lambda_k1,
                 a_lambda_q2, a_lambda_k2, a_subln_w, a_w_out, layer_idx=0)
    xr = ffn(xr, 0)
    xr = mixer_b(xr, B, S, norm1_w[1], b_w_in, b_gn_w, b_w_out)
    xr = ffn(xr, 1)
    xr = mixer_c(xr, B, S, norm1_w[2], c_w_in, c_conv_w, c_conv_b, c_dt_bias, c_a_log, c_d_skip, c_norm_w,
                 c_w_out)
    xr = ffn(xr, 2)
    xr = mixer_d(xr, B, S, norm1_w[3], d_w_in, d_q_norm_w, d_k_norm_w, d_w_out)
    xr = ffn(xr, 3)
    return xr.reshape(B, S, D)
```

```python
import functools
import math

import jax
import jax.numpy as jnp
from jax import lax
from jax.experimental import pallas as pl
from jax.experimental.pallas import tpu as pltpu

F32 = jnp.float32
BF16 = jnp.bfloat16

EPS = 1e-6
D_MODEL = 2048
D_FF = 4 * D_MODEL
LANES = 128

A_HEAD_DIM = 128
A_HEADS = 8
R_HEADS = 8
R_KDIM = 256
R_VDIM = 512
M_D_INNER = 4096
M_HEADDIM = 64
M_HEADS = 64
M_GROUPS = 8
M_DSTATE = 128
M_CONV = 4
M_GROUP_W = M_D_INNER // M_GROUPS
M_HPG = M_HEADS // M_GROUPS
D_HEADS = 16
D_HEAD_DIM = 128
D_PATTERNS = ((128, 1), (512, 4), (2048, 16))
D_SPAN = 128

VMEM_LIMIT_BYTES = 56 * 1024 * 1024
NEG_BIG = -1e30

NT_DIMS = (((1,), (1,)), ((), ()))
TN_DIMS = (((0,), (0,)), ((), ()))


def _params(*sem):
    return pltpu.CompilerParams(dimension_semantics=sem, vmem_limit_bytes=VMEM_LIMIT_BYTES)


def _silu(v):
    return v * (1.0 / (1.0 + jnp.exp(-v)))


def _rms_rows(v, w):
    ms = jnp.mean(v * v, axis=-1, keepdims=True)
    return v * lax.rsqrt(ms + EPS) * w


def _split3(v):
    p1 = v.astype(BF16)
    r1 = v - p1.astype(F32)
    p2 = r1.astype(BF16)
    return p1, p2, (r1 - p2.astype(F32)).astype(BF16)


def _norm_matmul_kernel(x_ref, nw_ref, w_ref, cw_ref, o_ref, h_sc, *, tn, head_norm, period, count):
    j = pl.program_id(1)

    @pl.when(j == 0)
    def _():
        h_sc[...] = _rms_rows(x_ref[...], nw_ref[...]).astype(BF16)

    acc = jnp.dot(h_sc[...], w_ref[...], preferred_element_type=F32)
    if not head_norm:
        o_ref[...] = acc.astype(o_ref.dtype)
        return
    is_norm = (j % period) < count

    @pl.when(is_norm)
    def _():
        for c in range(tn // LANES):
            sl = slice(c * LANES, (c + 1) * LANES)
            o_ref[:, sl] = _rms_rows(acc[:, sl], cw_ref[:, sl]).astype(o_ref.dtype)

    @pl.when(jnp.logical_not(is_norm))
    def _():
        o_ref[...] = acc.astype(o_ref.dtype)


def norm_matmul(x, nw, w, colw=None, *, tm=1024, tn=1024, out_dtype=BF16, norm_cols_per_2048=0):
    T, K = x.shape
    N = w.shape[1]
    tm = min(tm, T)
    tn = min(tn, N)
    head_norm = colw is not None
    if colw is None:
        colw = jnp.ones((1, N), F32)
    period = 3 * D_MODEL // tn
    count = 2 * D_MODEL // tn
    kern = functools.partial(_norm_matmul_kernel, tn=tn, head_norm=head_norm, period=period, count=count)
    return pl.pallas_call(
        kern,
        grid=(T // tm, N // tn),
        in_specs=[
            pl.BlockSpec((tm, K), lambda i, j: (i, 0)),
            pl.BlockSpec((1, K), lambda i, j: (0, 0)),
            pl.BlockSpec((K, tn), lambda i, j: (0, j)),
            pl.BlockSpec((1, tn), lambda i, j: (0, j)),
        ],
        out_specs=pl.BlockSpec((tm, tn), lambda i, j: (i, j)),
        out_shape=jax.ShapeDtypeStruct((T, N), out_dtype),
        scratch_shapes=[pltpu.VMEM((tm, K), BF16)],
        compiler_params=_params("parallel", "arbitrary"),
        name="norm_matmul",
    )(x, nw.reshape(1, K), w, colw)


def _norm_matmul_t_kernel(x_ref, nw_ref, wt_ref, o_ref, h_sc):
    @pl.when(pl.program_id(1) == 0)
    def _():
        h_sc[...] = _rms_rows(x_ref[...], nw_ref[...]).astype(BF16)

    o_ref[...] = lax.dot_general(wt_ref[...], h_sc[...], NT_DIMS, preferred_element_type=F32).astype(o_ref.dtype)


def norm_matmul_t(x, nw, w_t, *, tm=1024, tn=1024):
    T, K = x.shape
    N = w_t.shape[0]
    tm = min(tm, T)
    return pl.pallas_call(
        _norm_matmul_t_kernel,
        grid=(T // tm, N // tn),
        in_specs=[
            pl.BlockSpec((tm, K), lambda i, j: (i, 0)),
            pl.BlockSpec((1, K), lambda i, j: (0, 0)),
            pl.BlockSpec((tn, K), lambda i, j: (j, 0)),
        ],
        out_specs=pl.BlockSpec((tn, tm), lambda i, j: (j, i)),
        out_shape=jax.ShapeDtypeStruct((N, T), BF16),
        scratch_shapes=[pltpu.VMEM((tm, K), BF16)],
        compiler_params=_params("parallel", "arbitrary"),
        name="norm_matmul_t",
    )(x, nw.reshape(1, K), w_t)


def _matmul_residual_kernel(a_ref, w_ref, r_ref, o_ref):
    o_ref[...] = r_ref[...] + jnp.dot(a_ref[...], w_ref[...], preferred_element_type=F32)


def matmul_residual(a, w, res, *, tm=1024, tn=1024):
    T, K = a.shape
    N = w.shape[1]
    tm = min(tm, T)
    return pl.pallas_call(
        _matmul_residual_kernel,
        grid=(T // tm, N // tn),
        in_specs=[
            pl.BlockSpec((tm, K), lambda i, j: (i, 0)),
            pl.BlockSpec((K, tn), lambda i, j: (0, j)),
            pl.BlockSpec((tm, tn), lambda i, j: (i, j)),
        ],
        out_specs=pl.BlockSpec((tm, tn), lambda i, j: (i, j)),
        out_shape=jax.ShapeDtypeStruct((T, N), F32),
        compiler_params=_params("parallel", "arbitrary"),
        name="matmul_residual",
    )(a, w, res)


def _mlp_kernel(x_ref, nw_ref, w1_ref, w2_ref, o_ref, h_sc):
    f = pl.program_id(1)

    @pl.when(f == 0)
    def _():
        x = x_ref[...]
        h_sc[...] = _rms_rows(x, nw_ref[...]).astype(BF16)
        o_ref[...] = x

    u = jnp.dot(h_sc[...], w1_ref[...], preferred_element_type=F32)
    u = jnp.square(jnp.maximum(u, 0.0)).astype(BF16)
    o_ref[...] += jnp.dot(u, w2_ref[...], preferred_element_type=F32)


def mlp(x, nw, w1, w2, *, tm=512, tf=1024):
    T, D = x.shape
    FF = w1.shape[1]
    tm = min(tm, T)
    return pl.pallas_call(
        _mlp_kernel,
        grid=(T // tm, FF // tf),
        in_specs=[
            pl.BlockSpec((tm, D), lambda i, f: (i, 0)),
            pl.BlockSpec((1, D), lambda i, f: (0, 0)),
            pl.BlockSpec((D, tf), lambda i, f: (0, f)),
            pl.BlockSpec((tf, D), lambda i, f: (f, 0)),
        ],
        out_specs=pl.BlockSpec((tm, D), lambda i, f: (i, 0)),
        out_shape=jax.ShapeDtypeStruct((T, D), F32),
        scratch_shapes=[pltpu.VMEM((tm, D), BF16)],
        compiler_params=_params("parallel", "arbitrary"),
        name="mlp",
    )(x, nw.reshape(1, D), w1, w2)


def _diff_attn_kernel(q_ref, k_ref, vt_ref, lq1_ref, lk1_ref, lq2_ref, lk2_ref, sw_ref, o_ref,
                      m_sc, l_sc, acc_sc, *, tq, lambda_init):
    qi = pl.program_id(2)
    m_sc[...] = jnp.full_like(m_sc, NEG_BIG)
    l_sc[...] = jnp.zeros_like(l_sc)
    acc_sc[...] = jnp.zeros_like(acc_sc)
    key = lax.broadcasted_iota(jnp.int32, (tq, tq), 0)
    qry = lax.broadcasted_iota(jnp.int32, (tq, tq), 1)

    def scores(m, start):
        hs = slice(m * A_HEAD_DIM, (m + 1) * A_HEAD_DIM)
        return lax.dot_general(k_ref[0, pl.ds(start, tq), hs], q_ref[0, :, hs], NT_DIMS,
                               preferred_element_type=F32)

    def update(m, s, start, masked):
        if masked:
            s = jnp.where(key <= qry, s, NEG_BIG)
        m_prev = m_sc[m]
        m_new = jnp.maximum(m_prev, jnp.max(s, axis=0, keepdims=True))
        alpha = jnp.exp2(m_prev - m_new)
        p = jnp.exp2(s - m_new)
        m_sc[m] = m_new
        l_sc[m] = alpha * l_sc[m] + jnp.sum(p, axis=0, keepdims=True)
        acc_sc[m] = alpha * acc_sc[m] + jnp.dot(vt_ref[:, pl.ds(start, tq)], p.astype(BF16),
                                                preferred_element_type=F32)

    def attend(j, masked):
        start = pl.multiple_of(j * tq, tq)
        s0 = scores(0, start)
        s1 = scores(1, start)
        update(0, s0, start, masked)
        update(1, s1, start, masked)

    def body(j, carry):
        attend(j, False)
        return carry

    lax.fori_loop(0, qi, body, 0)
    attend(qi, True)
    lam = (jnp.exp(jnp.sum(lq1_ref[...] * lk1_ref[...], axis=-1, keepdims=True))
           - jnp.exp(jnp.sum(lq2_ref[...] * lk2_ref[...], axis=-1, keepdims=True)) + lambda_init)
    o_t = acc_sc[0] * (1.0 / l_sc[0]) - lam * (acc_sc[1] * (1.0 / l_sc[1]))
    o_ref[0] = (_rms_rows(o_t.T, sw_ref[...]) * (1.0 - lambda_init)).astype(o_ref.dtype)


def diff_attention(qk, v_t, lq1, lk1, lq2, lk2, subln_w, *, layer_idx, tq=512):
    B, S, _ = qk.shape
    tq = min(tq, S)
    lambda_init = 0.8 - 0.6 * math.exp(-0.3 * layer_idx)
    pw = 2 * A_HEAD_DIM
    vec = lambda a: a.reshape(1, -1).astype(F32)
    small = lambda n: pl.BlockSpec((1, n), lambda b, h, i: (0, 0))
    kern = functools.partial(_diff_attn_kernel, tq=tq, lambda_init=lambda_init)
    return pl.pallas_call(
        kern,
        grid=(B, A_HEADS, S // tq),
        in_specs=[
            pl.BlockSpec((1, tq, pw), lambda b, h, i: (b, i, h)),
            pl.BlockSpec((1, S, pw), lambda b, h, i: (b, 0, A_HEADS + h)),
            pl.BlockSpec((pw, S), lambda b, h, i: (h, b)),
            small(A_HEAD_DIM), small(A_HEAD_DIM), small(A_HEAD_DIM), small(A_HEAD_DIM), small(pw),
        ],
        out_specs=pl.BlockSpec((1, tq, pw), lambda b, h, i: (b, i, h)),
        out_shape=jax.ShapeDtypeStruct((B, S, D_MODEL), BF16),
        scratch_shapes=[
            pltpu.VMEM((2, 1, tq), F32),
            pltpu.VMEM((2, 1, tq), F32),
            pltpu.VMEM((2, pw, tq), F32),
        ],
        compiler_params=_params("parallel", "parallel", "arbitrary"),
        name="diff_attention",
    )(qk, qk, v_t, vec(lq1), vec(lk1), vec(lq2), vec(lk2), vec(subln_w))


def _retention_kernel(q_ref, k_ref, v_ref, g_ref, cos_ref, sin_ref, dm_ref, xi_ref, zeta_ref, cd_ref,
                      gw_ref, o_ref, r_sc, *, tr, chunk):
    @pl.when(pl.program_id(2) == 0)
    def _():
        r_sc[...] = jnp.zeros_like(r_sc)

    half = R_KDIM // 2
    dmask = dm_ref[0]
    xi = xi_ref[0]
    zeta = zeta_ref[0]
    cdecay = cd_ref[0, 0:1, 0:1]
    gw = gw_ref[0]

    def rotate(t, c, s):
        te, to = t[:, :half], t[:, half:]
        return te * c - to * s, to * c + te * s

    for ci in range(tr // chunk):
        rows = slice(ci * chunk, (ci + 1) * chunk)
        c = cos_ref[rows, :]
        s = sin_ref[rows, :]
        qe, qo = rotate(q_ref[0, rows, :].astype(F32), c, s)
        ke, ko = rotate(k_ref[0, rows, :].astype(F32) * (R_KDIM ** -0.5), c, s)
        v = v_ref[0, rows, :]
        q_r = jnp.concatenate([qe, qo], axis=1).astype(BF16)
        k_r = jnp.concatenate([ke, ko], axis=1).astype(BF16)
        q_x = jnp.concatenate([qe * xi, qo * xi], axis=1).astype(BF16)
        k_z = jnp.concatenate([ke * zeta, ko * zeta], axis=1).astype(BF16)
        r_old = r_sc[...]
        sc = lax.dot_general(q_r, k_r, NT_DIMS, preferred_element_type=F32) * dmask
        o = (jnp.dot(sc.astype(BF16), v, preferred_element_type=F32)
             + jnp.dot(q_x, r_old.astype(BF16), preferred_element_type=F32))
        r_sc[...] = r_old * cdecay + lax.dot_general(k_z, v, TN_DIMS, preferred_element_type=F32)
        gate = _silu(g_ref[0, rows, :].astype(F32))
        o_ref[0, rows, :] = (gate * _rms_rows(o, gw)).astype(o_ref.dtype)


def retention(proj, gn_w, *, tr=1024, chunk=256):
    B, S, _ = proj.shape
    tr = min(tr, S)
    chunk = min(chunk, tr)
    half = R_KDIM // 2
    angle = 1.0 / (10000.0 ** jnp.linspace(0.0, 1.0, half, dtype=F32))
    ang = jnp.arange(S, dtype=F32)[:, None] * angle[None, :]
    cos, sin = jnp.cos(ang), jnp.sin(ang)
    log_g = jnp.log(1.0 - 2.0 ** (-5.0 - jnp.arange(R_HEADS, dtype=F32)))
    idx = jnp.arange(chunk, dtype=F32)
    rel = idx[:, None] - idx[None, :]
    dmask = jnp.where(rel[None] >= 0, jnp.exp(jnp.maximum(rel, 0.0)[None] * log_g[:, None, None]), 0.0)
    xi = jnp.exp((idx + 1.0)[None, :] * log_g[:, None])
    zeta = jnp.exp((chunk - 1.0 - idx)[None, :] * log_g[:, None])
    cdecay = jnp.exp(chunk * log_g)
    bc = lambda t: jnp.broadcast_to(t[:, :, None], (R_HEADS, chunk, half))
    cd = jnp.broadcast_to(cdecay[:, None, None], (R_HEADS, 8, LANES))
    nq = D_MODEL // R_KDIM
    nv = 2 * D_MODEL // R_VDIM
    kern = functools.partial(_retention_kernel, tr=tr, chunk=chunk)
    head = lambda shape: pl.BlockSpec(shape, lambda b, h, i: (h, 0, 0))
    return pl.pallas_call(
        kern,
        grid=(B, R_HEADS, S // tr),
        in_specs=[
            pl.BlockSpec((1, tr, R_KDIM), lambda b, h, i: (b, i, h)),
            pl.BlockSpec((1, tr, R_KDIM), lambda b, h, i: (b, i, nq + h)),
            pl.BlockSpec((1, tr, R_VDIM), lambda b, h, i: (b, i, nv + h)),
            pl.BlockSpec((1, tr, R_VDIM), lambda b, h, i: (b, i, nv + R_HEADS + h)),
            pl.BlockSpec((tr, half), lambda b, h, i: (i, 0)),
            pl.BlockSpec((tr, half), lambda b, h, i: (i, 0)),
            head((1, chunk, chunk)), head((1, chunk, half)), head((1, chunk, half)),
            head((1, 8, LANES)), head((1, 1, R_VDIM)),
        ],
        out_specs=pl.BlockSpec((1, tr, R_VDIM), lambda b, h, i: (b, i, h)),
        out_shape=jax.ShapeDtypeStruct((B, S, R_HEADS * R_VDIM), BF16),
        scratch_shapes=[pltpu.VMEM((R_KDIM, R_VDIM), F32)],
        compiler_params=_params("parallel", "parallel", "arbitrary"),
        name="retention",
    )(proj, proj, proj, proj, cos, sin, dmask, bc(xi), bc(zeta), cd, gn_w.reshape(R_HEADS, 1, R_VDIM).astype(F32))


def _mamba_kernel(z_ref, x_ref, b_ref, c_ref, dt_ref, wx_ref, wb_ref, wc_ref, bx_ref, bb_ref, bc_ref,
                  dtb_ref, alog_ref, dsk_ref, nw_ref, o_ref,
                  st_sc, tx_sc, tb_sc, tc_sc, *, chunk):
    L = chunk

    @pl.when(pl.program_id(2) == 0)
    def _():
        st_sc[...] = jnp.zeros_like(st_sc)
        tx_sc[...] = jnp.zeros_like(tx_sc)
        tb_sc[...] = jnp.zeros_like(tb_sc)
        tc_sc[...] = jnp.zeros_like(tc_sc)

    def conv_silu(cur_ref, tail_sc, w_ref, bias_ref):
        cur = cur_ref[0].astype(F32)
        ext = jnp.concatenate([tail_sc[...], cur], axis=0)
        tail_sc[...] = cur[L - 8:, :]
        w = w_ref[...]
        acc = bias_ref[...] + cur * w[M_CONV - 1:M_CONV, :]
        for kk in range(M_CONV - 1):
            sh = M_CONV - 1 - kk
            acc = acc + ext[8 - sh:8 - sh + L, :] * w[kk:kk + 1, :]
        return _silu(acc)

    xs = conv_silu(x_ref, tx_sc, wx_ref, bx_ref)
    bm = conv_silu(b_ref, tb_sc, wb_ref, bb_ref)
    cm = conv_silu(c_ref, tc_sc, wc_ref, bc_ref)

    raw = dt_ref[0, 0] + dtb_ref[0]
    dt_t = jnp.maximum(raw, 0.0) + jnp.log(1.0 + jnp.exp(-jnp.abs(raw)))
    a_t = dt_t * (-jnp.exp(alog_ref[0])) * math.log2(math.e)

    ri = lax.broadcasted_iota(jnp.int32, (L, L), 0)
    ci = lax.broadcasted_iota(jnp.int32, (L, L), 1)
    tril = ci <= ri
    eye = jnp.where(ri == ci, 1.0, 0.0).astype(BF16)
    lower = jnp.where(tril, 1.0, 0.0).astype(BF16)
    upper = jnp.where(ri <= ci, 1.0, 0.0).astype(BF16)

    def per_column(t):
        return jnp.concatenate([jnp.broadcast_to(t[r:r + 1, :], (M_HEADDIM, L)) for r in range(M_HPG)], axis=0)

    a_parts = _split3(a_t)
    acs_row = sum(jnp.dot(p, upper, preferred_element_type=F32) for p in a_parts)

    def expand_dot(mat01, parts):
        return sum(lax.dot_general(mat01, per_column(p.astype(F32)).astype(BF16), NT_DIMS,
                                   preferred_element_type=F32) for p in parts)

    acs_x = expand_dot(lower, a_parts)
    dt_x = expand_dot(eye, _split3(dt_t))

    xdt = xs * dt_x
    cb = lax.dot_general(cm.astype(BF16), bm.astype(BF16), NT_DIMS, preferred_element_type=F32)
    xdt_b = xdt.astype(BF16)
    parts = []
    for r in range(M_HPG):
        diff = acs_x[:, r * M_HEADDIM:r * M_HEADDIM + 1] - acs_row[r:r + 1, :]
        lmat = jnp.exp2(jnp.where(tril, diff, NEG_BIG))
        mr = (cb * lmat).astype(BF16)
        parts.append(jnp.dot(mr, xdt_b[:, r * M_HEADDIM:(r + 1) * M_HEADDIM], preferred_element_type=F32))
    y = jnp.concatenate(parts, axis=1)

    state = st_sc[...]
    y = y + jnp.exp2(acs_x) * jnp.dot(cm.astype(BF16), state.astype(BF16), preferred_element_type=F32)
    last = acs_x[L - 1:L, :]
    decay_end = jnp.exp2(last - acs_x)
    st_sc[...] = state * jnp.exp2(last) + lax.dot_general(
        bm.astype(BF16), (xdt * decay_end).astype(BF16), TN_DIMS, preferred_element_type=F32)

    y = y + xs * dsk_ref[...]
    y = y * _silu(z_ref[0].astype(F32))
    o_ref[0] = _rms_rows(y, nw_ref[...]).astype(o_ref.dtype)


def mamba_ssd(zx, dt_raw, conv_w, conv_b, dt_bias, a_log, d_skip, norm_w, *, chunk=256):
    B, S, _ = zx.shape
    chunk = min(chunk, S)
    G, W, N = M_GROUPS, M_GROUP_W, M_DSTATE
    dt_t = dt_raw[:, :, :M_HEADS].reshape(B, S, G, M_HPG).transpose(0, 2, 3, 1)
    xoff = M_D_INNER // W
    boff = 2 * M_D_INNER // N
    coff = boff + G
    cwb = M_D_INNER // N
    conv_w = conv_w.astype(F32)
    conv_b = conv_b.reshape(1, -1).astype(F32)
    per_head = lambda t: t.reshape(G, M_HPG, 1).astype(F32)
    dsk_x = jnp.repeat(d_skip.astype(F32), M_HEADDIM).reshape(1, M_D_INNER)
    kern = functools.partial(_mamba_kernel, chunk=chunk)
    return pl.pallas_call(
        kern,
        grid=(B, G, S // chunk),
        in_specs=[
            pl.BlockSpec((1, chunk, W), lambda b, g, c: (b, c, g)),
            pl.BlockSpec((1, chunk, W), lambda b, g, c: (b, c, xoff + g)),
            pl.BlockSpec((1, chunk, N), lambda b, g, c: (b, c, boff + g)),
            pl.BlockSpec((1, chunk, N), lambda b, g, c: (b, c, coff + g)),
            pl.BlockSpec((1, 1, M_HPG, chunk), lambda b, g, c: (b, g, 0, c)),
            pl.BlockSpec((M_CONV, W), lambda b, g, c: (0, g)),
            pl.BlockSpec((M_CONV, N), lambda b, g, c: (0, cwb + g)),
            pl.BlockSpec((M_CONV, N), lambda b, g, c: (0, cwb + G + g)),
            pl.BlockSpec((1, W), lambda b, g, c: (0, g)),
            pl.BlockSpec((1, N), lambda b, g, c: (0, cwb + g)),
            pl.BlockSpec((1, N), lambda b, g, c: (0, cwb + G + g)),
            pl.BlockSpec((1, M_HPG, 1), lambda b, g, c: (g, 0, 0)),
            pl.BlockSpec((1, M_HPG, 1), lambda b, g, c: (g, 0, 0)),
            pl.BlockSpec((1, W), lambda b, g, c: (0, g)),
            pl.BlockSpec((1, W), lambda b, g, c: (0, g)),
        ],
        out_specs=pl.BlockSpec((1, chunk, W), lambda b, g, c: (b, c, g)),
        out_shape=jax.ShapeDtypeStruct((B, S, M_D_INNER), BF16),
        scratch_shapes=[
            pltpu.VMEM((N, W), F32),
            pltpu.VMEM((8, W), F32),
            pltpu.VMEM((8, N), F32),
            pltpu.VMEM((8, N), F32),
        ],
        compiler_params=_params("parallel", "parallel", "arbitrary"),
        name="mamba_ssd",
    )(zx, zx, zx, zx, dt_t, conv_w, conv_w, conv_w, conv_b, conv_b, conv_b,
      per_head(dt_bias), per_head(a_log), dsk_x, norm_w.reshape(1, M_D_INNER).astype(F32))


def _dilated_kernel(q_ref, kp_ref, kc_ref, vp_ref, vc_ref, o_ref, lse_ref, lse_sc, *staging, dil, tiles, unroll):
    n = pl.program_id(1)
    h = pl.program_id(2)
    T = D_SPAN
    span = T * dil
    staged = dil > 1
    if staged:
        q_sc, k_sc, v_sc, o_sc = staging
        q_sc[...] = q_ref[0].astype(F32)
        k_sc[0:span, :] = kp_ref[0].astype(F32)
        k_sc[span:, :] = kc_ref[0].astype(F32)
        v_sc[0:span, :] = vp_ref[0].astype(F32)
        v_sc[span:, :] = vc_ref[0].astype(F32)

    @pl.when(h == 0)
    def _():
        lse_sc[...] = jnp.zeros_like(lse_sc)

    ri = lax.broadcasted_iota(jnp.int32, (T, T), 0)
    ci = lax.broadcasted_iota(jnp.int32, (T, T), 1)
    lane = lax.broadcasted_iota(jnp.int32, (T, LANES), 1)

    def attend(operands):
        scores = [(lax.dot_general(q, kp, NT_DIMS, preferred_element_type=F32),
                   lax.dot_general(q, kc, NT_DIMS, preferred_element_type=F32))
                  for q, kp, kc, _, _, _ in operands]
        probs = []
        for (sp, sc), (_, _, _, _, _, has_prev) in zip(scores, operands):
            sp = jnp.where((ci >= ri) & has_prev, sp, NEG_BIG)
            sc = jnp.where(ci <= ri, sc, NEG_BIG)
            mx = jnp.maximum(jnp.max(sp, axis=-1, keepdims=True), jnp.max(sc, axis=-1, keepdims=True))
            pp = jnp.exp(sp - mx)
            pc = jnp.exp(sc - mx)
            l = jnp.sum(pp, axis=-1, keepdims=True) + jnp.sum(pc, axis=-1, keepdims=True)
            probs.append((pp.astype(BF16), pc.astype(BF16), l, mx))
        outs = []
        for (pp, pc, l, mx), (_, _, _, vp, vc, _) in zip(probs, operands):
            o = jnp.dot(pp, vp, preferred_element_type=F32) + jnp.dot(pc, vc, preferred_element_type=F32)
            outs.append((o * (1.0 / l), mx + jnp.log(l)))
        return outs

    if staged:
        def group(jj, carry):
            rs = [jj * unroll + u for u in range(unroll)]
            bf = lambda ref, r, off: ref[pl.ds(r + off, T, stride=dil), :].astype(BF16)
            outs = attend([(bf(q_sc, r, 0), bf(k_sc, r, 0), bf(k_sc, r, span), bf(v_sc, r, 0), bf(v_sc, r, span),
                            n > 0) for r in rs])
            for r, (o, lse) in zip(rs, outs):
                o_sc[pl.ds(r, T, stride=dil), :] = o
                lse_sc[pl.ds(r, T, stride=dil), :] = jnp.where(
                    lane == h, lse, lse_sc[pl.ds(r, T, stride=dil), :])
            return carry

        if dil == unroll:
            group(0, 0)
        else:
            lax.fori_loop(0, dil // unroll, group, 0)
        o_ref[0] = o_sc[...].astype(o_ref.dtype)
    else:
        operands = []
        for i in range(tiles):
            cur = slice(i * T, (i + 1) * T)
            prev = slice((i - 1) * T, i * T)
            kp = kp_ref[0] if i == 0 else kc_ref[0, prev, :]
            vp = vp_ref[0] if i == 0 else vc_ref[0, prev, :]
            operands.append((q_ref[0, cur, :], kp, kc_ref[0, cur, :], vp, vc_ref[0, cur, :],
                             jnp.logical_or(n > 0, i > 0)))
        for i, (o, lse) in enumerate(attend(operands)):
            cur = slice(i * T, (i + 1) * T)
            o_ref[0, cur, :] = o.astype(o_ref.dtype)
            lse_sc[cur, :] = jnp.where(lane == h, lse, lse_sc[cur, :])

    @pl.when(h == D_HEADS - 1)
    def _():
        lse_ref[0] = lse_sc[...]


def dilated_group(qkv, g, dil, *, tiles, unroll=4):
    B, S, C = qkv.shape
    assert dil == 1 or (tiles == 1 and dil % unroll == 0)
    span = D_SPAN * dil
    tb = span * tiles
    hd = D_HEAD_DIM
    col = lambda t: (g * 3 + t) * D_HEADS
    cur = lambda t: pl.BlockSpec((1, tb, hd), lambda b, n, h: (b, n, col(t) + h))
    prev = lambda t: pl.BlockSpec((1, span, hd), lambda b, n, h: (b, jnp.maximum(n * tiles - 1, 0), col(t) + h))
    kern = functools.partial(_dilated_kernel, dil=dil, tiles=tiles, unroll=unroll)
    staging = [] if dil == 1 else [
        pltpu.VMEM((tb, hd), F32),
        pltpu.VMEM((span + tb, hd), F32),
        pltpu.VMEM((span + tb, hd), F32),
        pltpu.VMEM((tb, hd), F32),
    ]
    o, lse = pl.pallas_call(
        kern,
        grid=(B, S // tb, D_HEADS),
        in_specs=[cur(0), prev(1), cur(1), prev(2), cur(2)],
        out_specs=[
            pl.BlockSpec((1, tb, hd), lambda b, n, h: (b, n, h)),
            pl.BlockSpec((1, tb, LANES), lambda b, n, h: (b, n, 0)),
        ],
        out_shape=[
            jax.ShapeDtypeStruct((B, S, D_HEADS * hd), BF16),
            jax.ShapeDtypeStruct((B, S, LANES), F32),
        ],
        scratch_shapes=[pltpu.VMEM((tb, LANES), F32)] + staging,
        compiler_params=_params("parallel", "parallel", "arbitrary"),
        name="dilated_attention",
    )(qkv, qkv, qkv, qkv, qkv)
    return o.reshape(B * S, D_HEADS * hd), lse.reshape(B * S, LANES)


def _combine_kernel(o0_ref, o1_ref, o2_ref, l0_ref, l1_ref, l2_ref, o_ref):
    l0, l1, l2 = l0_ref[...], l1_ref[...], l2_ref[...]
    mx = jnp.maximum(jnp.maximum(l0, l1), l2)
    e0, e1, e2 = jnp.exp(l0 - mx), jnp.exp(l1 - mx), jnp.exp(l2 - mx)
    inv = 1.0 / (e0 + e1 + e2)
    ws = (e0 * inv, e1 * inv, e2 * inv)
    for h in range(D_HEADS):
        hs = slice(h * D_HEAD_DIM, (h + 1) * D_HEAD_DIM)
        acc = ws[0][:, h:h + 1] * o0_ref[:, hs].astype(F32)
        acc = acc + ws[1][:, h:h + 1] * o1_ref[:, hs].astype(F32)
        acc = acc + ws[2][:, h:h + 1] * o2_ref[:, hs].astype(F32)
        o_ref[:, hs] = acc.astype(o_ref.dtype)


def combine_groups(outs, lses, *, tm=512):
    T, W = outs[0].shape
    tm = min(tm, T)
    wide = pl.BlockSpec((tm, W), lambda i: (i, 0))
    narrow = pl.BlockSpec((tm, LANES), lambda i: (i, 0))
    return pl.pallas_call(
        _combine_kernel,
        grid=(T // tm,),
        in_specs=[wide, wide, wide, narrow, narrow, narrow],
        out_specs=wide,
        out_shape=jax.ShapeDtypeStruct((T, W), BF16),
        compiler_params=_params("parallel"),
        name="combine_groups",
    )(*outs, *lses)


def _deinterleave_heads(w, heads, dim):
    k = w.shape[0]
    return w.reshape(k, heads, dim // 2, 2).transpose(0, 1, 3, 2).reshape(k, heads * dim)


def mixer_a(xr, B, S, nw, a_w_in, a_q_norm_w, a_k_norm_w, lq1, lk1, lq2, lk2, a_subln_w, a_w_out, *, layer_idx):
    T, D = xr.shape
    colw = jnp.concatenate([
        jnp.tile(a_q_norm_w.astype(F32) * (A_HEAD_DIM ** -0.5 * math.log2(math.e)), 2 * A_HEADS),
        jnp.tile(a_k_norm_w.astype(F32), 2 * A_HEADS)]).reshape(1, 2 * D)
    qk = norm_matmul(xr, nw, a_w_in[:, :2 * D].astype(BF16), colw)
    v_t = norm_matmul_t(xr, nw, a_w_in[:, 2 * D:].T.astype(BF16))
    o = diff_attention(qk.reshape(B, S, 2 * D), v_t, lq1, lk1, lq2, lk2, a_subln_w, layer_idx=layer_idx)
    return matmul_residual(o.reshape(T, D), a_w_out.astype(BF16), xr)


def mixer_b(xr, B, S, nw, b_w_in, b_gn_w, b_w_out):
    T, D = xr.shape
    w_in = jnp.concatenate([
        _deinterleave_heads(b_w_in[:, :D], R_HEADS, R_KDIM),
        _deinterleave_heads(b_w_in[:, D:2 * D], R_HEADS, R_KDIM),
        b_w_in[:, 2 * D:]], axis=1)
    proj = norm_matmul(xr, nw, w_in.astype(BF16))
    o = retention(proj.reshape(B, S, -1), b_gn_w)
    return matmul_residual(o.reshape(T, -1), b_w_out.astype(BF16), xr)


def mixer_c(xr, B, S, nw, c_w_in, c_conv_w, c_conv_b, c_dt_bias, c_a_log, c_d_skip, c_norm_w, c_w_out):
    T, D = xr.shape
    n_main = 2 * M_D_INNER + 2 * M_GROUPS * M_DSTATE
    zx = norm_matmul(xr, nw, c_w_in[:, :n_main].astype(BF16))
    w_dt = jnp.pad(c_w_in[:, n_main:], ((0, 0), (0, LANES - M_HEADS)))
    dt_raw = norm_matmul(xr, nw, w_dt.astype(BF16), tn=LANES, out_dtype=F32)
    y = mamba_ssd(zx.reshape(B, S, n_main), dt_raw.reshape(B, S, LANES), c_conv_w, c_conv_b, c_dt_bias,
                  c_a_log, c_d_skip, c_norm_w)
    return matmul_residual(y.reshape(T, M_D_INNER), c_w_out.astype(BF16), xr)


def mixer_d(xr, B, S, nw, d_w_in, d_q_norm_w, d_k_norm_w, d_w_out):
    T, D = xr.shape
    scale = D_HEAD_DIM ** -0.5
    colw = jnp.concatenate([
        jnp.concatenate([jnp.tile(d_q_norm_w[g].astype(F32) * scale, D_HEADS),
                         jnp.tile(d_k_norm_w[g].astype(F32), D_HEADS),
                         jnp.ones((D,), F32)])
        for g in range(len(D_PATTERNS))]).reshape(1, -1)
    qkv = norm_matmul(xr, nw, d_w_in.astype(BF16), colw).reshape(B, S, -1)
    outs, lses = zip(*[dilated_group(qkv, g, dil, tiles=max(1, 4 // dil)) for g, (_, dil) in enumerate(D_PATTERNS)])
    o = combine_groups(outs, lses)
    return matmul_residual(o, d_w_out.astype(BF16), xr)


@jax.jit
def kernel(x, norm1_w, norm2_w, mlp_w1, mlp_w2, a_w_in, a_q_norm_w, a_k_norm_w, a_lambda_q1, a_lambda_k1, a_lambda_q2, a_lambda_k2, a_subln_w, a_w_out, b_w_in, b_gn_w, b_w_out, c_w_in, c_conv_w, c_conv_b, c_dt_bias, c_a_log, c_d_skip, c_norm_w, c_w_out, d_w_in, d_q_norm_w, d_k_norm_w, d_w_out):
    B, S, D = x.shape
    xr = x.reshape(B * S, D)
    ffn = lambda t, i: mlp(t, norm2_w[i], mlp_w1[i].astype(BF16), mlp_w2[i].astype(BF16))
    xr = mixer_a(xr, B, S, norm1_w[0], a_w_in, a_q_norm_w, a_k_norm_w, a_lambda_q1, a_lambda_k1,
                 a_lambda_q2, a_lambda_k2, a_subln_w, a_w_out, layer_idx=0)
    xr = ffn(xr, 0)
    xr = mixer_b(xr, B, S, norm1_w[1], b_w_in, b_gn_w, b_w_out)
    xr = ffn(xr, 1)
    xr = mixer_c(xr, B, S, norm1_w[2], c_w_in, c_conv_w, c_conv_b, c_dt_bias, c_a_log, c_d_skip, c_norm_w,
                 c_w_out)
    xr = ffn(xr, 2)
    xr = mixer_d(xr, B, S, norm1_w[3], d_w_in, d_q_norm_w, d_k_norm_w, d_w_out)
    xr = ffn(xr, 3)
    return xr.reshape(B, S, D)
```

```python
import functools
import math

import jax
import jax.numpy as jnp
from jax import lax
from jax.experimental import pallas as pl
from jax.experimental.pallas import tpu as pltpu

F32 = jnp.float32
BF16 = jnp.bfloat16

EPS = 1e-6
D_MODEL = 2048
D_FF = 4 * D_MODEL
LANES = 128
MXU_COLS = 256

A_HEAD_DIM = 128
A_HEADS = 8
R_HEADS = 8
R_KDIM = 256
R_VDIM = 512
M_D_INNER = 4096
M_HEADDIM = 64
M_HEADS = 64
M_GROUPS = 8
M_DSTATE = 128
M_CONV = 4
M_GROUP_W = M_D_INNER // M_GROUPS
M_HPG = M_HEADS // M_GROUPS
D_HEADS = 16
D_HEAD_DIM = 128
D_PATTERNS = ((128, 1), (512, 4), (2048, 16))
D_SPAN = 128

VMEM_LIMIT_BYTES = 56 * 1024 * 1024
NEG_BIG = -1e30

NT_DIMS = (((1,), (1,)), ((), ()))
TN_DIMS = (((0,), (0,)), ((), ()))


def _params(*sem):
    return pltpu.CompilerParams(dimension_semantics=sem, vmem_limit_bytes=VMEM_LIMIT_BYTES)


def _silu(v):
    return v * (1.0 / (1.0 + jnp.exp(-v)))


def _rms_rows(v, w):
    ms = jnp.mean(v * v, axis=-1, keepdims=True)
    return v * lax.rsqrt(ms + EPS) * w


def _split3(v):
    p1 = v.astype(BF16)
    r1 = v - p1.astype(F32)
    p2 = r1.astype(BF16)
    return p1, p2, (r1 - p2.astype(F32)).astype(BF16)


def _norm_matmul_kernel(x_ref, nw_ref, w_ref, cw_ref, o_ref, h_sc, *, tn, head_norm, period, count):
    j = pl.program_id(1)

    @pl.when(j == 0)
    def _():
        h_sc[...] = _rms_rows(x_ref[...], nw_ref[...]).astype(BF16)

    def plain():
        o_ref[...] = jnp.dot(h_sc[...], w_ref[...], preferred_element_type=F32).astype(o_ref.dtype)

    if not head_norm:
        plain()
        return
    is_norm = (j % period) < count

    @pl.when(is_norm)
    def _():
        sub = min(tn, MXU_COLS)
        for c in range(tn // sub):
            acc = jnp.dot(h_sc[...], w_ref[:, c * sub:(c + 1) * sub], preferred_element_type=F32)
            for d in range(sub // LANES):
                sl = slice(c * sub + d * LANES, c * sub + (d + 1) * LANES)
                o_ref[:, sl] = _rms_rows(acc[:, d * LANES:(d + 1) * LANES], cw_ref[:, sl]).astype(o_ref.dtype)

    pl.when(jnp.logical_not(is_norm))(plain)


def norm_matmul(x, nw, w, colw=None, *, tm=1024, tn=1024, out_dtype=BF16, norm_cols_per_2048=0):
    T, K = x.shape
    N = w.shape[1]
    tm = min(tm, T)
    tn = min(tn, N)
    head_norm = colw is not None
    if colw is None:
        colw = jnp.ones((1, N), F32)
    period = 3 * D_MODEL // tn
    count = 2 * D_MODEL // tn
    kern = functools.partial(_norm_matmul_kernel, tn=tn, head_norm=head_norm, period=period, count=count)
    return pl.pallas_call(
        kern,
        grid=(T // tm, N // tn),
        in_specs=[
            pl.BlockSpec((tm, K), lambda i, j: (i, 0)),
            pl.BlockSpec((1, K), lambda i, j: (0, 0)),
            pl.BlockSpec((K, tn), lambda i, j: (0, j)),
            pl.BlockSpec((1, tn), lambda i, j: (0, j)),
        ],
        out_specs=pl.BlockSpec((tm, tn), lambda i, j: (i, j)),
        out_shape=jax.ShapeDtypeStruct((T, N), out_dtype),
        scratch_shapes=[pltpu.VMEM((tm, K), BF16)],
        compiler_params=_params("parallel", "arbitrary"),
        name="norm_matmul",
    )(x, nw.reshape(1, K), w, colw)


def _norm_matmul_t_kernel(x_ref, nw_ref, wt_ref, o_ref, h_sc):
    @pl.when(pl.program_id(1) == 0)
    def _():
        h_sc[...] = _rms_rows(x_ref[...], nw_ref[...]).astype(BF16)

    o_ref[...] = lax.dot_general(wt_ref[...], h_sc[...], NT_DIMS, preferred_element_type=F32).astype(o_ref.dtype)


def norm_matmul_t(x, nw, w_t, *, tm=1024, tn=1024):
    T, K = x.shape
    N = w_t.shape[0]
    tm = min(tm, T)
    return pl.pallas_call(
        _norm_matmul_t_kernel,
        grid=(T // tm, N // tn),
        in_specs=[
            pl.BlockSpec((tm, K), lambda i, j: (i, 0)),
            pl.BlockSpec((1, K), lambda i, j: (0, 0)),
            pl.BlockSpec((tn, K), lambda i, j: (j, 0)),
        ],
        out_specs=pl.BlockSpec((tn, tm), lambda i, j: (j, i)),
        out_shape=jax.ShapeDtypeStruct((N, T), BF16),
        scratch_shapes=[pltpu.VMEM((tm, K), BF16)],
        compiler_params=_params("parallel", "arbitrary"),
        name="norm_matmul_t",
    )(x, nw.reshape(1, K), w_t)


def _matmul_residual_kernel(a_ref, w_ref, r_ref, o_ref):
    o_ref[...] = r_ref[...] + jnp.dot(a_ref[...], w_ref[...], preferred_element_type=F32)


def matmul_residual(a, w, res, *, tm=1024, tn=1024):
    T, K = a.shape
    N = w.shape[1]
    tm = min(tm, T)
    return pl.pallas_call(
        _matmul_residual_kernel,
        grid=(T // tm, N // tn),
        in_specs=[
            pl.BlockSpec((tm, K), lambda i, j: (i, 0)),
            pl.BlockSpec((K, tn), lambda i, j: (0, j)),
            pl.BlockSpec((tm, tn), lambda i, j: (i, j)),
        ],
        out_specs=pl.BlockSpec((tm, tn), lambda i, j: (i, j)),
        out_shape=jax.ShapeDtypeStruct((T, N), F32),
        compiler_params=_params("parallel", "arbitrary"),
        name="matmul_residual",
    )(a, w, res)


def _mlp_kernel(x_ref, nw_ref, w1_ref, w2_ref, o_ref, h_sc):
    f = pl.program_id(1)

    @pl.when(f == 0)
    def _():
        x = x_ref[...]
        h_sc[...] = _rms_rows(x, nw_ref[...]).astype(BF16)
        o_ref[...] = x

    u = jnp.dot(h_sc[...], w1_ref[...], preferred_element_type=F32)
    u = jnp.square(jnp.maximum(u, 0.0)).astype(BF16)
    o_ref[...] += jnp.dot(u, w2_ref[...], preferred_element_type=F32)


def mlp(x, nw, w1, w2, *, tm=512, tf=1024):
    T, D = x.shape
    FF = w1.shape[1]
    tm = min(tm, T)
    return pl.pallas_call(
        _mlp_kernel,
        grid=(T // tm, FF // tf),
        in_specs=[
            pl.BlockSpec((tm, D), lambda i, f: (i, 0)),
            pl.BlockSpec((1, D), lambda i, f: (0, 0)),
            pl.BlockSpec((D, tf), lambda i, f: (0, f)),
            pl.BlockSpec((tf, D), lambda i, f: (f, 0)),
        ],
        out_specs=pl.BlockSpec((tm, D), lambda i, f: (i, 0)),
        out_shape=jax.ShapeDtypeStruct((T, D), F32),
        scratch_shapes=[pltpu.VMEM((tm, D), BF16)],
        compiler_params=_params("parallel", "arbitrary"),
        name="mlp",
    )(x, nw.reshape(1, D), w1, w2)


def _diff_attn_kernel(q_ref, k_ref, vt_ref, lq1_ref, lk1_ref, lq2_ref, lk2_ref, sw_ref, o_ref,
                      m_sc, l_sc, acc_sc, sa_sc, sb_sc, *, tq, lambda_init):
    qi = pl.program_id(2)
    m_sc[...] = jnp.full_like(m_sc, NEG_BIG)
    l_sc[...] = jnp.zeros_like(l_sc)
    acc_sc[...] = jnp.zeros_like(acc_sc)
    key = lax.broadcasted_iota(jnp.int32, (tq, tq), 0)
    qry = lax.broadcasted_iota(jnp.int32, (tq, tq), 1)

    def scores(m, start):
        hs = slice(m * A_HEAD_DIM, (m + 1) * A_HEAD_DIM)
        return lax.dot_general(k_ref[0, pl.ds(start, tq), hs], q_ref[0, :, hs], NT_DIMS,
                               preferred_element_type=F32)

    def update(m, s, start, masked):
        if masked:
            s = jnp.where(key <= qry, s, NEG_BIG)
        m_prev = m_sc[m]
        m_new = jnp.maximum(m_prev, jnp.max(s, axis=0, keepdims=True))
        alpha = jnp.exp2(m_prev - m_new)
        p = jnp.exp2(s - m_new)
        m_sc[m] = m_new
        l_sc[m] = alpha * l_sc[m] + jnp.sum(p, axis=0, keepdims=True)
        acc_sc[m] = alpha * acc_sc[m] + jnp.dot(vt_ref[:, pl.ds(start, tq)], p.astype(BF16),
                                                preferred_element_type=F32)

    def put_scores(s_sc, j):
        start = pl.multiple_of(j * tq, tq)
        s_sc[0] = scores(0, start)
        s_sc[1] = scores(1, start)

    def updates(s_sc, j, masked):
        start = pl.multiple_of(j * tq, tq)
        update(0, s_sc[0], start, masked)
        update(1, s_sc[1], start, masked)

    put_scores(sa_sc, 0)

    def body(jj, carry):
        j = 2 * jj
        put_scores(sb_sc, j + 1)
        updates(sa_sc, j, False)
        put_scores(sa_sc, j + 2)
        updates(sb_sc, j + 1, False)
        return carry

    lax.fori_loop(0, qi // 2, body, 0)

    @pl.when(qi % 2 == 0)
    def _():
        updates(sa_sc, qi, True)

    @pl.when(qi % 2 == 1)
    def _():
        put_scores(sb_sc, qi)
        updates(sa_sc, qi - 1, False)
        updates(sb_sc, qi, True)

    lam = (jnp.exp(jnp.sum(lq1_ref[...] * lk1_ref[...], axis=-1, keepdims=True))
           - jnp.exp(jnp.sum(lq2_ref[...] * lk2_ref[...], axis=-1, keepdims=True)) + lambda_init)
    o_t = acc_sc[0] * (1.0 / l_sc[0]) - lam * (acc_sc[1] * (1.0 / l_sc[1]))
    o_ref[0] = (_rms_rows(o_t.T, sw_ref[...]) * (1.0 - lambda_init)).astype(o_ref.dtype)


def diff_attention(qk, v_t, lq1, lk1, lq2, lk2, subln_w, *, layer_idx, tq=512):
    B, S, _ = qk.shape
    tq = min(tq, S)
    lambda_init = 0.8 - 0.6 * math.exp(-0.3 * layer_idx)
    pw = 2 * A_HEAD_DIM
    vec = lambda a: a.reshape(1, -1).astype(F32)
    small = lambda n: pl.BlockSpec((1, n), lambda b, h, i: (0, 0))
    kern = functools.partial(_diff_attn_kernel, tq=tq, lambda_init=lambda_init)
    return pl.pallas_call(
        kern,
        grid=(B, A_HEADS, S // tq),
        in_specs=[
            pl.BlockSpec((1, tq, pw), lambda b, h, i: (b, i, h)),
            pl.BlockSpec((1, S, pw), lambda b, h, i: (b, 0, A_HEADS + h)),
            pl.BlockSpec((pw, S), lambda b, h, i: (h, b)),
            small(A_HEAD_DIM), small(A_HEAD_DIM), small(A_HEAD_DIM), small(A_HEAD_DIM), small(pw),
        ],
        out_specs=pl.BlockSpec((1, tq, pw), lambda b, h, i: (b, i, h)),
        out_shape=jax.ShapeDtypeStruct((B, S, D_MODEL), BF16),
        scratch_shapes=[
            pltpu.VMEM((2, 1, tq), F32),
            pltpu.VMEM((2, 1, tq), F32),
            pltpu.VMEM((2, pw, tq), F32),
            pltpu.VMEM((2, tq, tq), F32),
            pltpu.VMEM((2, tq, tq), F32),
        ],
        compiler_params=_params("parallel", "parallel", "arbitrary"),
        name="diff_attention",
    )(qk, qk, v_t, vec(lq1), vec(lk1), vec(lq2), vec(lk2), vec(subln_w))


def _retention_kernel(q_ref, k_ref, v_ref, g_ref, cos_ref, sin_ref, dm_ref, xi_ref, zeta_ref, cd_ref,
                      gw_ref, o_ref, r_sc, *, tr, chunk):
    @pl.when(pl.program_id(2) == 0)
    def _():
        r_sc[...] = jnp.zeros_like(r_sc)

    half = R_KDIM // 2
    dmask = dm_ref[0]
    xi = xi_ref[0]
    zeta = zeta_ref[0]
    cdecay = cd_ref[0, 0:1, 0:1]
    gw = gw_ref[0]

    def rotate(t, c, s):
        te, to = t[:, :half], t[:, half:]
        return te * c - to * s, to * c + te * s

    for ci in range(tr // chunk):
        rows = slice(ci * chunk, (ci + 1) * chunk)
        c = cos_ref[rows, :]
        s = sin_ref[rows, :]
        qe, qo = rotate(q_ref[0, rows, :].astype(F32), c, s)
        ke, ko = rotate(k_ref[0, rows, :].astype(F32) * (R_KDIM ** -0.5), c, s)
        v = v_ref[0, rows, :]
        q_r = jnp.concatenate([qe, qo], axis=1).astype(BF16)
        k_r = jnp.concatenate([ke, ko], axis=1).astype(BF16)
        q_x = jnp.concatenate([qe * xi, qo * xi], axis=1).astype(BF16)
        k_z = jnp.concatenate([ke * zeta, ko * zeta], axis=1).astype(BF16)
        r_old = r_sc[...]
        sc = lax.dot_general(q_r, k_r, NT_DIMS, preferred_element_type=F32) * dmask
        o = (jnp.dot(sc.astype(BF16), v, preferred_element_type=F32)
             + jnp.dot(q_x, r_old.astype(BF16), preferred_element_type=F32))
        r_sc[...] = r_old * cdecay + lax.dot_general(k_z, v, TN_DIMS, preferred_element_type=F32)
        gate = _silu(g_ref[0, rows, :].astype(F32))
        o_ref[0, rows, :] = (gate * _rms_rows(o, gw)).astype(o_ref.dtype)


def retention(proj, gn_w, *, tr=1024, chunk=256):
    B, S, _ = proj.shape
    tr = min(tr, S)
    chunk = min(chunk, tr)
    half = R_KDIM // 2
    angle = 1.0 / (10000.0 ** jnp.linspace(0.0, 1.0, half, dtype=F32))
    ang = jnp.arange(S, dtype=F32)[:, None] * angle[None, :]
    cos, sin = jnp.cos(ang), jnp.sin(ang)
    log_g = jnp.log(1.0 - 2.0 ** (-5.0 - jnp.arange(R_HEADS, dtype=F32)))
    idx = jnp.arange(chunk, dtype=F32)
    rel = idx[:, None] - idx[None, :]
    dmask = jnp.where(rel[None] >= 0, jnp.exp(jnp.maximum(rel, 0.0)[None] * log_g[:, None, None]), 0.0)
    xi = jnp.exp((idx + 1.0)[None, :] * log_g[:, None])
    zeta = jnp.exp((chunk - 1.0 - idx)[None, :] * log_g[:, None])
    cdecay = jnp.exp(chunk * log_g)
    bc = lambda t: jnp.broadcast_to(t[:, :, None], (R_HEADS, chunk, half))
    cd = jnp.broadcast_to(cdecay[:, None, None], (R_HEADS, 8, LANES))
    nq = D_MODEL // R_KDIM
    nv = 2 * D_MODEL // R_VDIM
    kern = functools.partial(_retention_kernel, tr=tr, chunk=chunk)
    head = lambda shape: pl.BlockSpec(shape, lambda b, h, i: (h, 0, 0))
    return pl.pallas_call(
        kern,
        grid=(B, R_HEADS, S // tr),
        in_specs=[
            pl.BlockSpec((1, tr, R_KDIM), lambda b, h, i: (b, i, h)),
            pl.BlockSpec((1, tr, R_KDIM), lambda b, h, i: (b, i, nq + h)),
            pl.BlockSpec((1, tr, R_VDIM), lambda b, h, i: (b, i, nv + h)),
            pl.BlockSpec((1, tr, R_VDIM), lambda b, h, i: (b, i, nv + R_HEADS + h)),
            pl.BlockSpec((tr, half), lambda b, h, i: (i, 0)),
            pl.BlockSpec((tr, half), lambda b, h, i: (i, 0)),
            head((1, chunk, chunk)), head((1, chunk, half)), head((1, chunk, half)),
            head((1, 8, LANES)), head((1, 1, R_VDIM)),
        ],
        out_specs=pl.BlockSpec((1, tr, R_VDIM), lambda b, h, i: (b, i, h)),
        out_shape=jax.ShapeDtypeStruct((B, S, R_HEADS * R_VDIM), BF16),
        scratch_shapes=[pltpu.VMEM((R_KDIM, R_VDIM), F32)],
        compiler_params=_params("parallel", "parallel", "arbitrary"),
        name="retention",
    )(proj, proj, proj, proj, cos, sin, dmask, bc(xi), bc(zeta), cd, gn_w.reshape(R_HEADS, 1, R_VDIM).astype(F32))


def _mamba_kernel(z_ref, x_ref, b_ref, c_ref, dt_ref, wx_ref, wb_ref, wc_ref, bx_ref, bb_ref, bc_ref,
                  dtb_ref, alog_ref, dsk_ref, nw_ref, o_ref,
                  st_sc, tx_sc, tb_sc, tc_sc, *, chunk):
    L = chunk

    @pl.when(pl.program_id(2) == 0)
    def _():
        st_sc[...] = jnp.zeros_like(st_sc)
        tx_sc[...] = jnp.zeros_like(tx_sc)
        tb_sc[...] = jnp.zeros_like(tb_sc)
        tc_sc[...] = jnp.zeros_like(tc_sc)

    def conv_silu(cur_ref, tail_sc, w_ref, bias_ref):
        cur = cur_ref[0].astype(F32)
        ext = jnp.concatenate([tail_sc[...], cur], axis=0)
        tail_sc[...] = cur[L - 8:, :]
        w = w_ref[...]
        acc = bias_ref[...] + cur * w[M_CONV - 1:M_CONV, :]
        for kk in range(M_CONV - 1):
            sh = M_CONV - 1 - kk
            acc = acc + ext[8 - sh:8 - sh + L, :] * w[kk:kk + 1, :]
        return _silu(acc)

    xs = conv_silu(x_ref, tx_sc, wx_ref, bx_ref)
    bm = conv_silu(b_ref, tb_sc, wb_ref, bb_ref)
    cm = conv_silu(c_ref, tc_sc, wc_ref, bc_ref)

    raw = dt_ref[0, 0] + dtb_ref[0]
    dt_t = jnp.maximum(raw, 0.0) + jnp.log(1.0 + jnp.exp(-jnp.abs(raw)))
    a_t = dt_t * (-jnp.exp(alog_ref[0])) * math.log2(math.e)

    ri = lax.broadcasted_iota(jnp.int32, (L, L), 0)
    ci = lax.broadcasted_iota(jnp.int32, (L, L), 1)
    tril = ci <= ri
    eye = jnp.where(ri == ci, 1.0, 0.0).astype(BF16)
    lower = jnp.where(tril, 1.0, 0.0).astype(BF16)
    upper = jnp.where(ri <= ci, 1.0, 0.0).astype(BF16)

    def per_column(t):
        return jnp.concatenate([jnp.broadcast_to(t[r:r + 1, :], (M_HEADDIM, L)) for r in range(M_HPG)], axis=0)

    a_parts = _split3(a_t)
    acs_row = sum(jnp.dot(p, upper, preferred_element_type=F32) for p in a_parts)

    def expand_dot(mat01, parts):
        return sum(lax.dot_general(mat01, per_column(p.astype(F32)).astype(BF16), NT_DIMS,
                                   preferred_element_type=F32) for p in parts)

    acs_x = expand_dot(lower, a_parts)
    dt_x = expand_dot(eye, _split3(dt_t))

    xdt = xs * dt_x
    cb = lax.dot_general(cm.astype(BF16), bm.astype(BF16), NT_DIMS, preferred_element_type=F32)
    xdt_b = xdt.astype(BF16)
    parts = []
    for r in range(M_HPG):
        diff = acs_x[:, r * M_HEADDIM:r * M_HEADDIM + 1] - acs_row[r:r + 1, :]
        lmat = jnp.exp2(jnp.where(tril, diff, NEG_BIG))
        mr = (cb * lmat).astype(BF16)
        parts.append(jnp.dot(mr, xdt_b[:, r * M_HEADDIM:(r + 1) * M_HEADDIM], preferred_element_type=F32))
    y = jnp.concatenate(parts, axis=1)

    state = st_sc[...]
    y = y + jnp.exp2(acs_x) * jnp.dot(cm.astype(BF16), state.astype(BF16), preferred_element_type=F32)
    last = acs_x[L - 1:L, :]
    decay_end = jnp.exp2(last - acs_x)
    st_sc[...] = state * jnp.exp2(last) + lax.dot_general(
        bm.astype(BF16), (xdt * decay_end).astype(BF16), TN_DIMS, preferred_element_type=F32)

    y = y + xs * dsk_ref[...]
    y = y * _silu(z_ref[0].astype(F32))
    o_ref[0] = _rms_rows(y, nw_ref[...]).astype(o_ref.dtype)


def mamba_ssd(zx, dt_raw, conv_w, conv_b, dt_bias, a_log, d_skip, norm_w, *, chunk=256):
    B, S, _ = zx.shape
    chunk = min(chunk, S)
    G, W, N = M_GROUPS, M_GROUP_W, M_DSTATE
    dt_t = dt_raw[:, :, :M_HEADS].reshape(B, S, G, M_HPG).transpose(0, 2, 3, 1)
    xoff = M_D_INNER // W
    boff = 2 * M_D_INNER // N
    coff = boff + G
    cwb = M_D_INNER // N
    conv_w = conv_w.astype(F32)
    conv_b = conv_b.reshape(1, -1).astype(F32)
    per_head = lambda t: t.reshape(G, M_HPG, 1).astype(F32)
    dsk_x = jnp.repeat(d_skip.astype(F32), M_HEADDIM).reshape(1, M_D_INNER)
    kern = functools.partial(_mamba_kernel, chunk=chunk)
    return pl.pallas_call(
        kern,
        grid=(B, G, S // chunk),
        in_specs=[
            pl.BlockSpec((1, chunk, W), lambda b, g, c: (b, c, g)),
            pl.BlockSpec((1, chunk, W), lambda b, g, c: (b, c, xoff + g)),
            pl.BlockSpec((1, chunk, N), lambda b, g, c: (b, c, boff + g)),
            pl.BlockSpec((1, chunk, N), lambda b, g, c: (b, c, coff + g)),
            pl.BlockSpec((1, 1, M_HPG, chunk), lambda b, g, c: (b, g, 0, c)),
            pl.BlockSpec((M_CONV, W), lambda b, g, c: (0, g)),
            pl.BlockSpec((M_CONV, N), lambda b, g, c: (0, cwb + g)),
            pl.BlockSpec((M_CONV, N), lambda b, g, c: (0, cwb + G + g)),
            pl.BlockSpec((1, W), lambda b, g, c: (0, g)),
            pl.BlockSpec((1, N), lambda b, g, c: (0, cwb + g)),
            pl.BlockSpec((1, N), lambda b, g, c: (0, cwb + G + g)),
            pl.BlockSpec((1, M_HPG, 1), lambda b, g, c: (g, 0, 0)),
            pl.BlockSpec((1, M_HPG, 1), lambda b, g, c: (g, 0, 0)),
            pl.BlockSpec((1, W), lambda b, g, c: (0, g)),
            pl.BlockSpec((1, W), lambda b, g, c: (0, g)),
        ],
        out_specs=pl.BlockSpec((1, chunk, W), lambda b, g, c: (b, c, g)),
        out_shape=jax.ShapeDtypeStruct((B, S, M_D_INNER), BF16),
        scratch_shapes=[
            pltpu.VMEM((N, W), F32),
            pltpu.VMEM((8, W), F32),
            pltpu.VMEM((8, N), F32),
            pltpu.VMEM((8, N), F32),
        ],
        compiler_params=_params("parallel", "parallel", "arbitrary"),
        name="mamba_ssd",
    )(zx, zx, zx, zx, dt_t, conv_w, conv_w, conv_w, conv_b, conv_b, conv_b,
      per_head(dt_bias), per_head(a_log), dsk_x, norm_w.reshape(1, M_D_INNER).astype(F32))


def _dilated_kernel(q_ref, kp_ref, kc_ref, vp_ref, vc_ref, o_ref, lse_ref, lse_sc, *staging, dil, tiles, unroll,
                    mxu_sums):
    n = pl.program_id(1)
    h = pl.program_id(2)
    T = D_SPAN
    span = T * dil
    staged = dil > 1
    if staged:
        q_sc, k_sc, v_sc, o_sc = staging
        q_sc[...] = q_ref[0].astype(F32)
        k_sc[0:span, :] = kp_ref[0].astype(F32)
        k_sc[span:, :] = kc_ref[0].astype(F32)
        v_sc[0:span, :] = vp_ref[0].astype(F32)
        v_sc[span:, :] = vc_ref[0].astype(F32)

    @pl.when(h == 0)
    def _():
        lse_sc[...] = jnp.zeros_like(lse_sc)

    ri = lax.broadcasted_iota(jnp.int32, (T, T), 0)
    ci = lax.broadcasted_iota(jnp.int32, (T, T), 1)
    lane = lax.broadcasted_iota(jnp.int32, (T, LANES), 1)

    ones = jnp.ones((T, LANES), BF16)

    def attend(operands):
        scores = [(lax.dot_general(q, kp, NT_DIMS, preferred_element_type=F32),
                   lax.dot_general(q, kc, NT_DIMS, preferred_element_type=F32))
                  for q, kp, kc, _, _, _ in operands]
        probs = []
        for (sp, sc), (_, _, _, _, _, has_prev) in zip(scores, operands):
            sp = jnp.where((ci >= ri) & has_prev, sp, NEG_BIG)
            sc = jnp.where(ci <= ri, sc, NEG_BIG)
            mx = jnp.maximum(jnp.max(sp, axis=-1, keepdims=True), jnp.max(sc, axis=-1, keepdims=True))
            pp = jnp.exp(sp - mx)
            pc = jnp.exp(sc - mx)
            l = None if mxu_sums else jnp.sum(pp, axis=-1, keepdims=True) + jnp.sum(pc, axis=-1, keepdims=True)
            probs.append((pp.astype(BF16), pc.astype(BF16), mx, l))
        outs = []
        for (pp, pc, mx, l), (_, _, _, vp, vc, _) in zip(probs, operands):
            o = jnp.dot(pp, vp, preferred_element_type=F32) + jnp.dot(pc, vc, preferred_element_type=F32)
            if mxu_sums:
                l = jnp.dot(pp, ones, preferred_element_type=F32) + jnp.dot(pc, ones, preferred_element_type=F32)
            outs.append((o * (1.0 / l), mx + jnp.log(l)))
        return outs

    if staged:
        def group(jj, carry):
            starts, firsts = [], []
            for u in range(unroll):
                idx = jj * unroll + u
                i = 0 if tiles == 1 else idx // dil
                starts.append(i * span + (idx if tiles == 1 else idx % dil))
                firsts.append(jnp.logical_or(n > 0, i > 0))
            bf = lambda ref, st, off: ref[pl.ds(st + off, T, stride=dil), :].astype(BF16)
            outs = attend([(bf(q_sc, st, 0), bf(k_sc, st, 0), bf(k_sc, st, span), bf(v_sc, st, 0),
                            bf(v_sc, st, span), hp) for st, hp in zip(starts, firsts)])
            for st, (o, lse) in zip(starts, outs):
                o_sc[pl.ds(st, T, stride=dil), :] = o
                lse_sc[pl.ds(st, T, stride=dil), :] = jnp.where(
                    lane == h, lse, lse_sc[pl.ds(st, T, stride=dil), :])
            return carry

        if dil * tiles == unroll:
            group(0, 0)
        else:
            lax.fori_loop(0, dil * tiles // unroll, group, 0)
        o_ref[0] = o_sc[...].astype(o_ref.dtype)
    else:
        for i0 in range(0, tiles, unroll):
            operands = []
            for i in range(i0, i0 + unroll):
                cur = slice(i * T, (i + 1) * T)
                prev = slice((i - 1) * T, i * T)
                kp = kp_ref[0] if i == 0 else kc_ref[0, prev, :]
                vp = vp_ref[0] if i == 0 else vc_ref[0, prev, :]
                operands.append((q_ref[0, cur, :], kp, kc_ref[0, cur, :], vp, vc_ref[0, cur, :],
                                 jnp.logical_or(n > 0, i > 0)))
            for i, (o, lse) in zip(range(i0, i0 + unroll), attend(operands)):
                cur = slice(i * T, (i + 1) * T)
                o_ref[0, cur, :] = o.astype(o_ref.dtype)
                lse_sc[cur, :] = jnp.where(lane == h, lse, lse_sc[cur, :])

    @pl.when(h == D_HEADS - 1)
    def _():
        lse_ref[0] = lse_sc[...]


def dilated_group(qkv, g, dil, *, tiles, unroll, mxu_sums):
    B, S, C = qkv.shape
    assert (dil * tiles) % unroll == 0
    span = D_SPAN * dil
    tb = span * tiles
    hd = D_HEAD_DIM
    col = lambda t: (g * 3 + t) * D_HEADS
    cur = lambda t: pl.BlockSpec((1, tb, hd), lambda b, n, h: (b, n, col(t) + h))
    prev = lambda t: pl.BlockSpec((1, span, hd), lambda b, n, h: (b, jnp.maximum(n * tiles - 1, 0), col(t) + h))
    kern = functools.partial(_dilated_kernel, dil=dil, tiles=tiles, unroll=unroll, mxu_sums=mxu_sums)
    staging = [] if dil == 1 else [
        pltpu.VMEM((tb, hd), F32),
        pltpu.VMEM((span + tb, hd), F32),
        pltpu.VMEM((span + tb, hd), F32),
        pltpu.VMEM((tb, hd), F32),
    ]
    o, lse = pl.pallas_call(
        kern,
        grid=(B, S // tb, D_HEADS),
        in_specs=[cur(0), prev(1), cur(1), prev(2), cur(2)],
        out_specs=[
            pl.BlockSpec((1, tb, hd), lambda b, n, h: (b, n, h)),
            pl.BlockSpec((1, tb, LANES), lambda b, n, h: (b, n, 0)),
        ],
        out_shape=[
            jax.ShapeDtypeStruct((B, S, D_HEADS * hd), BF16),
            jax.ShapeDtypeStruct((B, S, LANES), F32),
        ],
        scratch_shapes=[pltpu.VMEM((tb, LANES), F32)] + staging,
        compiler_params=_params("parallel", "parallel", "arbitrary"),
        name="dilated_attention",
    )(qkv, qkv, qkv, qkv, qkv)
    return o.reshape(B * S, D_HEADS * hd), lse.reshape(B * S, LANES)


def _combine_kernel(o0_ref, o1_ref, o2_ref, l0_ref, l1_ref, l2_ref, o_ref):
    l0, l1, l2 = l0_ref[...], l1_ref[...], l2_ref[...]
    mx = jnp.maximum(jnp.maximum(l0, l1), l2)
    e0, e1, e2 = jnp.exp(l0 - mx), jnp.exp(l1 - mx), jnp.exp(l2 - mx)
    inv = 1.0 / (e0 + e1 + e2)
    ws = (e0 * inv, e1 * inv, e2 * inv)
    for h in range(D_HEADS):
        hs = slice(h * D_HEAD_DIM, (h + 1) * D_HEAD_DIM)
        acc = ws[0][:, h:h + 1] * o0_ref[:, hs].astype(F32)
        acc = acc + ws[1][:, h:h + 1] * o1_ref[:, hs].astype(F32)
        acc = acc + ws[2][:, h:h + 1] * o2_ref[:, hs].astype(F32)
        o_ref[:, hs] = acc.astype(o_ref.dtype)


def combine_groups(outs, lses, *, tm=512):
    T, W = outs[0].shape
    tm = min(tm, T)
    wide = pl.BlockSpec((tm, W), lambda i: (i, 0))
    narrow = pl.BlockSpec((tm, LANES), lambda i: (i, 0))
    return pl.pallas_call(
        _combine_kernel,
        grid=(T // tm,),
        in_specs=[wide, wide, wide, narrow, narrow, narrow],
        out_specs=wide,
        out_shape=jax.ShapeDtypeStruct((T, W), BF16),
        compiler_params=_params("parallel"),
        name="combine_groups",
    )(*outs, *lses)


def _deinterleave_heads(w, heads, dim):
    k = w.shape[0]
    return w.reshape(k, heads, dim // 2, 2).transpose(0, 1, 3, 2).reshape(k, heads * dim)


def mixer_a(xr, B, S, nw, a_w_in, a_q_norm_w, a_k_norm_w, lq1, lk1, lq2, lk2, a_subln_w, a_w_out, *, layer_idx):
    T, D = xr.shape
    colw = jnp.concatenate([
        jnp.tile(a_q_norm_w.astype(F32) * (A_HEAD_DIM ** -0.5 * math.log2(math.e)), 2 * A_HEADS),
        jnp.tile(a_k_norm_w.astype(F32), 2 * A_HEADS)]).reshape(1, 2 * D)
    qk = norm_matmul(xr, nw, a_w_in[:, :2 * D].astype(BF16), colw)
    v_t = norm_matmul_t(xr, nw, a_w_in[:, 2 * D:].T.astype(BF16))
    o = diff_attention(qk.reshape(B, S, 2 * D), v_t, lq1, lk1, lq2, lk2, a_subln_w, layer_idx=layer_idx)
    return matmul_residual(o.reshape(T, D), a_w_out.astype(BF16), xr)


def mixer_b(xr, B, S, nw, b_w_in, b_gn_w, b_w_out):
    T, D = xr.shape
    w_in = jnp.concatenate([
        _deinterleave_heads(b_w_in[:, :D], R_HEADS, R_KDIM),
        _deinterleave_heads(b_w_in[:, D:2 * D], R_HEADS, R_KDIM),
        b_w_in[:, 2 * D:]], axis=1)
    proj = norm_matmul(xr, nw, w_in.astype(BF16))
    o = retention(proj.reshape(B, S, -1), b_gn_w)
    return matmul_residual(o.reshape(T, -1), b_w_out.astype(BF16), xr)


def mixer_c(xr, B, S, nw, c_w_in, c_conv_w, c_conv_b, c_dt_bias, c_a_log, c_d_skip, c_norm_w, c_w_out):
    T, D = xr.shape
    n_main = 2 * M_D_INNER + 2 * M_GROUPS * M_DSTATE
    zx = norm_matmul(xr, nw, c_w_in[:, :n_main].astype(BF16))
    w_dt = jnp.pad(c_w_in[:, n_main:], ((0, 0), (0, LANES - M_HEADS)))
    dt_raw = norm_matmul(xr, nw, w_dt.astype(BF16), tn=LANES, out_dtype=F32)
    y = mamba_ssd(zx.reshape(B, S, n_main), dt_raw.reshape(B, S, LANES), c_conv_w, c_conv_b, c_dt_bias,
                  c_a_log, c_d_skip, c_norm_w)
    return matmul_residual(y.reshape(T, M_D_INNER), c_w_out.astype(BF16), xr)


def mixer_d(xr, B, S, nw, d_w_in, d_q_norm_w, d_k_norm_w, d_w_out):
    T, D = xr.shape
    scale = D_HEAD_DIM ** -0.5
    colw = jnp.concatenate([
        jnp.concatenate([jnp.tile(d_q_norm_w[g].astype(F32) * scale, D_HEADS),
                         jnp.tile(d_k_norm_w[g].astype(F32), D_HEADS),
                         jnp.ones((D,), F32)])
        for g in range(len(D_PATTERNS))]).reshape(1, -1)
    qkv = norm_matmul(xr, nw, d_w_in.astype(BF16), colw).reshape(B, S, -1)
    outs, lses = zip(*[dilated_group(qkv, g, dil, tiles=max(1, 8 // dil), unroll=8 if dil < 16 else 4,
                                     mxu_sums=dil < 16)
                       for g, (_, dil) in enumerate(D_PATTERNS)])
    o = combine_groups(outs, lses)
    return matmul_residual(o, d_w_out.astype(BF16), xr)


@jax.jit
def kernel(x, norm1_w, norm2_w, mlp_w1, mlp_w2, a_w_in, a_q_norm_w, a_k_norm_w, a_lambda_q1, a_lambda_k1, a_lambda_q2, a_lambda_k2, a_subln_w, a_w_out, b_w_in, b_gn_w, b_w_out, c_w_in, c_conv_w, c_conv_b, c_dt_bias, c_a_log, c_d_skip, c_norm_w, c_w_out, d_w_in, d_q_norm_w, d_k_norm_w, d_w_out):
    B, S, D = x.shape
    xr = x.reshape(B * S, D)
    ffn = lambda t, i: mlp(t, norm2_w[i], mlp_w1[i].astype(BF16), mlp_w2[i].astype(BF16))
    xr = mixer_a(xr, B, S, norm1_w[0], a_w_in, a_q_norm_w, a_k_norm_w, a_lambda_q1, a_lambda_k1,
                 a_lambda_q2, a_lambda_k2, a_subln_w, a_w_out, layer_idx=0)
    xr = ffn(xr, 0)
    xr = mixer_b(xr, B, S, norm1_w[1], b_w_in, b_gn_w, b_w_out)
    xr = ffn(xr, 1)
    xr = mixer_c(xr, B, S, norm1_w[2], c_w_in, c_conv_w, c_conv_b, c_dt_bias, c_a_log, c_d_skip, c_norm_w,
                 c_w_out)
    xr = ffn(xr, 2)
    xr = mixer_d(xr, B, S, norm1_w[3], d_w_in, d_q_norm_w, d_k_norm_w, d_w_out)
    xr = ffn(xr, 3)
    return xr.reshape(B, S, D)
```

```python
import functools
import math

import jax
import jax.numpy as jnp
from jax import lax
from jax.experimental import pallas as pl
from jax.experimental.pallas import tpu as pltpu

F32 = jnp.float32
BF16 = jnp.bfloat16

EPS = 1e-6
D_MODEL = 2048
D_FF = 4 * D_MODEL
LANES = 128
MXU_COLS = 256

A_HEAD_DIM = 128
A_HEADS = 8
R_HEADS = 8
R_KDIM = 256
R_VDIM = 512
M_D_INNER = 4096
M_HEADDIM = 64
M_HEADS = 64
M_GROUPS = 8
M_DSTATE = 128
M_CONV = 4
M_GROUP_W = M_D_INNER // M_GROUPS
M_HPG = M_HEADS // M_GROUPS
D_HEADS = 16
D_HEAD_DIM = 128
D_PATTERNS = ((128, 1), (512, 4), (2048, 16))
D_SPAN = 128

VMEM_LIMIT_BYTES = 56 * 1024 * 1024
NEG_BIG = -1e30

NT_DIMS = (((1,), (1,)), ((), ()))
TN_DIMS = (((0,), (0,)), ((), ()))


def _params(*sem):
    return pltpu.CompilerParams(dimension_semantics=sem, vmem_limit_bytes=VMEM_LIMIT_BYTES)


def _silu(v):
    return v * (1.0 / (1.0 + jnp.exp(-v)))


def _rms_rows(v, w):
    ms = jnp.mean(v * v, axis=-1, keepdims=True)
    return v * lax.rsqrt(ms + EPS) * w


def _split3(v):
    p1 = v.astype(BF16)
    r1 = v - p1.astype(F32)
    p2 = r1.astype(BF16)
    return p1, p2, (r1 - p2.astype(F32)).astype(BF16)


def _norm_matmul_kernel(x_ref, nw_ref, w_ref, cw_ref, o_ref, h_sc, *, tn, head_norm, period, count):
    j = pl.program_id(1)

    @pl.when(j == 0)
    def _():
        h_sc[...] = _rms_rows(x_ref[...], nw_ref[...]).astype(BF16)

    def plain():
        o_ref[...] = jnp.dot(h_sc[...], w_ref[...], preferred_element_type=F32).astype(o_ref.dtype)

    if not head_norm:
        plain()
        return
    is_norm = (j % period) < count

    @pl.when(is_norm)
    def _():
        sub = min(tn, MXU_COLS)
        for c in range(tn // sub):
            acc = jnp.dot(h_sc[...], w_ref[:, c * sub:(c + 1) * sub], preferred_element_type=F32)
            for d in range(sub // LANES):
                sl = slice(c * sub + d * LANES, c * sub + (d + 1) * LANES)
                o_ref[:, sl] = _rms_rows(acc[:, d * LANES:(d + 1) * LANES], cw_ref[:, sl]).astype(o_ref.dtype)

    pl.when(jnp.logical_not(is_norm))(plain)


def norm_matmul(x, nw, w, colw=None, *, tm=1024, tn=1024, out_dtype=BF16):
    T, K = x.shape
    N = w.shape[1]
    tm = min(tm, T)
    tn = min(tn, N)
    head_norm = colw is not None
    if colw is None:
        colw = jnp.ones((1, N), F32)
    period = 3 * D_MODEL // tn
    count = 2 * D_MODEL // tn
    kern = functools.partial(_norm_matmul_kernel, tn=tn, head_norm=head_norm, period=period, count=count)
    return pl.pallas_call(
        kern,
        grid=(T // tm, N // tn),
        in_specs=[
            pl.BlockSpec((tm, K), lambda i, j: (i, 0)),
            pl.BlockSpec((1, K), lambda i, j: (0, 0)),
            pl.BlockSpec((K, tn), lambda i, j: (0, j)),
            pl.BlockSpec((1, tn), lambda i, j: (0, j)),
        ],
        out_specs=pl.BlockSpec((tm, tn), lambda i, j: (i, j)),
        out_shape=jax.ShapeDtypeStruct((T, N), out_dtype),
        scratch_shapes=[pltpu.VMEM((tm, K), BF16)],
        compiler_params=_params("parallel", "arbitrary"),
        name="norm_matmul",
    )(x, nw.reshape(1, K), w, colw)


def _norm_matmul_t_kernel(x_ref, nw_ref, wt_ref, o_ref, h_sc):
    @pl.when(pl.program_id(1) == 0)
    def _():
        h_sc[...] = _rms_rows(x_ref[...], nw_ref[...]).astype(BF16)

    o_ref[...] = lax.dot_general(wt_ref[...], h_sc[...], NT_DIMS, preferred_element_type=F32).astype(o_ref.dtype)


def norm_matmul_t(x, nw, w_t, *, tm=1024, tn=1024):
    T, K = x.shape
    N = w_t.shape[0]
    tm = min(tm, T)
    return pl.pallas_call(
        _norm_matmul_t_kernel,
        grid=(T // tm, N // tn),
        in_specs=[
            pl.BlockSpec((tm, K), lambda i, j: (i, 0)),
            pl.BlockSpec((1, K), lambda i, j: (0, 0)),
            pl.BlockSpec((tn, K), lambda i, j: (j, 0)),
        ],
        out_specs=pl.BlockSpec((tn, tm), lambda i, j: (j, i)),
        out_shape=jax.ShapeDtypeStruct((N, T), BF16),
        scratch_shapes=[pltpu.VMEM((tm, K), BF16)],
        compiler_params=_params("parallel", "arbitrary"),
        name="norm_matmul_t",
    )(x, nw.reshape(1, K), w_t)


def _matmul_residual_kernel(a_ref, w_ref, r_ref, o_ref):
    o_ref[...] = r_ref[...] + jnp.dot(a_ref[...], w_ref[...], preferred_element_type=F32)


def matmul_residual(a, w, res, *, tm=1024, tn=1024):
    T, K = a.shape
    N = w.shape[1]
    tm = min(tm, T)
    return pl.pallas_call(
        _matmul_residual_kernel,
        grid=(T // tm, N // tn),
        in_specs=[
            pl.BlockSpec((tm, K), lambda i, j: (i, 0)),
            pl.BlockSpec((K, tn), lambda i, j: (0, j)),
            pl.BlockSpec((tm, tn), lambda i, j: (i, j)),
        ],
        out_specs=pl.BlockSpec((tm, tn), lambda i, j: (i, j)),
        out_shape=jax.ShapeDtypeStruct((T, N), F32),
        compiler_params=_params("parallel", "arbitrary"),
        name="matmul_residual",
    )(a, w, res)


def _mlp_kernel(x_ref, nw_ref, w1_ref, w2_ref, o_ref, h_sc):
    f = pl.program_id(1)

    @pl.when(f == 0)
    def _():
        x = x_ref[...]
        h_sc[...] = _rms_rows(x, nw_ref[...]).astype(BF16)
        o_ref[...] = x

    u = jnp.dot(h_sc[...], w1_ref[...], preferred_element_type=F32)
    u = jnp.square(jnp.maximum(u, 0.0)).astype(BF16)
    o_ref[...] += jnp.dot(u, w2_ref[...], preferred_element_type=F32)


def mlp(x, nw, w1, w2, *, tm=512, tf=1024):
    T, D = x.shape
    FF = w1.shape[1]
    tm = min(tm, T)
    return pl.pallas_call(
        _mlp_kernel,
        grid=(T // tm, FF // tf),
        in_specs=[
            pl.BlockSpec((tm, D), lambda i, f: (i, 0)),
            pl.BlockSpec((1, D), lambda i, f: (0, 0)),
            pl.BlockSpec((D, tf), lambda i, f: (0, f)),
            pl.BlockSpec((tf, D), lambda i, f: (f, 0)),
        ],
        out_specs=pl.BlockSpec((tm, D), lambda i, f: (i, 0)),
        out_shape=jax.ShapeDtypeStruct((T, D), F32),
        scratch_shapes=[pltpu.VMEM((tm, D), BF16)],
        compiler_params=_params("parallel", "arbitrary"),
        name="mlp",
    )(x, nw.reshape(1, D), w1, w2)


def _diff_attn_kernel(q_ref, k_ref, vt_ref, lq1_ref, lk1_ref, lq2_ref, lk2_ref, sw_ref, o_ref,
                      m_sc, l_sc, acc_sc, sa_sc, sb_sc, *, tq, lambda_init):
    qi = pl.program_id(2)
    m_sc[...] = jnp.full_like(m_sc, NEG_BIG)
    l_sc[...] = jnp.zeros_like(l_sc)
    acc_sc[...] = jnp.zeros_like(acc_sc)
    key = lax.broadcasted_iota(jnp.int32, (tq, tq), 0)
    qry = lax.broadcasted_iota(jnp.int32, (tq, tq), 1)

    def scores(m, start):
        hs = slice(m * A_HEAD_DIM, (m + 1) * A_HEAD_DIM)
        return lax.dot_general(k_ref[0, pl.ds(start, tq), hs], q_ref[0, :, hs], NT_DIMS,
                               preferred_element_type=F32)

    def update(m, s, start, masked):
        if masked:
            s = jnp.where(key <= qry, s, NEG_BIG)
        m_prev = m_sc[m]
        m_new = jnp.maximum(m_prev, jnp.max(s, axis=0, keepdims=True))
        alpha = jnp.exp2(m_prev - m_new)
        p = jnp.exp2(s - m_new)
        m_sc[m] = m_new
        l_sc[m] = alpha * l_sc[m] + jnp.sum(p, axis=0, keepdims=True)
        acc_sc[m] = alpha * acc_sc[m] + jnp.dot(vt_ref[:, pl.ds(start, tq)], p.astype(BF16),
                                                preferred_element_type=F32)

    def put_scores(s_sc, j):
        start = pl.multiple_of(j * tq, tq)
        s_sc[0] = scores(0, start)
        s_sc[1] = scores(1, start)

    def updates(s_sc, j, masked):
        start = pl.multiple_of(j * tq, tq)
        update(0, s_sc[0], start, masked)
        update(1, s_sc[1], start, masked)

    put_scores(sa_sc, 0)

    def body(jj, carry):
        j = 2 * jj
        put_scores(sb_sc, j + 1)
        updates(sa_sc, j, False)
        put_scores(sa_sc, j + 2)
        updates(sb_sc, j + 1, False)
        return carry

    lax.fori_loop(0, qi // 2, body, 0)

    @pl.when(qi % 2 == 0)
    def _():
        updates(sa_sc, qi, True)

    @pl.when(qi % 2 == 1)
    def _():
        put_scores(sb_sc, qi)
        updates(sa_sc, qi - 1, False)
        updates(sb_sc, qi, True)

    lam = (jnp.exp(jnp.sum(lq1_ref[...] * lk1_ref[...], axis=-1, keepdims=True))
           - jnp.exp(jnp.sum(lq2_ref[...] * lk2_ref[...], axis=-1, keepdims=True)) + lambda_init)
    o_t = acc_sc[0] * (1.0 / l_sc[0]) - lam * (acc_sc[1] * (1.0 / l_sc[1]))
    o_ref[0] = (_rms_rows(o_t.T, sw_ref[...]) * (1.0 - lambda_init)).astype(o_ref.dtype)


def diff_attention(qk, v_t, lq1, lk1, lq2, lk2, subln_w, *, layer_idx, tq=512):
    B, S, _ = qk.shape
    tq = min(tq, S)
    lambda_init = 0.8 - 0.6 * math.exp(-0.3 * layer_idx)
    pw = 2 * A_HEAD_DIM
    vec = lambda a: a.reshape(1, -1).astype(F32)
    small = lambda n: pl.BlockSpec((1, n), lambda b, h, i: (0, 0))
    kern = functools.partial(_diff_attn_kernel, tq=tq, lambda_init=lambda_init)
    return pl.pallas_call(
        kern,
        grid=(B, A_HEADS, S // tq),
        in_specs=[
            pl.BlockSpec((1, tq, pw), lambda b, h, i: (b, i, h)),
            pl.BlockSpec((1, S, pw), lambda b, h, i: (b, 0, A_HEADS + h)),
            pl.BlockSpec((pw, S), lambda b, h, i: (h, b)),
            small(A_HEAD_DIM), small(A_HEAD_DIM), small(A_HEAD_DIM), small(A_HEAD_DIM), small(pw),
        ],
        out_specs=pl.BlockSpec((1, tq, pw), lambda b, h, i: (b, i, h)),
        out_shape=jax.ShapeDtypeStruct((B, S, D_MODEL), BF16),
        scratch_shapes=[
            pltpu.VMEM((2, 1, tq), F32),
            pltpu.VMEM((2, 1, tq), F32),
            pltpu.VMEM((2, pw, tq), F32),
            pltpu.VMEM((2, tq, tq), F32),
            pltpu.VMEM((2, tq, tq), F32),
        ],
        compiler_params=_params("parallel", "parallel", "arbitrary"),
        name="diff_attention",
    )(qk, qk, v_t, vec(lq1), vec(lk1), vec(lq2), vec(lk2), vec(subln_w))


def _retention_kernel(q_ref, k_ref, v_ref, g_ref, cos_ref, sin_ref, dm_ref, xi_ref, zeta_ref, cd_ref,
                      gw_ref, o_ref, r_sc, *, tr, chunk):
    @pl.when(pl.program_id(2) == 0)
    def _():
        r_sc[...] = jnp.zeros_like(r_sc)

    half = R_KDIM // 2
    dmask = dm_ref[0]
    xi = xi_ref[0]
    zeta = zeta_ref[0]
    cdecay = cd_ref[0, 0:1, 0:1]
    gw = gw_ref[0]

    def rotate(t, c, s):
        te, to = t[:, :half], t[:, half:]
        return te * c - to * s, to * c + te * s

    for ci in range(tr // chunk):
        rows = slice(ci * chunk, (ci + 1) * chunk)
        c = cos_ref[rows, :]
        s = sin_ref[rows, :]
        qe, qo = rotate(q_ref[0, rows, :].astype(F32), c, s)
        ke, ko = rotate(k_ref[0, rows, :].astype(F32) * (R_KDIM ** -0.5), c, s)
        v = v_ref[0, rows, :]
        q_r = jnp.concatenate([qe, qo], axis=1).astype(BF16)
        k_r = jnp.concatenate([ke, ko], axis=1).astype(BF16)
        q_x = jnp.concatenate([qe * xi, qo * xi], axis=1).astype(BF16)
        k_z = jnp.concatenate([ke * zeta, ko * zeta], axis=1).astype(BF16)
        r_old = r_sc[...]
        sc = lax.dot_general(q_r, k_r, NT_DIMS, preferred_element_type=F32) * dmask
        o = (jnp.dot(sc.astype(BF16), v, preferred_element_type=F32)
             + jnp.dot(q_x, r_old.astype(BF16), preferred_element_type=F32))
        r_sc[...] = r_old * cdecay + lax.dot_general(k_z, v, TN_DIMS, preferred_element_type=F32)
        gate = _silu(g_ref[0, rows, :].astype(F32))
        o_ref[0, rows, :] = (gate * _rms_rows(o, gw)).astype(o_ref.dtype)


def retention(proj, gn_w, *, tr=1024, chunk=256):
    B, S, _ = proj.shape
    tr = min(tr, S)
    chunk = min(chunk, tr)
    half = R_KDIM // 2
    angle = 1.0 / (10000.0 ** jnp.linspace(0.0, 1.0, half, dtype=F32))
    ang = jnp.arange(S, dtype=F32)[:, None] * angle[None, :]
    cos, sin = jnp.cos(ang), jnp.sin(ang)
    log_g = jnp.log(1.0 - 2.0 ** (-5.0 - jnp.arange(R_HEADS, dtype=F32)))
    idx = jnp.arange(chunk, dtype=F32)
    rel = idx[:, None] - idx[None, :]
    dmask = jnp.where(rel[None] >= 0, jnp.exp(jnp.maximum(rel, 0.0)[None] * log_g[:, None, None]), 0.0)
    xi = jnp.exp((idx + 1.0)[None, :] * log_g[:, None])
    zeta = jnp.exp((chunk - 1.0 - idx)[None, :] * log_g[:, None])
    cdecay = jnp.exp(chunk * log_g)
    bc = lambda t: jnp.broadcast_to(t[:, :, None], (R_HEADS, chunk, half))
    cd = jnp.broadcast_to(cdecay[:, None, None], (R_HEADS, 8, LANES))
    nq = D_MODEL // R_KDIM
    nv = 2 * D_MODEL // R_VDIM
    kern = functools.partial(_retention_kernel, tr=tr, chunk=chunk)
    head = lambda shape: pl.BlockSpec(shape, lambda b, h, i: (h, 0, 0))
    return pl.pallas_call(
        kern,
        grid=(B, R_HEADS, S // tr),
        in_specs=[
            pl.BlockSpec((1, tr, R_KDIM), lambda b, h, i: (b, i, h)),
            pl.BlockSpec((1, tr, R_KDIM), lambda b, h, i: (b, i, nq + h)),
            pl.BlockSpec((1, tr, R_VDIM), lambda b, h, i: (b, i, nv + h)),
            pl.BlockSpec((1, tr, R_VDIM), lambda b, h, i: (b, i, nv + R_HEADS + h)),
            pl.BlockSpec((tr, half), lambda b, h, i: (i, 0)),
            pl.BlockSpec((tr, half), lambda b, h, i: (i, 0)),
            head((1, chunk, chunk)), head((1, chunk, half)), head((1, chunk, half)),
            head((1, 8, LANES)), head((1, 1, R_VDIM)),
        ],
        out_specs=pl.BlockSpec((1, tr, R_VDIM), lambda b, h, i: (b, i, h)),
        out_shape=jax.ShapeDtypeStruct((B, S, R_HEADS * R_VDIM), BF16),
        scratch_shapes=[pltpu.VMEM((R_KDIM, R_VDIM), F32)],
        compiler_params=_params("parallel", "parallel", "arbitrary"),
        name="retention",
    )(proj, proj, proj, proj, cos, sin, dmask, bc(xi), bc(zeta), cd, gn_w.reshape(R_HEADS, 1, R_VDIM).astype(F32))


def _mamba_kernel(z_ref, x_ref, b_ref, c_ref, dt_ref, wx_ref, wb_ref, wc_ref, bx_ref, bb_ref, bc_ref,
                  dtb_ref, alog_ref, dsk_ref, nw_ref, o_ref,
                  st_sc, tx_sc, tb_sc, tc_sc, *, chunk):
    L = chunk

    @pl.when(pl.program_id(2) == 0)
    def _():
        st_sc[...] = jnp.zeros_like(st_sc)
        for ext_sc in (tx_sc, tb_sc, tc_sc):
            ext_sc[0:8, :] = jnp.zeros((8, ext_sc.shape[1]), F32)

    def conv_silu(cur_ref, ext_sc, w_ref, bias_ref):
        ext_sc[8:, :] = cur_ref[0].astype(F32)
        w = w_ref[...]
        acc = bias_ref[...] + ext_sc[8:, :] * w[M_CONV - 1:M_CONV, :]
        for kk in range(M_CONV - 1):
            sh = M_CONV - 1 - kk
            acc = acc + ext_sc[8 - sh:8 - sh + L, :] * w[kk:kk + 1, :]
        ext_sc[0:8, :] = ext_sc[L:, :]
        return _silu(acc)

    raw = dt_ref[0, 0] + dtb_ref[0]
    dt_t = jnp.maximum(raw, 0.0) + jnp.log(1.0 + jnp.exp(-jnp.abs(raw)))
    a_t = dt_t * (-jnp.exp(alog_ref[0])) * math.log2(math.e)

    ri = lax.broadcasted_iota(jnp.int32, (L, L), 0)
    ci = lax.broadcasted_iota(jnp.int32, (L, L), 1)
    tril = ci <= ri
    eye = jnp.where(ri == ci, 1.0, 0.0).astype(BF16)
    lower = jnp.where(tril, 1.0, 0.0).astype(BF16)
    upper = jnp.where(ri <= ci, 1.0, 0.0).astype(BF16)

    def per_column(t):
        return jnp.concatenate([jnp.broadcast_to(t[r:r + 1, :], (M_HEADDIM, L)) for r in range(M_HPG)], axis=0)

    a_parts = _split3(a_t)
    acs_row = sum(jnp.dot(p, upper, preferred_element_type=F32) for p in a_parts)

    def expand_dot(mat01, parts):
        return sum(lax.dot_general(mat01, per_column(p.astype(F32)).astype(BF16), NT_DIMS,
                                   preferred_element_type=F32) for p in parts)

    acs_x = expand_dot(lower, a_parts)
    dt_x = expand_dot(eye, _split3(dt_t))

    xs = conv_silu(x_ref, tx_sc, wx_ref, bx_ref)
    bm = conv_silu(b_ref, tb_sc, wb_ref, bb_ref)
    cm = conv_silu(c_ref, tc_sc, wc_ref, bc_ref)

    xdt = xs * dt_x
    cb = lax.dot_general(cm.astype(BF16), bm.astype(BF16), NT_DIMS, preferred_element_type=F32)
    xdt_b = xdt.astype(BF16)
    parts = []
    for r in range(M_HPG):
        diff = acs_x[:, r * M_HEADDIM:r * M_HEADDIM + 1] - acs_row[r:r + 1, :]
        lmat = jnp.exp2(jnp.where(tril, diff, NEG_BIG))
        mr = (cb * lmat).astype(BF16)
        parts.append(jnp.dot(mr, xdt_b[:, r * M_HEADDIM:(r + 1) * M_HEADDIM], preferred_element_type=F32))
    y = jnp.concatenate(parts, axis=1)

    state = st_sc[...]
    y = y + jnp.exp2(acs_x) * jnp.dot(cm.astype(BF16), state.astype(BF16), preferred_element_type=F32)
    last = acs_x[L - 1:L, :]
    decay_end = jnp.exp2(last - acs_x)
    st_sc[...] = state * jnp.exp2(last) + lax.dot_general(
        bm.astype(BF16), (xdt * decay_end).astype(BF16), TN_DIMS, preferred_element_type=F32)

    y = y + xs * dsk_ref[...]
    y = y * _silu(z_ref[0].astype(F32))
    o_ref[0] = _rms_rows(y, nw_ref[...]).astype(o_ref.dtype)


def mamba_ssd(zx, dt_raw, conv_w, conv_b, dt_bias, a_log, d_skip, norm_w, *, chunk=256):
    B, S, _ = zx.shape
    chunk = min(chunk, S)
    G, W, N = M_GROUPS, M_GROUP_W, M_DSTATE
    dt_t = dt_raw[:, :, :M_HEADS].reshape(B, S, G, M_HPG).transpose(0, 2, 3, 1)
    xoff = M_D_INNER // W
    boff = 2 * M_D_INNER // N
    coff = boff + G
    cwb = M_D_INNER // N
    conv_w = conv_w.astype(F32)
    conv_b = conv_b.reshape(1, -1).astype(F32)
    per_head = lambda t: t.reshape(G, M_HPG, 1).astype(F32)
    dsk_x = jnp.repeat(d_skip.astype(F32), M_HEADDIM).reshape(1, M_D_INNER)
    kern = functools.partial(_mamba_kernel, chunk=chunk)
    return pl.pallas_call(
        kern,
        grid=(B, G, S // chunk),
        in_specs=[
            pl.BlockSpec((1, chunk, W), lambda b, g, c: (b, c, g)),
            pl.BlockSpec((1, chunk, W), lambda b, g, c: (b, c, xoff + g)),
            pl.BlockSpec((1, chunk, N), lambda b, g, c: (b, c, boff + g)),
            pl.BlockSpec((1, chunk, N), lambda b, g, c: (b, c, coff + g)),
            pl.BlockSpec((1, 1, M_HPG, chunk), lambda b, g, c: (b, g, 0, c)),
            pl.BlockSpec((M_CONV, W), lambda b, g, c: (0, g)),
            pl.BlockSpec((M_CONV, N), lambda b, g, c: (0, cwb + g)),
            pl.BlockSpec((M_CONV, N), lambda b, g, c: (0, cwb + G + g)),
            pl.BlockSpec((1, W), lambda b, g, c: (0, g)),
            pl.BlockSpec((1, N), lambda b, g, c: (0, cwb + g)),
            pl.BlockSpec((1, N), lambda b, g, c: (0, cwb + G + g)),
            pl.BlockSpec((1, M_HPG, 1), lambda b, g, c: (g, 0, 0)),
            pl.BlockSpec((1, M_HPG, 1), lambda b, g, c: (g, 0, 0)),
            pl.BlockSpec((1, W), lambda b, g, c: (0, g)),
            pl.BlockSpec((1, W), lambda b, g, c: (0, g)),
        ],
        out_specs=pl.BlockSpec((1, chunk, W), lambda b, g, c: (b, c, g)),
        out_shape=jax.ShapeDtypeStruct((B, S, M_D_INNER), BF16),
        scratch_shapes=[
            pltpu.VMEM((N, W), F32),
            pltpu.VMEM((8 + chunk, W), F32),
            pltpu.VMEM((8 + chunk, N), F32),
            pltpu.VMEM((8 + chunk, N), F32),
        ],
        compiler_params=_params("parallel", "parallel", "arbitrary"),
        name="mamba_ssd",
    )(zx, zx, zx, zx, dt_t, conv_w, conv_w, conv_w, conv_b, conv_b, conv_b,
      per_head(dt_bias), per_head(a_log), dsk_x, norm_w.reshape(1, M_D_INNER).astype(F32))


def _window_attend(operands, mxu_sums):
    T = D_SPAN
    ri = lax.broadcasted_iota(jnp.int32, (T, T), 0)
    ci = lax.broadcasted_iota(jnp.int32, (T, T), 1)
    ones = jnp.ones((T, LANES), BF16)
    scores = [(lax.dot_general(q, kp, NT_DIMS, preferred_element_type=F32),
               lax.dot_general(q, kc, NT_DIMS, preferred_element_type=F32))
              for q, kp, kc, _, _, _ in operands]
    probs = []
    for (sp, sc), (_, _, _, _, _, has_prev) in zip(scores, operands):
        sp = jnp.where((ci >= ri) & has_prev, sp, NEG_BIG)
        sc = jnp.where(ci <= ri, sc, NEG_BIG)
        mx = jnp.maximum(jnp.max(sp, axis=-1, keepdims=True), jnp.max(sc, axis=-1, keepdims=True))
        pp = jnp.exp(sp - mx)
        pc = jnp.exp(sc - mx)
        l = None if mxu_sums else jnp.sum(pp, axis=-1, keepdims=True) + jnp.sum(pc, axis=-1, keepdims=True)
        probs.append((pp.astype(BF16), pc.astype(BF16), mx, l))
    outs = []
    for (pp, pc, mx, l), (_, _, _, vp, vc, _) in zip(probs, operands):
        o = jnp.dot(pp, vp, preferred_element_type=F32) + jnp.dot(pc, vc, preferred_element_type=F32)
        if mxu_sums:
            l = jnp.dot(pp, ones, preferred_element_type=F32) + jnp.dot(pc, ones, preferred_element_type=F32)
        outs.append((o * (1.0 / l), mx + jnp.log(l)))
    return outs


def _dense_window_kernel(q_ref, kp_ref, kc_ref, vp_ref, vc_ref, o_ref, lse_ref, lse_sc, *, tiles, unroll, mxu_sums):
    n = pl.program_id(1)
    h = pl.program_id(2)
    T = D_SPAN

    @pl.when(h == 0)
    def _():
        lse_sc[...] = jnp.zeros_like(lse_sc)

    lane = lax.broadcasted_iota(jnp.int32, (T, LANES), 1)
    for i0 in range(0, tiles, unroll):
        operands = []
        for i in range(i0, i0 + unroll):
            cur = slice(i * T, (i + 1) * T)
            prev = slice((i - 1) * T, i * T)
            kp = kp_ref[0] if i == 0 else kc_ref[0, prev, :]
            vp = vp_ref[0] if i == 0 else vc_ref[0, prev, :]
            operands.append((q_ref[0, cur, :], kp, kc_ref[0, cur, :], vp, vc_ref[0, cur, :],
                             jnp.logical_or(n > 0, i > 0)))
        for i, (o, lse) in zip(range(i0, i0 + unroll), _window_attend(operands, mxu_sums)):
            cur = slice(i * T, (i + 1) * T)
            o_ref[0, cur, :] = o.astype(o_ref.dtype)
            lse_sc[cur, :] = jnp.where(lane == h, lse, lse_sc[cur, :])

    @pl.when(h == D_HEADS - 1)
    def _():
        lse_ref[0] = lse_sc[...]


def _strided_window_kernel(q_ref, k_ref, v_ref, o_ref, lse_ref, lse_sc, q_sc, k_sc, v_sc, o_sc, kprev_sc, vprev_sc,
                           *, dil, tiles, unroll, mxu_sums):
    n = pl.program_id(1)
    h = pl.program_id(2)
    T = D_SPAN
    span = T * dil
    q_sc[...] = q_ref[0].astype(F32)
    k_sc[...] = k_ref[0].astype(F32)
    v_sc[...] = v_ref[0].astype(F32)

    @pl.when(h == 0)
    def _():
        lse_sc[...] = jnp.zeros_like(lse_sc)

    @pl.when(n == 0)
    def _():
        kprev_sc[h] = jnp.zeros(kprev_sc.shape[1:], BF16)
        vprev_sc[h] = jnp.zeros(vprev_sc.shape[1:], BF16)

    lane = lax.broadcasted_iota(jnp.int32, (T, LANES), 1)
    strided = lambda ref, start: ref[pl.ds(start, T, stride=dil), :]

    def group(tile_ids):
        operands, places = [], []
        for idx in tile_ids:
            i, r = (0, idx) if tiles == 1 else divmod(idx, dil)
            start = i * span + r
            saved = pl.ds(pl.multiple_of(r * T, T), T)
            kc = strided(k_sc, start).astype(BF16)
            vc = strided(v_sc, start).astype(BF16)
            if i == 0:
                kp, vp = kprev_sc[h, saved, :], vprev_sc[h, saved, :]
            else:
                kp = strided(k_sc, start - span).astype(BF16)
                vp = strided(v_sc, start - span).astype(BF16)
            operands.append((strided(q_sc, start).astype(BF16), kp, kc, vp, vc, jnp.logical_or(n > 0, i > 0)))
            places.append((start, saved if i == tiles - 1 else None))
        for (start, saved), (_, _, kc, _, vc, _), (o, lse) in zip(places, operands,
                                                                   _window_attend(operands, mxu_sums)):
            o_sc[pl.ds(start, T, stride=dil), :] = o
            lse_sc[pl.ds(start, T, stride=dil), :] = jnp.where(lane == h, lse, strided(lse_sc, start))
            if saved is not None:
                kprev_sc[h, saved, :] = kc
                vprev_sc[h, saved, :] = vc

    if tiles == 1 and dil > unroll:
        def body(jj, carry):
            group([jj * unroll + u for u in range(unroll)])
            return carry
        lax.fori_loop(0, dil // unroll, body, 0)
    else:
        for j0 in range(0, dil * tiles, unroll):
            group(list(range(j0, j0 + unroll)))
    o_ref[0] = o_sc[...].astype(o_ref.dtype)

    @pl.when(h == D_HEADS - 1)
    def _():
        lse_ref[0] = lse_sc[...]


def dilated_group(qkv, g, dil, *, tiles, unroll, mxu_sums):
    B, S, C = qkv.shape
    assert (dil * tiles) % unroll == 0
    span = D_SPAN * dil
    tb = span * tiles
    hd = D_HEAD_DIM
    col = lambda t: (g * 3 + t) * D_HEADS
    cur = lambda t: pl.BlockSpec((1, tb, hd), lambda b, n, h: (b, n, col(t) + h))
    prev = lambda t: pl.BlockSpec((1, span, hd), lambda b, n, h: (b, jnp.maximum(n * tiles - 1, 0), col(t) + h))
    if dil == 1:
        kern = functools.partial(_dense_window_kernel, tiles=tiles, unroll=unroll, mxu_sums=mxu_sums)
        in_specs = [cur(0), prev(1), cur(1), prev(2), cur(2)]
        scratch = []
    else:
        kern = functools.partial(_strided_window_kernel, dil=dil, tiles=tiles, unroll=unroll, mxu_sums=mxu_sums)
        in_specs = [cur(0), cur(1), cur(2)]
        scratch = [pltpu.VMEM((tb, hd), F32)] * 4 + [pltpu.VMEM((D_HEADS, span, hd), BF16)] * 2
    o, lse = pl.pallas_call(
        kern,
        grid=(B, S // tb, D_HEADS),
        in_specs=in_specs,
        out_specs=[
            pl.BlockSpec((1, tb, hd), lambda b, n, h: (b, n, h)),
            pl.BlockSpec((1, tb, LANES), lambda b, n, h: (b, n, 0)),
        ],
        out_shape=[
            jax.ShapeDtypeStruct((B, S, D_HEADS * hd), BF16),
            jax.ShapeDtypeStruct((B, S, LANES), F32),
        ],
        scratch_shapes=[pltpu.VMEM((tb, LANES), F32)] + scratch,
        compiler_params=_params("parallel", "arbitrary", "arbitrary"),
        name="dilated_attention",
    )(*([qkv] * len(in_specs)))
    return o.reshape(B * S, D_HEADS * hd), lse.reshape(B * S, LANES)


def _combine_kernel(o0_ref, o1_ref, o2_ref, l0_ref, l1_ref, l2_ref, o_ref):
    l0, l1, l2 = l0_ref[...], l1_ref[...], l2_ref[...]
    mx = jnp.maximum(jnp.maximum(l0, l1), l2)
    e0, e1, e2 = jnp.exp(l0 - mx), jnp.exp(l1 - mx), jnp.exp(l2 - mx)
    inv = 1.0 / (e0 + e1 + e2)
    ws = (e0 * inv, e1 * inv, e2 * inv)
    for h in range(D_HEADS):
        hs = slice(h * D_HEAD_DIM, (h + 1) * D_HEAD_DIM)
        acc = ws[0][:, h:h + 1] * o0_ref[:, hs].astype(F32)
        acc = acc + ws[1][:, h:h + 1] * o1_ref[:, hs].astype(F32)
        acc = acc + ws[2][:, h:h + 1] * o2_ref[:, hs].astype(F32)
        o_ref[:, hs] = acc.astype(o_ref.dtype)


def combine_groups(outs, lses, *, tm=512):
    T, W = outs[0].shape
    tm = min(tm, T)
    wide = pl.BlockSpec((tm, W), lambda i: (i, 0))
    narrow = pl.BlockSpec((tm, LANES), lambda i: (i, 0))
    return pl.pallas_call(
        _combine_kernel,
        grid=(T // tm,),
        in_specs=[wide, wide, wide, narrow, narrow, narrow],
        out_specs=wide,
        out_shape=jax.ShapeDtypeStruct((T, W), BF16),
        compiler_params=_params("parallel"),
        name="combine_groups",
    )(*outs, *lses)


def _deinterleave_heads(w, heads, dim):
    k = w.shape[0]
    return w.reshape(k, heads, dim // 2, 2).transpose(0, 1, 3, 2).reshape(k, heads * dim)


def mixer_a(xr, B, S, nw, a_w_in, a_q_norm_w, a_k_norm_w, lq1, lk1, lq2, lk2, a_subln_w, a_w_out, *, layer_idx):
    T, D = xr.shape
    colw = jnp.concatenate([
        jnp.tile(a_q_norm_w.astype(F32) * (A_HEAD_DIM ** -0.5 * math.log2(math.e)), 2 * A_HEADS),
        jnp.tile(a_k_norm_w.astype(F32), 2 * A_HEADS)]).reshape(1, 2 * D)
    qk = norm_matmul(xr, nw, a_w_in[:, :2 * D].astype(BF16), colw)
    v_t = norm_matmul_t(xr, nw, a_w_in[:, 2 * D:].T.astype(BF16))
    o = diff_attention(qk.reshape(B, S, 2 * D), v_t, lq1, lk1, lq2, lk2, a_subln_w, layer_idx=layer_idx)
    return matmul_residual(o.reshape(T, D), a_w_out.astype(BF16), xr)


def mixer_b(xr, B, S, nw, b_w_in, b_gn_w, b_w_out):
    T, D = xr.shape
    w_in = jnp.concatenate([
        _deinterleave_heads(b_w_in[:, :D], R_HEADS, R_KDIM),
        _deinterleave_heads(b_w_in[:, D:2 * D], R_HEADS, R_KDIM),
        b_w_in[:, 2 * D:]], axis=1)
    proj = norm_matmul(xr, nw, w_in.astype(BF16))
    o = retention(proj.reshape(B, S, -1), b_gn_w)
    return matmul_residual(o.reshape(T, -1), b_w_out.astype(BF16), xr)


def mixer_c(xr, B, S, nw, c_w_in, c_conv_w, c_conv_b, c_dt_bias, c_a_log, c_d_skip, c_norm_w, c_w_out):
    T, D = xr.shape
    n_main = 2 * M_D_INNER + 2 * M_GROUPS * M_DSTATE
    zx = norm_matmul(xr, nw, c_w_in[:, :n_main].astype(BF16))
    w_dt = jnp.pad(c_w_in[:, n_main:], ((0, 0), (0, LANES - M_HEADS)))
    dt_raw = norm_matmul(xr, nw, w_dt.astype(BF16), tn=LANES, out_dtype=F32)
    y = mamba_ssd(zx.reshape(B, S, n_main), dt_raw.reshape(B, S, LANES), c_conv_w, c_conv_b, c_dt_bias,
                  c_a_log, c_d_skip, c_norm_w)
    return matmul_residual(y.reshape(T, M_D_INNER), c_w_out.astype(BF16), xr)


def mixer_d(xr, B, S, nw, d_w_in, d_q_norm_w, d_k_norm_w, d_w_out):
    T, D = xr.shape
    scale = D_HEAD_DIM ** -0.5
    colw = jnp.concatenate([
        jnp.concatenate([jnp.tile(d_q_norm_w[g].astype(F32) * scale, D_HEADS),
                         jnp.tile(d_k_norm_w[g].astype(F32), D_HEADS),
                         jnp.ones((D,), F32)])
        for g in range(len(D_PATTERNS))]).reshape(1, -1)
    qkv = norm_matmul(xr, nw, d_w_in.astype(BF16), colw).reshape(B, S, -1)
    outs, lses = zip(*[dilated_group(qkv, g, dil, tiles=max(1, 8 // dil), unroll=8 if dil < 16 else 4,
                                     mxu_sums=dil < 16)
                       for g, (_, dil) in enumerate(D_PATTERNS)])
    o = combine_groups(outs, lses)
    return matmul_residual(o, d_w_out.astype(BF16), xr)


@jax.jit
def kernel(x, norm1_w, norm2_w, mlp_w1, mlp_w2, a_w_in, a_q_norm_w, a_k_norm_w, a_lambda_q1, a_lambda_k1, a_lambda_q2, a_lambda_k2, a_subln_w, a_w_out, b_w_in, b_gn_w, b_w_out, c_w_in, c_conv_w, c_conv_b, c_dt_bias, c_a_log, c_d_skip, c_norm_w, c_w_out, d_w_in, d_q_norm_w, d_k_norm_w, d_w_out):
    B, S, D = x.shape
    xr = x.reshape(B * S, D)
    ffn = lambda t, i: mlp(t, norm2_w[i], mlp_w1[i].astype(BF16), mlp_w2[i].astype(BF16))
    xr = mixer_a(xr, B, S, norm1_w[0], a_w_in, a_q_norm_w, a_k_norm_w, a_lambda_q1, a_lambda_k1,
                 a_lambda_q2, a_lambda_k2, a_subln_w, a_w_out, layer_idx=0)
    xr = ffn(xr, 0)
    xr = mixer_b(xr, B, S, norm1_w[1], b_w_in, b_gn_w, b_w_out)
    xr = ffn(xr, 1)
    xr = mixer_c(xr, B, S, norm1_w[2], c_w_in, c_conv_w, c_conv_b, c_dt_bias, c_a_log, c_d_skip, c_norm_w,
                 c_w_out)
    xr = ffn(xr, 2)
    xr = mixer_d(xr, B, S, norm1_w[3], d_w_in, d_q_norm_w, d_k_norm_w, d_w_out)
    xr = ffn(xr, 3)
    return xr.reshape(B, S, D)
```

```python
import functools
import math

import jax
import jax.numpy as jnp
from jax import lax
from jax.experimental import pallas as pl
from jax.experimental.pallas import tpu as pltpu

F32 = jnp.float32
BF16 = jnp.bfloat16

EPS = 1e-6
D_MODEL = 2048
D_FF = 4 * D_MODEL
LANES = 128
MXU_COLS = 256
ROW_BLOCK = 256

A_HEAD_DIM = 128
A_HEADS = 8
R_HEADS = 8
R_KDIM = 256
R_VDIM = 512
M_D_INNER = 4096
M_HEADDIM = 64
M_HEADS = 64
M_GROUPS = 8
M_DSTATE = 128
M_CONV = 4
M_GROUP_W = M_D_INNER // M_GROUPS
M_HPG = M_HEADS // M_GROUPS
D_HEADS = 16
D_HEAD_DIM = 128
D_PATTERNS = ((128, 1), (512, 4), (2048, 16))
D_SPAN = 128

VMEM_LIMIT_BYTES = 56 * 1024 * 1024
NEG_BIG = -1e30

NT_DIMS = (((1,), (1,)), ((), ()))
TN_DIMS = (((0,), (0,)), ((), ()))


def _params(*sem):
    return pltpu.CompilerParams(dimension_semantics=sem, vmem_limit_bytes=VMEM_LIMIT_BYTES)


def _silu(v):
    return v * (1.0 / (1.0 + jnp.exp(-v)))


def _rms_rows(v, w):
    ms = jnp.mean(v * v, axis=-1, keepdims=True)
    return v * lax.rsqrt(ms + EPS) * w


def _split3(v):
    p1 = v.astype(BF16)
    r1 = v - p1.astype(F32)
    p2 = r1.astype(BF16)
    return p1, p2, (r1 - p2.astype(F32)).astype(BF16)


def _row_blocks(tm):
    rb = min(tm, ROW_BLOCK)
    return [slice(r, r + rb) for r in range(0, tm, rb)]


def _norm_matmul_kernel(x_ref, nw_ref, w_ref, cw_ref, o_ref, h_sc, *, tm, tn, head_norm, period, count):
    j = pl.program_id(1)
    every = slice(0, tm)

    def plain(rows):
        o_ref[rows, :] = jnp.dot(h_sc[rows, :], w_ref[...], preferred_element_type=F32).astype(o_ref.dtype)

    def normed(rows):
        sub = min(tn, MXU_COLS)
        for c in range(tn // sub):
            acc = jnp.dot(h_sc[rows, :], w_ref[:, c * sub:(c + 1) * sub], preferred_element_type=F32)
            for d in range(sub // LANES):
                sl = slice(c * sub + d * LANES, c * sub + (d + 1) * LANES)
                o_ref[rows, sl] = _rms_rows(acc[:, d * LANES:(d + 1) * LANES], cw_ref[:, sl]).astype(o_ref.dtype)

    @pl.when(j == 0)
    def _():
        for rows in _row_blocks(tm):
            h_sc[rows, :] = _rms_rows(x_ref[rows, :], nw_ref[...]).astype(BF16)
            (normed if head_norm else plain)(rows)

    if not head_norm:
        pl.when(j > 0)(functools.partial(plain, every))
        return
    is_norm = (j % period) < count
    pl.when(jnp.logical_and(j > 0, is_norm))(functools.partial(normed, every))
    pl.when(jnp.logical_not(is_norm))(functools.partial(plain, every))


def norm_matmul(x, nw, w, colw=None, *, tm=1024, tn=1024, out_dtype=BF16):
    T, K = x.shape
    N = w.shape[1]
    tm = min(tm, T)
    tn = min(tn, N)
    head_norm = colw is not None
    if colw is None:
        colw = jnp.ones((1, N), F32)
    period = 3 * D_MODEL // tn
    count = 2 * D_MODEL // tn
    kern = functools.partial(_norm_matmul_kernel, tm=tm, tn=tn, head_norm=head_norm, period=period, count=count)
    return pl.pallas_call(
        kern,
        grid=(T // tm, N // tn),
        in_specs=[
            pl.BlockSpec((tm, K), lambda i, j: (i, 0)),
            pl.BlockSpec((1, K), lambda i, j: (0, 0)),
            pl.BlockSpec((K, tn), lambda i, j: (0, j)),
            pl.BlockSpec((1, tn), lambda i, j: (0, j)),
        ],
        out_specs=pl.BlockSpec((tm, tn), lambda i, j: (i, j)),
        out_shape=jax.ShapeDtypeStruct((T, N), out_dtype),
        scratch_shapes=[pltpu.VMEM((tm, K), BF16)],
        compiler_params=_params("parallel", "arbitrary"),
        name="norm_matmul",
    )(x, nw.reshape(1, K), w, colw)


def _norm_matmul_t_kernel(x_ref, nw_ref, wt_ref, o_ref, h_sc, *, tm):
    j = pl.program_id(1)

    def emit(rows):
        o_ref[:, rows] = lax.dot_general(wt_ref[...], h_sc[rows, :], NT_DIMS,
                                         preferred_element_type=F32).astype(o_ref.dtype)

    @pl.when(j == 0)
    def _():
        for rows in _row_blocks(tm):
            h_sc[rows, :] = _rms_rows(x_ref[rows, :], nw_ref[...]).astype(BF16)
            emit(rows)

    pl.when(j > 0)(functools.partial(emit, slice(0, tm)))


def norm_matmul_t(x, nw, w_t, *, tm=1024, tn=1024):
    T, K = x.shape
    N = w_t.shape[0]
    tm = min(tm, T)
    return pl.pallas_call(
        functools.partial(_norm_matmul_t_kernel, tm=tm),
        grid=(T // tm, N // tn),
        in_specs=[
            pl.BlockSpec((tm, K), lambda i, j: (i, 0)),
            pl.BlockSpec((1, K), lambda i, j: (0, 0)),
            pl.BlockSpec((tn, K), lambda i, j: (j, 0)),
        ],
        out_specs=pl.BlockSpec((tn, tm), lambda i, j: (j, i)),
        out_shape=jax.ShapeDtypeStruct((N, T), BF16),
        scratch_shapes=[pltpu.VMEM((tm, K), BF16)],
        compiler_params=_params("parallel", "arbitrary"),
        name="norm_matmul_t",
    )(x, nw.reshape(1, K), w_t)


def _matmul_residual_kernel(a_ref, w_ref, r_ref, o_ref):
    o_ref[...] = r_ref[...] + jnp.dot(a_ref[...], w_ref[...], preferred_element_type=F32)


def matmul_residual(a, w, res, *, tm=1024, tn=1024):
    T, K = a.shape
    N = w.shape[1]
    tm = min(tm, T)
    return pl.pallas_call(
        _matmul_residual_kernel,
        grid=(T // tm, N // tn),
        in_specs=[
            pl.BlockSpec((tm, K), lambda i, j: (i, 0)),
            pl.BlockSpec((K, tn), lambda i, j: (0, j)),
            pl.BlockSpec((tm, tn), lambda i, j: (i, j)),
        ],
        out_specs=pl.BlockSpec((tm, tn), lambda i, j: (i, j)),
        out_shape=jax.ShapeDtypeStruct((T, N), F32),
        compiler_params=_params("parallel", "arbitrary"),
        name="matmul_residual",
    )(a, w, res)


def _mlp_kernel(x_ref, nw_ref, w1_ref, w2_ref, o_ref, h_sc, *, tm):
    f = pl.program_id(1)

    def accumulate(rows):
        u = jnp.dot(h_sc[rows, :], w1_ref[...], preferred_element_type=F32)
        u = jnp.square(jnp.maximum(u, 0.0)).astype(BF16)
        o_ref[rows, :] += jnp.dot(u, w2_ref[...], preferred_element_type=F32)

    @pl.when(f == 0)
    def _():
        for rows in _row_blocks(tm):
            x = x_ref[rows, :]
            h_sc[rows, :] = _rms_rows(x, nw_ref[...]).astype(BF16)
            o_ref[rows, :] = x
            accumulate(rows)

    pl.when(f > 0)(functools.partial(accumulate, slice(0, tm)))


def mlp(x, nw, w1, w2, *, tm=512, tf=1024):
    T, D = x.shape
    FF = w1.shape[1]
    tm = min(tm, T)
    return pl.pallas_call(
        functools.partial(_mlp_kernel, tm=tm),
        grid=(T // tm, FF // tf),
        in_specs=[
            pl.BlockSpec((tm, D), lambda i, f: (i, 0)),
            pl.BlockSpec((1, D), lambda i, f: (0, 0)),
            pl.BlockSpec((D, tf), lambda i, f: (0, f)),
            pl.BlockSpec((tf, D), lambda i, f: (f, 0)),
        ],
        out_specs=pl.BlockSpec((tm, D), lambda i, f: (i, 0)),
        out_shape=jax.ShapeDtypeStruct((T, D), F32),
        scratch_shapes=[pltpu.VMEM((tm, D), BF16)],
        compiler_params=_params("parallel", "arbitrary"),
        name="mlp",
    )(x, nw.reshape(1, D), w1, w2)


def _diff_attn_kernel(q_ref, k_ref, vt_ref, lq1_ref, lk1_ref, lq2_ref, lk2_ref, sw_ref, o_ref,
                      m_sc, l_sc, acc_sc, sa_sc, sb_sc, *, tq, lambda_init):
    qi = pl.program_id(2)
    m_sc[...] = jnp.full_like(m_sc, NEG_BIG)
    l_sc[...] = jnp.zeros_like(l_sc)
    acc_sc[...] = jnp.zeros_like(acc_sc)
    key = lax.broadcasted_iota(jnp.int32, (tq, tq), 0)
    qry = lax.broadcasted_iota(jnp.int32, (tq, tq), 1)

    def scores(m, start):
        hs = slice(m * A_HEAD_DIM, (m + 1) * A_HEAD_DIM)
        return lax.dot_general(k_ref[0, pl.ds(start, tq), hs], q_ref[0, :, hs], NT_DIMS,
                               preferred_element_type=F32)

    def update(m, s, start, masked):
        if masked:
            s = jnp.where(key <= qry, s, NEG_BIG)
        m_prev = m_sc[m]
        m_new = jnp.maximum(m_prev, jnp.max(s, axis=0, keepdims=True))
        alpha = jnp.exp2(m_prev - m_new)
        p = jnp.exp2(s - m_new)
        m_sc[m] = m_new
        l_sc[m] = alpha * l_sc[m] + jnp.sum(p, axis=0, keepdims=True)
        acc_sc[m] = alpha * acc_sc[m] + jnp.dot(vt_ref[:, pl.ds(start, tq)], p.astype(BF16),
                                                preferred_element_type=F32)

    def put_scores(s_sc, j):
        start = pl.multiple_of(j * tq, tq)
        s_sc[0] = scores(0, start)
        s_sc[1] = scores(1, start)

    def updates(s_sc, j, masked):
        start = pl.multiple_of(j * tq, tq)
        update(0, s_sc[0], start, masked)
        update(1, s_sc[1], start, masked)

    put_scores(sa_sc, 0)

    def body(jj, carry):
        j = 2 * jj
        put_scores(sb_sc, j + 1)
        updates(sa_sc, j, False)
        put_scores(sa_sc, j + 2)
        updates(sb_sc, j + 1, False)
        return carry

    lax.fori_loop(0, qi // 2, body, 0)

    @pl.when(qi % 2 == 0)
    def _():
        updates(sa_sc, qi, True)

    @pl.when(qi % 2 == 1)
    def _():
        put_scores(sb_sc, qi)
        updates(sa_sc, qi - 1, False)
        updates(sb_sc, qi, True)

    lam = (jnp.exp(jnp.sum(lq1_ref[...] * lk1_ref[...], axis=-1, keepdims=True))
           - jnp.exp(jnp.sum(lq2_ref[...] * lk2_ref[...], axis=-1, keepdims=True)) + lambda_init)
    o_t = acc_sc[0] * (1.0 / l_sc[0]) - lam * (acc_sc[1] * (1.0 / l_sc[1]))
    o_ref[0] = (_rms_rows(o_t.T, sw_ref[...]) * (1.0 - lambda_init)).astype(o_ref.dtype)


def diff_attention(qk, v_t, lq1, lk1, lq2, lk2, subln_w, *, layer_idx, tq=512):
    B, S, _ = qk.shape
    tq = min(tq, S)
    lambda_init = 0.8 - 0.6 * math.exp(-0.3 * layer_idx)
    pw = 2 * A_HEAD_DIM
    vec = lambda a: a.reshape(1, -1).astype(F32)
    small = lambda n: pl.BlockSpec((1, n), lambda b, h, i: (0, 0))
    kern = functools.partial(_diff_attn_kernel, tq=tq, lambda_init=lambda_init)
    return pl.pallas_call(
        kern,
        grid=(B, A_HEADS, S // tq),
        in_specs=[
            pl.BlockSpec((1, tq, pw), lambda b, h, i: (b, i, h)),
            pl.BlockSpec((1, S, pw), lambda b, h, i: (b, 0, A_HEADS + h)),
            pl.BlockSpec((pw, S), lambda b, h, i: (h, b)),
            small(A_HEAD_DIM), small(A_HEAD_DIM), small(A_HEAD_DIM), small(A_HEAD_DIM), small(pw),
        ],
        out_specs=pl.BlockSpec((1, tq, pw), lambda b, h, i: (b, i, h)),
        out_shape=jax.ShapeDtypeStruct((B, S, D_MODEL), BF16),
        scratch_shapes=[
            pltpu.VMEM((2, 1, tq), F32),
            pltpu.VMEM((2, 1, tq), F32),
            pltpu.VMEM((2, pw, tq), F32),
            pltpu.VMEM((2, tq, tq), F32),
            pltpu.VMEM((2, tq, tq), F32),
        ],
        compiler_params=_params("parallel", "parallel", "arbitrary"),
        name="diff_attention",
    )(qk, qk, v_t, vec(lq1), vec(lk1), vec(lq2), vec(lk2), vec(subln_w))


def _retention_kernel(q_ref, k_ref, v_ref, g_ref, cos_ref, sin_ref, dm_ref, xi_ref, zeta_ref, cd_ref,
                      gw_ref, o_ref, r_sc, *, tr, chunk):
    @pl.when(pl.program_id(2) == 0)
    def _():
        r_sc[...] = jnp.zeros_like(r_sc)

    half = R_KDIM // 2
    dmask = dm_ref[0]
    xi = xi_ref[0]
    zeta = zeta_ref[0]
    cdecay = cd_ref[0, 0:1, 0:1]
    gw = gw_ref[0]

    def rotate(t, c, s):
        te, to = t[:, :half], t[:, half:]
        return te * c - to * s, to * c + te * s

    for ci in range(tr // chunk):
        rows = slice(ci * chunk, (ci + 1) * chunk)
        c = cos_ref[rows, :]
        s = sin_ref[rows, :]
        qe, qo = rotate(q_ref[0, rows, :].astype(F32), c, s)
        ke, ko = rotate(k_ref[0, rows, :].astype(F32) * (R_KDIM ** -0.5), c, s)
        v = v_ref[0, rows, :]
        q_r = jnp.concatenate([qe, qo], axis=1).astype(BF16)
        k_r = jnp.concatenate([ke, ko], axis=1).astype(BF16)
        q_x = jnp.concatenate([qe * xi, qo * xi], axis=1).astype(BF16)
        k_z = jnp.concatenate([ke * zeta, ko * zeta], axis=1).astype(BF16)
        r_old = r_sc[...]
        sc = lax.dot_general(q_r, k_r, NT_DIMS, preferred_element_type=F32) * dmask
        o = (jnp.dot(sc.astype(BF16), v, preferred_element_type=F32)
             + jnp.dot(q_x, r_old.astype(BF16), preferred_element_type=F32))
        r_sc[...] = r_old * cdecay + lax.dot_general(k_z, v, TN_DIMS, preferred_element_type=F32)
        gate = _silu(g_ref[0, rows, :].astype(F32))
        o_ref[0, rows, :] = (gate * _rms_rows(o, gw)).astype(o_ref.dtype)


def retention(proj, gn_w, *, tr=1024, chunk=256):
    B, S, _ = proj.shape
    tr = min(tr, S)
    chunk = min(chunk, tr)
    half = R_KDIM // 2
    angle = 1.0 / (10000.0 ** jnp.linspace(0.0, 1.0, half, dtype=F32))
    ang = jnp.arange(S, dtype=F32)[:, None] * angle[None, :]
    cos, sin = jnp.cos(ang), jnp.sin(ang)
    log_g = jnp.log(1.0 - 2.0 ** (-5.0 - jnp.arange(R_HEADS, dtype=F32)))
    idx = jnp.arange(chunk, dtype=F32)
    rel = idx[:, None] - idx[None, :]
    dmask = jnp.where(rel[None] >= 0, jnp.exp(jnp.maximum(rel, 0.0)[None] * log_g[:, None, None]), 0.0)
    xi = jnp.exp((idx + 1.0)[None, :] * log_g[:, None])
    zeta = jnp.exp((chunk - 1.0 - idx)[None, :] * log_g[:, None])
    cdecay = jnp.exp(chunk * log_g)
    bc = lambda t: jnp.broadcast_to(t[:, :, None], (R_HEADS, chunk, half))
    cd = jnp.broadcast_to(cdecay[:, None, None], (R_HEADS, 8, LANES))
    nq = D_MODEL // R_KDIM
    nv = 2 * D_MODEL // R_VDIM
    kern = functools.partial(_retention_kernel, tr=tr, chunk=chunk)
    head = lambda shape: pl.BlockSpec(shape, lambda b, h, i: (h, 0, 0))
    return pl.pallas_call(
        kern,
        grid=(B, R_HEADS, S // tr),
        in_specs=[
            pl.BlockSpec((1, tr, R_KDIM), lambda b, h, i: (b, i, h)),
            pl.BlockSpec((1, tr, R_KDIM), lambda b, h, i: (b, i, nq + h)),
            pl.BlockSpec((1, tr, R_VDIM), lambda b, h, i: (b, i, nv + h)),
            pl.BlockSpec((1, tr, R_VDIM), lambda b, h, i: (b, i, nv + R_HEADS + h)),
            pl.BlockSpec((tr, half), lambda b, h, i: (i, 0)),
            pl.BlockSpec((tr, half), lambda b, h, i: (i, 0)),
            head((1, chunk, chunk)), head((1, chunk, half)), head((1, chunk, half)),
            head((1, 8, LANES)), head((1, 1, R_VDIM)),
        ],
        out_specs=pl.BlockSpec((1, tr, R_VDIM), lambda b, h, i: (b, i, h)),
        out_shape=jax.ShapeDtypeStruct((B, S, R_HEADS * R_VDIM), BF16),
        scratch_shapes=[pltpu.VMEM((R_KDIM, R_VDIM), F32)],
        compiler_params=_params("parallel", "parallel", "arbitrary"),
        name="retention",
    )(proj, proj, proj, proj, cos, sin, dmask, bc(xi), bc(zeta), cd, gn_w.reshape(R_HEADS, 1, R_VDIM).astype(F32))


def _mamba_kernel(z_ref, x_ref, b_ref, c_ref, dt_ref, wx_ref, wb_ref, wc_ref, bx_ref, bb_ref, bc_ref,
                  dtb_ref, alog_ref, dsk_ref, nw_ref, o_ref,
                  st_sc, tx_sc, tb_sc, tc_sc, *, chunk):
    L = chunk

    @pl.when(pl.program_id(2) == 0)
    def _():
        st_sc[...] = jnp.zeros_like(st_sc)
        for ext_sc in (tx_sc, tb_sc, tc_sc):
            ext_sc[0:8, :] = jnp.zeros((8, ext_sc.shape[1]), F32)

    def conv_silu(cur_ref, ext_sc, w_ref, bias_ref):
        ext_sc[8:, :] = cur_ref[0].astype(F32)
        w = w_ref[...]
        acc = bias_ref[...] + ext_sc[8:, :] * w[M_CONV - 1:M_CONV, :]
        for kk in range(M_CONV - 1):
            sh = M_CONV - 1 - kk
            acc = acc + ext_sc[8 - sh:8 - sh + L, :] * w[kk:kk + 1, :]
        ext_sc[0:8, :] = ext_sc[L:, :]
        return _silu(acc)

    raw = dt_ref[0, 0] + dtb_ref[0]
    dt_t = jnp.maximum(raw, 0.0) + jnp.log(1.0 + jnp.exp(-jnp.abs(raw)))
    a_t = dt_t * (-jnp.exp(alog_ref[0])) * math.log2(math.e)

    ri = lax.broadcasted_iota(jnp.int32, (L, L), 0)
    ci = lax.broadcasted_iota(jnp.int32, (L, L), 1)
    tril = ci <= ri
    eye = jnp.where(ri == ci, 1.0, 0.0).astype(BF16)
    lower = jnp.where(tril, 1.0, 0.0).astype(BF16)
    upper = jnp.where(ri <= ci, 1.0, 0.0).astype(BF16)

    def per_column(t):
        return jnp.concatenate([jnp.broadcast_to(t[r:r + 1, :], (M_HEADDIM, L)) for r in range(M_HPG)], axis=0)

    a_parts = _split3(a_t)
    acs_row = sum(jnp.dot(p, upper, preferred_element_type=F32) for p in a_parts)

    def expand_dot(mat01, parts):
        return sum(lax.dot_general(mat01, per_column(p.astype(F32)).astype(BF16), NT_DIMS,
                                   preferred_element_type=F32) for p in parts)

    acs_x = expand_dot(lower, a_parts)
    dt_x = expand_dot(eye, _split3(dt_t))

    xs = conv_silu(x_ref, tx_sc, wx_ref, bx_ref)
    bm = conv_silu(b_ref, tb_sc, wb_ref, bb_ref)
    cm = conv_silu(c_ref, tc_sc, wc_ref, bc_ref)

    xdt = xs * dt_x
    cb = lax.dot_general(cm.astype(BF16), bm.astype(BF16), NT_DIMS, preferred_element_type=F32)
    xdt_b = xdt.astype(BF16)
    parts = []
    for r in range(M_HPG):
        diff = acs_x[:, r * M_HEADDIM:r * M_HEADDIM + 1] - acs_row[r:r + 1, :]
        lmat = jnp.exp2(jnp.where(tril, diff, NEG_BIG))
        mr = (cb * lmat).astype(BF16)
        parts.append(jnp.dot(mr, xdt_b[:, r * M_HEADDIM:(r + 1) * M_HEADDIM], preferred_element_type=F32))
    y = jnp.concatenate(parts, axis=1)

    state = st_sc[...]
    y = y + jnp.exp2(acs_x) * jnp.dot(cm.astype(BF16), state.astype(BF16), preferred_element_type=F32)
    last = acs_x[L - 1:L, :]
    decay_end = jnp.exp2(last - acs_x)
    st_sc[...] = state * jnp.exp2(last) + lax.dot_general(
        bm.astype(BF16), (xdt * decay_end).astype(BF16), TN_DIMS, preferred_element_type=F32)

    y = y + xs * dsk_ref[...]
    y = y * _silu(z_ref[0].astype(F32))
    o_ref[0] = _rms_rows(y, nw_ref[...]).astype(o_ref.dtype)


def mamba_ssd(zx, dt_raw, conv_w, conv_b, dt_bias, a_log, d_skip, norm_w, *, chunk=256):
    B, S, _ = zx.shape
    chunk = min(chunk, S)
    G, W, N = M_GROUPS, M_GROUP_W, M_DSTATE
    dt_t = dt_raw[:, :, :M_HEADS].reshape(B, S, G, M_HPG).transpose(0, 2, 3, 1)
    xoff = M_D_INNER // W
    boff = 2 * M_D_INNER // N
    coff = boff + G
    cwb = M_D_INNER // N
    conv_w = conv_w.astype(F32)
    conv_b = conv_b.reshape(1, -1).astype(F32)
    per_head = lambda t: t.reshape(G, M_HPG, 1).astype(F32)
    dsk_x = jnp.repeat(d_skip.astype(F32), M_HEADDIM).reshape(1, M_D_INNER)
    kern = functools.partial(_mamba_kernel, chunk=chunk)
    return pl.pallas_call(
        kern,
        grid=(B, G, S // chunk),
        in_specs=[
            pl.BlockSpec((1, chunk, W), lambda b, g, c: (b, c, g)),
            pl.BlockSpec((1, chunk, W), lambda b, g, c: (b, c, xoff + g)),
            pl.BlockSpec((1, chunk, N), lambda b, g, c: (b, c, boff + g)),
            pl.BlockSpec((1, chunk, N), lambda b, g, c: (b, c, coff + g)),
            pl.BlockSpec((1, 1, M_HPG, chunk), lambda b, g, c: (b, g, 0, c)),
            pl.BlockSpec((M_CONV, W), lambda b, g, c: (0, g)),
            pl.BlockSpec((M_CONV, N), lambda b, g, c: (0, cwb + g)),
            pl.BlockSpec((M_CONV, N), lambda b, g, c: (0, cwb + G + g)),
            pl.BlockSpec((1, W), lambda b, g, c: (0, g)),
            pl.BlockSpec((1, N), lambda b, g, c: (0, cwb + g)),
            pl.BlockSpec((1, N), lambda b, g, c: (0, cwb + G + g)),
            pl.BlockSpec((1, M_HPG, 1), lambda b, g, c: (g, 0, 0)),
            pl.BlockSpec((1, M_HPG, 1), lambda b, g, c: (g, 0, 0)),
            pl.BlockSpec((1, W), lambda b, g, c: (0, g)),
            pl.BlockSpec((1, W), lambda b, g, c: (0, g)),
        ],
        out_specs=pl.BlockSpec((1, chunk, W), lambda b, g, c: (b, c, g)),
        out_shape=jax.ShapeDtypeStruct((B, S, M_D_INNER), BF16),
        scratch_shapes=[
            pltpu.VMEM((N, W), F32),
            pltpu.VMEM((8 + chunk, W), F32),
            pltpu.VMEM((8 + chunk, N), F32),
            pltpu.VMEM((8 + chunk, N), F32),
        ],
        compiler_params=_params("parallel", "parallel", "arbitrary"),
        name="mamba_ssd",
    )(zx, zx, zx, zx, dt_t, conv_w, conv_w, conv_w, conv_b, conv_b, conv_b,
      per_head(dt_bias), per_head(a_log), dsk_x, norm_w.reshape(1, M_D_INNER).astype(F32))


def _window_attend(operands, mxu_sums):
    T = D_SPAN
    ri = lax.broadcasted_iota(jnp.int32, (T, T), 0)
    ci = lax.broadcasted_iota(jnp.int32, (T, T), 1)
    ones = jnp.ones((T, LANES), BF16)
    scores = [(lax.dot_general(q, kp, NT_DIMS, preferred_element_type=F32),
               lax.dot_general(q, kc, NT_DIMS, preferred_element_type=F32))
              for q, kp, kc, _, _, _ in operands]
    probs = []
    for (sp, sc), (_, _, _, _, _, has_prev) in zip(scores, operands):
        sp = jnp.where((ci >= ri) & has_prev, sp, NEG_BIG)
        sc = jnp.where(ci <= ri, sc, NEG_BIG)
        mx = jnp.maximum(jnp.max(sp, axis=-1, keepdims=True), jnp.max(sc, axis=-1, keepdims=True))
        pp = jnp.exp(sp - mx)
        pc = jnp.exp(sc - mx)
        l = None if mxu_sums else jnp.sum(pp, axis=-1, keepdims=True) + jnp.sum(pc, axis=-1, keepdims=True)
        probs.append((pp.astype(BF16), pc.astype(BF16), mx, l))
    outs = []
    for (pp, pc, mx, l), (_, _, _, vp, vc, _) in zip(probs, operands):
        o = jnp.dot(pp, vp, preferred_element_type=F32) + jnp.dot(pc, vc, preferred_element_type=F32)
        if mxu_sums:
            l = jnp.dot(pp, ones, preferred_element_type=F32) + jnp.dot(pc, ones, preferred_element_type=F32)
        outs.append((o * (1.0 / l), mx + jnp.log(l)))
    return outs


def _dense_window_kernel(q_ref, kp_ref, kc_ref, vp_ref, vc_ref, o_ref, lse_ref, lse_sc, *, tiles, unroll, mxu_sums):
    n = pl.program_id(1)
    h = pl.program_id(2)
    T = D_SPAN

    @pl.when(h == 0)
    def _():
        lse_sc[...] = jnp.zeros_like(lse_sc)

    lane = lax.broadcasted_iota(jnp.int32, (T, LANES), 1)
    for i0 in range(0, tiles, unroll):
        operands = []
        for i in range(i0, i0 + unroll):
            cur = slice(i * T, (i + 1) * T)
            prev = slice((i - 1) * T, i * T)
            kp = kp_ref[0] if i == 0 else kc_ref[0, prev, :]
            vp = vp_ref[0] if i == 0 else vc_ref[0, prev, :]
            operands.append((q_ref[0, cur, :], kp, kc_ref[0, cur, :], vp, vc_ref[0, cur, :],
                             jnp.logical_or(n > 0, i > 0)))
        for i, (o, lse) in zip(range(i0, i0 + unroll), _window_attend(operands, mxu_sums)):
            cur = slice(i * T, (i + 1) * T)
            o_ref[0, cur, :] = o.astype(o_ref.dtype)
            lse_sc[cur, :] = jnp.where(lane == h, lse, lse_sc[cur, :])

    @pl.when(h == D_HEADS - 1)
    def _():
        lse_ref[0] = lse_sc[...]


def _strided_window_kernel(q_ref, k_ref, v_ref, o_ref, lse_ref, lse_sc, q_sc, k_sc, v_sc, o_sc, kprev_sc, vprev_sc,
                           *, dil, tiles, unroll, mxu_sums):
    n = pl.program_id(1)
    h = pl.program_id(2)
    T = D_SPAN
    span = T * dil
    q_sc[...] = q_ref[0].astype(F32)
    k_sc[...] = k_ref[0].astype(F32)
    v_sc[...] = v_ref[0].astype(F32)

    @pl.when(h == 0)
    def _():
        lse_sc[...] = jnp.zeros_like(lse_sc)

    @pl.when(n == 0)
    def _():
        kprev_sc[h] = jnp.zeros(kprev_sc.shape[1:], BF16)
        vprev_sc[h] = jnp.zeros(vprev_sc.shape[1:], BF16)

    lane = lax.broadcasted_iota(jnp.int32, (T, LANES), 1)
    strided = lambda ref, start: ref[pl.ds(start, T, stride=dil), :]

    def group(tile_ids):
        operands, places = [], []
        for idx in tile_ids:
            i, r = (0, idx) if tiles == 1 else divmod(idx, dil)
            start = i * span + r
            saved = pl.ds(pl.multiple_of(r * T, T), T)
            kc = strided(k_sc, start).astype(BF16)
            vc = strided(v_sc, start).astype(BF16)
            if i == 0:
                kp, vp = kprev_sc[h, saved, :], vprev_sc[h, saved, :]
            else:
                kp = strided(k_sc, start - span).astype(BF16)
                vp = strided(v_sc, start - span).astype(BF16)
            operands.append((strided(q_sc, start).astype(BF16), kp, kc, vp, vc, jnp.logical_or(n > 0, i > 0)))
            places.append((start, saved if i == tiles - 1 else None))
        for (start, saved), (_, _, kc, _, vc, _), (o, lse) in zip(places, operands,
                                                                   _window_attend(operands, mxu_sums)):
            o_sc[pl.ds(start, T, stride=dil), :] = o
            lse_sc[pl.ds(start, T, stride=dil), :] = jnp.where(lane == h, lse, strided(lse_sc, start))
            if saved is not None:
                kprev_sc[h, saved, :] = kc
                vprev_sc[h, saved, :] = vc

    for j0 in range(0, dil * tiles, unroll):
        group(list(range(j0, j0 + unroll)))
    o_ref[0] = o_sc[...].astype(o_ref.dtype)

    @pl.when(h == D_HEADS - 1)
    def _():
        lse_ref[0] = lse_sc[...]


def dilated_group(qkv, g, dil, *, tiles, unroll, mxu_sums):
    B, S, C = qkv.shape
    assert (dil * tiles) % unroll == 0
    span = D_SPAN * dil
    tb = span * tiles
    hd = D_HEAD_DIM
    col = lambda t: (g * 3 + t) * D_HEADS
    cur = lambda t: pl.BlockSpec((1, tb, hd), lambda b, n, h: (b, n, col(t) + h))
    prev = lambda t: pl.BlockSpec((1, span, hd), lambda b, n, h: (b, jnp.maximum(n * tiles - 1, 0), col(t) + h))
    if dil == 1:
        kern = functools.partial(_dense_window_kernel, tiles=tiles, unroll=unroll, mxu_sums=mxu_sums)
        in_specs = [cur(0), prev(1), cur(1), prev(2), cur(2)]
        scratch = []
    else:
        kern = functools.partial(_strided_window_kernel, dil=dil, tiles=tiles, unroll=unroll, mxu_sums=mxu_sums)
        in_specs = [cur(0), cur(1), cur(2)]
        scratch = [pltpu.VMEM((tb, hd), F32)] * 4 + [pltpu.VMEM((D_HEADS, span, hd), BF16)] * 2
    o, lse = pl.pallas_call(
        kern,
        grid=(B, S // tb, D_HEADS),
        in_specs=in_specs,
        out_specs=[
            pl.BlockSpec((1, tb, hd), lambda b, n, h: (b, n, h)),
            pl.BlockSpec((1, tb, LANES), lambda b, n, h: (b, n, 0)),
        ],
        out_shape=[
            jax.ShapeDtypeStruct((B, S, D_HEADS * hd), BF16),
            jax.ShapeDtypeStruct((B, S, LANES), F32),
        ],
        scratch_shapes=[pltpu.VMEM((tb, LANES), F32)] + scratch,
        compiler_params=_params("parallel", "arbitrary", "arbitrary"),
        name="dilated_attention",
    )(*([qkv] * len(in_specs)))
    return o.reshape(B * S, D_HEADS * hd), lse.reshape(B * S, LANES)


def _combine_kernel(o0_ref, o1_ref, o2_ref, l0_ref, l1_ref, l2_ref, o_ref):
    l0, l1, l2 = l0_ref[...], l1_ref[...], l2_ref[...]
    mx = jnp.maximum(jnp.maximum(l0, l1), l2)
    e0, e1, e2 = jnp.exp(l0 - mx), jnp.exp(l1 - mx), jnp.exp(l2 - mx)
    inv = 1.0 / (e0 + e1 + e2)
    ws = (e0 * inv, e1 * inv, e2 * inv)
    for h in range(D_HEADS):
        hs = slice(h * D_HEAD_DIM, (h + 1) * D_HEAD_DIM)
        acc = ws[0][:, h:h + 1] * o0_ref[:, hs].astype(F32)
        acc = acc + ws[1][:, h:h + 1] * o1_ref[:, hs].astype(F32)
        acc = acc + ws[2][:, h:h + 1] * o2_ref[:, hs].astype(F32)
        o_ref[:, hs] = acc.astype(o_ref.dtype)


def combine_groups(outs, lses, *, tm=512):
    T, W = outs[0].shape
    tm = min(tm, T)
    wide = pl.BlockSpec((tm, W), lambda i: (i, 0))
    narrow = pl.BlockSpec((tm, LANES), lambda i: (i, 0))
    return pl.pallas_call(
        _combine_kernel,
        grid=(T // tm,),
        in_specs=[wide, wide, wide, narrow, narrow, narrow],
        out_specs=wide,
        out_shape=jax.ShapeDtypeStruct((T, W), BF16),
        compiler_params=_params("parallel"),
        name="combine_groups",
    )(*outs, *lses)


def _deinterleave_heads(w, heads, dim):
    k = w.shape[0]
    return w.reshape(k, heads, dim // 2, 2).transpose(0, 1, 3, 2).reshape(k, heads * dim)


def mixer_a(xr, B, S, nw, a_w_in, a_q_norm_w, a_k_norm_w, lq1, lk1, lq2, lk2, a_subln_w, a_w_out, *, layer_idx):
    T, D = xr.shape
    colw = jnp.concatenate([
        jnp.tile(a_q_norm_w.astype(F32) * (A_HEAD_DIM ** -0.5 * math.log2(math.e)), 2 * A_HEADS),
        jnp.tile(a_k_norm_w.astype(F32), 2 * A_HEADS)]).reshape(1, 2 * D)
    qk = norm_matmul(xr, nw, a_w_in[:, :2 * D].astype(BF16), colw)
    v_t = norm_matmul_t(xr, nw, a_w_in[:, 2 * D:].T.astype(BF16))
    o = diff_attention(qk.reshape(B, S, 2 * D), v_t, lq1, lk1, lq2, lk2, a_subln_w, layer_idx=layer_idx)
    return matmul_residual(o.reshape(T, D), a_w_out.astype(BF16), xr)


def mixer_b(xr, B, S, nw, b_w_in, b_gn_w, b_w_out):
    T, D = xr.shape
    w_in = jnp.concatenate([
        _deinterleave_heads(b_w_in[:, :D], R_HEADS, R_KDIM),
        _deinterleave_heads(b_w_in[:, D:2 * D], R_HEADS, R_KDIM),
        b_w_in[:, 2 * D:]], axis=1)
    proj = norm_matmul(xr, nw, w_in.astype(BF16))
    o = retention(proj.reshape(B, S, -1), b_gn_w)
    return matmul_residual(o.reshape(T, -1), b_w_out.astype(BF16), xr)


def mixer_c(xr, B, S, nw, c_w_in, c_conv_w, c_conv_b, c_dt_bias, c_a_log, c_d_skip, c_norm_w, c_w_out):
    T, D = xr.shape
    n_main = 2 * M_D_INNER + 2 * M_GROUPS * M_DSTATE
    zx = norm_matmul(xr, nw, c_w_in[:, :n_main].astype(BF16))
    w_dt = jnp.pad(c_w_in[:, n_main:], ((0, 0), (0, LANES - M_HEADS)))
    dt_raw = norm_matmul(xr, nw, w_dt.astype(BF16), tn=LANES, out_dtype=F32)
    y = mamba_ssd(zx.reshape(B, S, n_main), dt_raw.reshape(B, S, LANES), c_conv_w, c_conv_b, c_dt_bias,
                  c_a_log, c_d_skip, c_norm_w)
    return matmul_residual(y.reshape(T, M_D_INNER), c_w_out.astype(BF16), xr)


def mixer_d(xr, B, S, nw, d_w_in, d_q_norm_w, d_k_norm_w, d_w_out):
    T, D = xr.shape
    scale = D_HEAD_DIM ** -0.5
    colw = jnp.concatenate([
        jnp.concatenate([jnp.tile(d_q_norm_w[g].astype(F32) * scale, D_HEADS),
                         jnp.tile(d_k_norm_w[g].astype(F32), D_HEADS),
                         jnp.ones((D,), F32)])
        for g in range(len(D_PATTERNS))]).reshape(1, -1)
    qkv = norm_matmul(xr, nw, d_w_in.astype(BF16), colw).reshape(B, S, -1)
    outs, lses = zip(*[dilated_group(qkv, g, dil, tiles=max(1, 8 // dil), unroll=8 if dil < 16 else 4,
                                     mxu_sums=dil < 16)
                       for g, (_, dil) in enumerate(D_PATTERNS)])
    o = combine_groups(outs, lses)
    return matmul_residual(o, d_w_out.astype(BF16), xr)


@jax.jit
def kernel(x, norm1_w, norm2_w, mlp_w1, mlp_w2, a_w_in, a_q_norm_w, a_k_norm_w, a_lambda_q1, a_lambda_k1, a_lambda_q2, a_lambda_k2, a_subln_w, a_w_out, b_w_in, b_gn_w, b_w_out, c_w_in, c_conv_w, c_conv_b, c_dt_bias, c_a_log, c_d_skip, c_norm_w, c_w_out, d_w_in, d_q_norm_w, d_k_norm_w, d_w_out):
    B, S, D = x.shape
    xr = x.reshape(B * S, D)
    ffn = lambda t, i: mlp(t, norm2_w[i], mlp_w1[i].astype(BF16), mlp_w2[i].astype(BF16))
    xr = mixer_a(xr, B, S, norm1_w[0], a_w_in, a_q_norm_w, a_k_norm_w, a_lambda_q1, a_lambda_k1,
                 a_lambda_q2, a_lambda_k2, a_subln_w, a_w_out, layer_idx=0)
    xr = ffn(xr, 0)
    xr = mixer_b(xr, B, S, norm1_w[1], b_w_in, b_gn_w, b_w_out)
    xr = ffn(xr, 1)
    xr = mixer_c(xr, B, S, norm1_w[2], c_w_in, c_conv_w, c_conv_b, c_dt_bias, c_a_log, c_d_skip, c_norm_w,
                 c_w_out)
    xr = ffn(xr, 2)
    xr = mixer_d(xr, B, S, norm1_w[3], d_w_in, d_q_norm_w, d_k_norm_w, d_w_out)
    xr = ffn(xr, 3)
    return xr.reshape(B, S, D)
```

```python
import functools
import math

import jax
import jax.numpy as jnp
from jax import lax
from jax.experimental import pallas as pl
from jax.experimental.pallas import tpu as pltpu

F32 = jnp.float32
BF16 = jnp.bfloat16

EPS = 1e-6
D_MODEL = 2048
D_FF = 4 * D_MODEL
LANES = 128
MXU_COLS = 256
ROW_BLOCK = 256

A_HEAD_DIM = 128
A_HEADS = 8
R_HEADS = 8
R_KDIM = 256
R_VDIM = 512
M_D_INNER = 4096
M_HEADDIM = 64
M_HEADS = 64
M_GROUPS = 8
M_DSTATE = 128
M_CONV = 4
M_GROUP_W = M_D_INNER // M_GROUPS
M_HPG = M_HEADS // M_GROUPS
D_HEADS = 16
D_HEAD_DIM = 128
D_PATTERNS = ((128, 1), (512, 4), (2048, 16))
D_SPAN = 128

VMEM_LIMIT_BYTES = 56 * 1024 * 1024
NEG_BIG = -1e30

NT_DIMS = (((1,), (1,)), ((), ()))
TN_DIMS = (((0,), (0,)), ((), ()))


def _params(*sem):
    return pltpu.CompilerParams(dimension_semantics=sem, vmem_limit_bytes=VMEM_LIMIT_BYTES)


def _silu(v):
    return v * (1.0 / (1.0 + jnp.exp(-v)))


def _rms_rows(v, w):
    ms = jnp.mean(v * v, axis=-1, keepdims=True)
    return v * lax.rsqrt(ms + EPS) * w


def _split3(v):
    p1 = v.astype(BF16)
    r1 = v - p1.astype(F32)
    p2 = r1.astype(BF16)
    return p1, p2, (r1 - p2.astype(F32)).astype(BF16)


def _row_blocks(tm):
    rb = min(tm, ROW_BLOCK)
    return [slice(r, r + rb) for r in range(0, tm, rb)]


def _norm_matmul_kernel(x_ref, nw_ref, w_ref, cw_ref, *rest, tm, tn, head_norm, period, count):
    if len(rest) == 4:
        ws_ref, o_ref, side_ref, h_sc = rest
    else:
        (o_ref, h_sc), ws_ref, side_ref = rest, None, None
    j = pl.program_id(1)
    every = slice(0, tm)

    def plain(rows):
        o_ref[rows, :] = jnp.dot(h_sc[rows, :], w_ref[...], preferred_element_type=F32).astype(o_ref.dtype)

    def normed(rows):
        sub = min(tn, MXU_COLS)
        for c in range(tn // sub):
            acc = jnp.dot(h_sc[rows, :], w_ref[:, c * sub:(c + 1) * sub], preferred_element_type=F32)
            for d in range(sub // LANES):
                sl = slice(c * sub + d * LANES, c * sub + (d + 1) * LANES)
                o_ref[rows, sl] = _rms_rows(acc[:, d * LANES:(d + 1) * LANES], cw_ref[:, sl]).astype(o_ref.dtype)

    @pl.when(j == 0)
    def _():
        for rows in _row_blocks(tm):
            h_sc[rows, :] = _rms_rows(x_ref[rows, :], nw_ref[...]).astype(BF16)
            (normed if head_norm else plain)(rows)
            if side_ref is not None:
                side_ref[rows, :] = jnp.dot(h_sc[rows, :], ws_ref[...], preferred_element_type=F32)

    if not head_norm:
        pl.when(j > 0)(functools.partial(plain, every))
        return
    is_norm = (j % period) < count
    pl.when(jnp.logical_and(j > 0, is_norm))(functools.partial(normed, every))
    pl.when(jnp.logical_not(is_norm))(functools.partial(plain, every))


def norm_matmul(x, nw, w, colw=None, side_w=None, *, tm=1024, tn=1024):
    T, K = x.shape
    N = w.shape[1]
    tm = min(tm, T)
    tn = min(tn, N)
    head_norm = colw is not None
    if colw is None:
        colw = jnp.ones((1, N), F32)
    period = 3 * D_MODEL // tn
    count = 2 * D_MODEL // tn
    kern = functools.partial(_norm_matmul_kernel, tm=tm, tn=tn, head_norm=head_norm, period=period, count=count)
    in_specs = [
        pl.BlockSpec((tm, K), lambda i, j: (i, 0)),
        pl.BlockSpec((1, K), lambda i, j: (0, 0)),
        pl.BlockSpec((K, tn), lambda i, j: (0, j)),
        pl.BlockSpec((1, tn), lambda i, j: (0, j)),
    ]
    out_specs = pl.BlockSpec((tm, tn), lambda i, j: (i, j))
    out_shape = jax.ShapeDtypeStruct((T, N), BF16)
    args = (x, nw.reshape(1, K), w, colw)
    if side_w is not None:
        in_specs.append(pl.BlockSpec((K, LANES), lambda i, j: (0, 0)))
        out_specs = [out_specs, pl.BlockSpec((tm, LANES), lambda i, j: (i, 0))]
        out_shape = [out_shape, jax.ShapeDtypeStruct((T, LANES), F32)]
        args += (side_w,)
    return pl.pallas_call(
        kern,
        grid=(T // tm, N // tn),
        in_specs=in_specs,
        out_specs=out_specs,
        out_shape=out_shape,
        scratch_shapes=[pltpu.VMEM((tm, K), BF16)],
        compiler_params=_params("parallel", "arbitrary"),
        name="norm_matmul",
    )(*args)


def _norm_matmul_t_kernel(x_ref, nw_ref, wt_ref, o_ref, h_sc, *, tm):
    j = pl.program_id(1)

    def emit(rows):
        o_ref[:, rows] = lax.dot_general(wt_ref[...], h_sc[rows, :], NT_DIMS,
                                         preferred_element_type=F32).astype(o_ref.dtype)

    @pl.when(j == 0)
    def _():
        for rows in _row_blocks(tm):
            h_sc[rows, :] = _rms_rows(x_ref[rows, :], nw_ref[...]).astype(BF16)
            emit(rows)

    pl.when(j > 0)(functools.partial(emit, slice(0, tm)))


def norm_matmul_t(x, nw, w_t, *, tm=1024, tn=1024):
    T, K = x.shape
    N = w_t.shape[0]
    tm = min(tm, T)
    return pl.pallas_call(
        functools.partial(_norm_matmul_t_kernel, tm=tm),
        grid=(T // tm, N // tn),
        in_specs=[
            pl.BlockSpec((tm, K), lambda i, j: (i, 0)),
            pl.BlockSpec((1, K), lambda i, j: (0, 0)),
            pl.BlockSpec((tn, K), lambda i, j: (j, 0)),
        ],
        out_specs=pl.BlockSpec((tn, tm), lambda i, j: (j, i)),
        out_shape=jax.ShapeDtypeStruct((N, T), BF16),
        scratch_shapes=[pltpu.VMEM((tm, K), BF16)],
        compiler_params=_params("parallel", "arbitrary"),
        name="norm_matmul_t",
    )(x, nw.reshape(1, K), w_t)


def _matmul_residual_kernel(a_ref, w_ref, r_ref, o_ref):
    o_ref[...] = r_ref[...] + jnp.dot(a_ref[...], w_ref[...], preferred_element_type=F32)


def matmul_residual(a, w, res, *, tm=1024, tn=1024):
    T, K = a.shape
    N = w.shape[1]
    tm = min(tm, T)
    return pl.pallas_call(
        _matmul_residual_kernel,
        grid=(T // tm, N // tn),
        in_specs=[
            pl.BlockSpec((tm, K), lambda i, j: (i, 0)),
            pl.BlockSpec((K, tn), lambda i, j: (0, j)),
            pl.BlockSpec((tm, tn), lambda i, j: (i, j)),
        ],
        out_specs=pl.BlockSpec((tm, tn), lambda i, j: (i, j)),
        out_shape=jax.ShapeDtypeStruct((T, N), F32),
        compiler_params=_params("parallel", "arbitrary"),
        name="matmul_residual",
    )(a, w, res)


def _mlp_kernel(x_ref, nw_ref, w1_ref, w2_ref, o_ref, h_sc, *, tm):
    f = pl.program_id(1)

    def accumulate(rows):
        u = jnp.dot(h_sc[rows, :], w1_ref[0], preferred_element_type=F32)
        u = jnp.square(jnp.maximum(u, 0.0)).astype(BF16)
        o_ref[rows, :] += jnp.dot(u, w2_ref[0], preferred_element_type=F32)

    @pl.when(f == 0)
    def _():
        for rows in _row_blocks(tm):
            x = x_ref[rows, :]
            h_sc[rows, :] = _rms_rows(x, nw_ref[...]).astype(BF16)
            o_ref[rows, :] = x
            accumulate(rows)

    pl.when(f > 0)(functools.partial(accumulate, slice(0, tm)))


def mlp(x, nw, w1, w2, layer, *, tm=512, tf=1024):
    T, D = x.shape
    FF = w1.shape[2]
    tm = min(tm, T)
    return pl.pallas_call(
        functools.partial(_mlp_kernel, tm=tm),
        grid=(T // tm, FF // tf),
        in_specs=[
            pl.BlockSpec((tm, D), lambda i, f: (i, 0)),
            pl.BlockSpec((1, D), lambda i, f: (0, 0)),
            pl.BlockSpec((1, D, tf), lambda i, f: (layer, 0, f)),
            pl.BlockSpec((1, tf, D), lambda i, f: (layer, f, 0)),
        ],
        out_specs=pl.BlockSpec((tm, D), lambda i, f: (i, 0)),
        out_shape=jax.ShapeDtypeStruct((T, D), F32),
        scratch_shapes=[pltpu.VMEM((tm, D), BF16)],
        compiler_params=_params("parallel", "arbitrary"),
        name="mlp",
    )(x, nw.reshape(1, D), w1, w2)


def _diff_attn_kernel(q_ref, k_ref, vt_ref, lq1_ref, lk1_ref, lq2_ref, lk2_ref, sw_ref, o_ref,
                      m_sc, l_sc, acc_sc, sa_sc, sb_sc, *, tq, lambda_init):
    qi = pl.program_id(2)
    m_sc[...] = jnp.full_like(m_sc, NEG_BIG)
    l_sc[...] = jnp.zeros_like(l_sc)
    acc_sc[...] = jnp.zeros_like(acc_sc)
    key = lax.broadcasted_iota(jnp.int32, (tq, tq), 0)
    qry = lax.broadcasted_iota(jnp.int32, (tq, tq), 1)

    def scores(m, start):
        hs = slice(m * A_HEAD_DIM, (m + 1) * A_HEAD_DIM)
        return lax.dot_general(k_ref[0, pl.ds(start, tq), hs], q_ref[0, :, hs], NT_DIMS,
                               preferred_element_type=F32)

    def update(m, s, start, masked):
        if masked:
            s = jnp.where(key <= qry, s, NEG_BIG)
        m_prev = m_sc[m]
        m_new = jnp.maximum(m_prev, jnp.max(s, axis=0, keepdims=True))
        alpha = jnp.exp2(m_prev - m_new)
        p = jnp.exp2(s - m_new)
        m_sc[m] = m_new
        l_sc[m] = alpha * l_sc[m] + jnp.sum(p, axis=0, keepdims=True)
        acc_sc[m] = alpha * acc_sc[m] + jnp.dot(vt_ref[:, pl.ds(start, tq)], p.astype(BF16),
                                                preferred_element_type=F32)

    def put_scores(s_sc, j):
        start = pl.multiple_of(j * tq, tq)
        s_sc[0] = scores(0, start)
        s_sc[1] = scores(1, start)

    def updates(s_sc, j, masked):
        start = pl.multiple_of(j * tq, tq)
        update(0, s_sc[0], start, masked)
        update(1, s_sc[1], start, masked)

    put_scores(sa_sc, 0)

    def body(jj, carry):
        j = 2 * jj
        put_scores(sb_sc, j + 1)
        updates(sa_sc, j, False)
        put_scores(sa_sc, j + 2)
        updates(sb_sc, j + 1, False)
        return carry

    lax.fori_loop(0, qi // 2, body, 0)

    @pl.when(qi % 2 == 0)
    def _():
        updates(sa_sc, qi, True)

    @pl.when(qi % 2 == 1)
    def _():
        put_scores(sb_sc, qi)
        updates(sa_sc, qi - 1, False)
        updates(sb_sc, qi, True)

    lam = (jnp.exp(jnp.sum(lq1_ref[...] * lk1_ref[...], axis=-1, keepdims=True))
           - jnp.exp(jnp.sum(lq2_ref[...] * lk2_ref[...], axis=-1, keepdims=True)) + lambda_init)
    o_t = acc_sc[0] * (1.0 / l_sc[0]) - lam * (acc_sc[1] * (1.0 / l_sc[1]))
    o_ref[0] = (_rms_rows(o_t.T, sw_ref[...]) * (1.0 - lambda_init)).astype(o_ref.dtype)


def diff_attention(qk, v_t, lq1, lk1, lq2, lk2, subln_w, *, layer_idx, tq=512):
    B, S, _ = qk.shape
    tq = min(tq, S)
    lambda_init = 0.8 - 0.6 * math.exp(-0.3 * layer_idx)
    pw = 2 * A_HEAD_DIM
    vec = lambda a: a.reshape(1, -1).astype(F32)
    small = lambda n: pl.BlockSpec((1, n), lambda b, h, i: (0, 0))
    kern = functools.partial(_diff_attn_kernel, tq=tq, lambda_init=lambda_init)
    return pl.pallas_call(
        kern,
        grid=(B, A_HEADS, S // tq),
        in_specs=[
            pl.BlockSpec((1, tq, pw), lambda b, h, i: (b, i, h)),
            pl.BlockSpec((1, S, pw), lambda b, h, i: (b, 0, A_HEADS + h)),
            pl.BlockSpec((pw, S), lambda b, h, i: (h, b)),
            small(A_HEAD_DIM), small(A_HEAD_DIM), small(A_HEAD_DIM), small(A_HEAD_DIM), small(pw),
        ],
        out_specs=pl.BlockSpec((1, tq, pw), lambda b, h, i: (b, i, h)),
        out_shape=jax.ShapeDtypeStruct((B, S, D_MODEL), BF16),
        scratch_shapes=[
            pltpu.VMEM((2, 1, tq), F32),
            pltpu.VMEM((2, 1, tq), F32),
            pltpu.VMEM((2, pw, tq), F32),
            pltpu.VMEM((2, tq, tq), F32),
            pltpu.VMEM((2, tq, tq), F32),
        ],
        compiler_params=_params("parallel", "parallel", "arbitrary"),
        name="diff_attention",
    )(qk, qk, v_t, vec(lq1), vec(lk1), vec(lq2), vec(lk2), vec(subln_w))


def _retention_kernel(q_ref, k_ref, v_ref, g_ref, cos_ref, sin_ref, dm_ref, xi_ref, zeta_ref, cd_ref,
                      gw_ref, o_ref, r_sc, *, tr, chunk):
    @pl.when(pl.program_id(2) == 0)
    def _():
        r_sc[...] = jnp.zeros_like(r_sc)

    half = R_KDIM // 2
    dmask = dm_ref[0]
    xi = xi_ref[0]
    zeta = zeta_ref[0]
    cdecay = cd_ref[0, 0:1, 0:1]
    gw = gw_ref[0]

    def rotate(t, c, s):
        te, to = t[:, :half], t[:, half:]
        return te * c - to * s, to * c + te * s

    for ci in range(tr // chunk):
        rows = slice(ci * chunk, (ci + 1) * chunk)
        c = cos_ref[rows, :]
        s = sin_ref[rows, :]
        qe, qo = rotate(q_ref[0, rows, :].astype(F32), c, s)
        ke, ko = rotate(k_ref[0, rows, :].astype(F32) * (R_KDIM ** -0.5), c, s)
        v = v_ref[0, rows, :]
        q_r = jnp.concatenate([qe, qo], axis=1).astype(BF16)
        k_r = jnp.concatenate([ke, ko], axis=1).astype(BF16)
        q_x = jnp.concatenate([qe * xi, qo * xi], axis=1).astype(BF16)
        k_z = jnp.concatenate([ke * zeta, ko * zeta], axis=1).astype(BF16)
        r_old = r_sc[...]
        sc = lax.dot_general(q_r, k_r, NT_DIMS, preferred_element_type=F32) * dmask
        o = (jnp.dot(sc.astype(BF16), v, preferred_element_type=F32)
             + jnp.dot(q_x, r_old.astype(BF16), preferred_element_type=F32))
        r_sc[...] = r_old * cdecay + lax.dot_general(k_z, v, TN_DIMS, preferred_element_type=F32)
        gate = _silu(g_ref[0, rows, :].astype(F32))
        o_ref[0, rows, :] = (gate * _rms_rows(o, gw)).astype(o_ref.dtype)


def retention(proj, gn_w, *, tr=1024, chunk=256):
    B, S, _ = proj.shape
    tr = min(tr, S)
    chunk = min(chunk, tr)
    half = R_KDIM // 2
    angle = 1.0 / (10000.0 ** jnp.linspace(0.0, 1.0, half, dtype=F32))
    ang = jnp.arange(S, dtype=F32)[:, None] * angle[None, :]
    cos, sin = jnp.cos(ang), jnp.sin(ang)
    log_g = jnp.log(1.0 - 2.0 ** (-5.0 - jnp.arange(R_HEADS, dtype=F32)))
    idx = jnp.arange(chunk, dtype=F32)
    rel = idx[:, None] - idx[None, :]
    dmask = jnp.where(rel[None] >= 0, jnp.exp(jnp.maximum(rel, 0.0)[None] * log_g[:, None, None]), 0.0)
    xi = jnp.exp((idx + 1.0)[None, :] * log_g[:, None])
    zeta = jnp.exp((chunk - 1.0 - idx)[None, :] * log_g[:, None])
    cdecay = jnp.exp(chunk * log_g)
    bc = lambda t: jnp.broadcast_to(t[:, :, None], (R_HEADS, chunk, half))
    cd = jnp.broadcast_to(cdecay[:, None, None], (R_HEADS, 8, LANES))
    nq = D_MODEL // R_KDIM
    nv = 2 * D_MODEL // R_VDIM
    kern = functools.partial(_retention_kernel, tr=tr, chunk=chunk)
    head = lambda shape: pl.BlockSpec(shape, lambda b, h, i: (h, 0, 0))
    return pl.pallas_call(
        kern,
        grid=(B, R_HEADS, S // tr),
        in_specs=[
            pl.BlockSpec((1, tr, R_KDIM), lambda b, h, i: (b, i, h)),
            pl.BlockSpec((1, tr, R_KDIM), lambda b, h, i: (b, i, nq + h)),
            pl.BlockSpec((1, tr, R_VDIM), lambda b, h, i: (b, i, nv + h)),
            pl.BlockSpec((1, tr, R_VDIM), lambda b, h, i: (b, i, nv + R_HEADS + h)),
            pl.BlockSpec((tr, half), lambda b, h, i: (i, 0)),
            pl.BlockSpec((tr, half), lambda b, h, i: (i, 0)),
            head((1, chunk, chunk)), head((1, chunk, half)), head((1, chunk, half)),
            head((1, 8, LANES)), head((1, 1, R_VDIM)),
        ],
        out_specs=pl.BlockSpec((1, tr, R_VDIM), lambda b, h, i: (b, i, h)),
        out_shape=jax.ShapeDtypeStruct((B, S, R_HEADS * R_VDIM), BF16),
        scratch_shapes=[pltpu.VMEM((R_KDIM, R_VDIM), F32)],
        compiler_params=_params("parallel", "parallel", "arbitrary"),
        name="retention",
    )(proj, proj, proj, proj, cos, sin, dmask, bc(xi), bc(zeta), cd, gn_w.reshape(R_HEADS, 1, R_VDIM).astype(F32))


def _mamba_kernel(z_ref, x_ref, b_ref, c_ref, dt_ref, wx_ref, wb_ref, wc_ref, bx_ref, bb_ref, bc_ref,
                  dtb_ref, alog_ref, dsk_ref, nw_ref, o_ref,
                  st_sc, tx_sc, tb_sc, tc_sc, *, chunk):
    L = chunk

    @pl.when(pl.program_id(2) == 0)
    def _():
        st_sc[...] = jnp.zeros_like(st_sc)
        for ext_sc in (tx_sc, tb_sc, tc_sc):
            ext_sc[0:8, :] = jnp.zeros((8, ext_sc.shape[1]), F32)

    def conv_silu(cur_ref, ext_sc, w_ref, bias_ref):
        ext_sc[8:, :] = cur_ref[0].astype(F32)
        w = w_ref[...]
        acc = bias_ref[...] + ext_sc[8:, :] * w[M_CONV - 1:M_CONV, :]
        for kk in range(M_CONV - 1):
            sh = M_CONV - 1 - kk
            acc = acc + ext_sc[8 - sh:8 - sh + L, :] * w[kk:kk + 1, :]
        ext_sc[0:8, :] = ext_sc[L:, :]
        return _silu(acc)

    raw = dt_ref[0, 0] + dtb_ref[0]
    dt_t = jnp.maximum(raw, 0.0) + jnp.log(1.0 + jnp.exp(-jnp.abs(raw)))
    a_t = dt_t * (-jnp.exp(alog_ref[0])) * math.log2(math.e)

    ri = lax.broadcasted_iota(jnp.int32, (L, L), 0)
    ci = lax.broadcasted_iota(jnp.int32, (L, L), 1)
    tril = ci <= ri
    eye = jnp.where(ri == ci, 1.0, 0.0).astype(BF16)
    lower = jnp.where(tril, 1.0, 0.0).astype(BF16)
    upper = jnp.where(ri <= ci, 1.0, 0.0).astype(BF16)

    def per_column(t):
        return jnp.concatenate([jnp.broadcast_to(t[r:r + 1, :], (M_HEADDIM, L)) for r in range(M_HPG)], axis=0)

    a_parts = _split3(a_t)
    acs_row = sum(jnp.dot(p, upper, preferred_element_type=F32) for p in a_parts)

    def expand_dot(mat01, parts):
        return sum(lax.dot_general(mat01, per_column(p.astype(F32)).astype(BF16), NT_DIMS,
                                   preferred_element_type=F32) for p in parts)

    acs_x = expand_dot(lower, a_parts)
    dt_x = expand_dot(eye, _split3(dt_t))

    xs = conv_silu(x_ref, tx_sc, wx_ref, bx_ref)
    bm = conv_silu(b_ref, tb_sc, wb_ref, bb_ref)
    cm = conv_silu(c_ref, tc_sc, wc_ref, bc_ref)

    xdt = xs * dt_x
    cb = lax.dot_general(cm.astype(BF16), bm.astype(BF16), NT_DIMS, preferred_element_type=F32)
    xdt_b = xdt.astype(BF16)
    parts = []
    for r in range(M_HPG):
        diff = acs_x[:, r * M_HEADDIM:r * M_HEADDIM + 1] - acs_row[r:r + 1, :]
        lmat = jnp.exp2(jnp.where(tril, diff, NEG_BIG))
        mr = (cb * lmat).astype(BF16)
        parts.append(jnp.dot(mr, xdt_b[:, r * M_HEADDIM:(r + 1) * M_HEADDIM], preferred_element_type=F32))
    y = jnp.concatenate(parts, axis=1)

    state = st_sc[...]
    y = y + jnp.exp2(acs_x) * jnp.dot(cm.astype(BF16), state.astype(BF16), preferred_element_type=F32)
    last = acs_x[L - 1:L, :]
    decay_end = jnp.exp2(last - acs_x)
    st_sc[...] = state * jnp.exp2(last) + lax.dot_general(
        bm.astype(BF16), (xdt * decay_end).astype(BF16), TN_DIMS, preferred_element_type=F32)

    y = y + xs * dsk_ref[...]
    y = y * _silu(z_ref[0].astype(F32))
    o_ref[0] = _rms_rows(y, nw_ref[...]).astype(o_ref.dtype)


def mamba_ssd(zx, dt_raw, conv_w, conv_b, dt_bias, a_log, d_skip, norm_w, *, chunk=256):
    B, S, _ = zx.shape
    chunk = min(chunk, S)
    G, W, N = M_GROUPS, M_GROUP_W, M_DSTATE
    dt_t = dt_raw[:, :, :M_HEADS].reshape(B, S, G, M_HPG).transpose(0, 2, 3, 1)
    xoff = M_D_INNER // W
    boff = 2 * M_D_INNER // N
    coff = boff + G
    cwb = M_D_INNER // N
    conv_w = conv_w.astype(F32)
    conv_b = conv_b.reshape(1, -1).astype(F32)
    per_head = lambda t: t.reshape(G, M_HPG, 1).astype(F32)
    dsk_x = jnp.repeat(d_skip.astype(F32), M_HEADDIM).reshape(1, M_D_INNER)
    kern = functools.partial(_mamba_kernel, chunk=chunk)
    return pl.pallas_call(
        kern,
        grid=(B, G, S // chunk),
        in_specs=[
            pl.BlockSpec((1, chunk, W), lambda b, g, c: (b, c, g)),
            pl.BlockSpec((1, chunk, W), lambda b, g, c: (b, c, xoff + g)),
            pl.BlockSpec((1, chunk, N), lambda b, g, c: (b, c, boff + g)),
            pl.BlockSpec((1, chunk, N), lambda b, g, c: (b, c, coff + g)),
            pl.BlockSpec((1, 1, M_HPG, chunk), lambda b, g, c: (b, g, 0, c)),
            pl.BlockSpec((M_CONV, W), lambda b, g, c: (0, g)),
            pl.BlockSpec((M_CONV, N), lambda b, g, c: (0, cwb + g)),
            pl.BlockSpec((M_CONV, N), lambda b, g, c: (0, cwb + G + g)),
            pl.BlockSpec((1, W), lambda b, g, c: (0, g)),
            pl.BlockSpec((1, N), lambda b, g, c: (0, cwb + g)),
            pl.BlockSpec((1, N), lambda b, g, c: (0, cwb + G + g)),
            pl.BlockSpec((1, M_HPG, 1), lambda b, g, c: (g, 0, 0)),
            pl.BlockSpec((1, M_HPG, 1), lambda b, g, c: (g, 0, 0)),
            pl.BlockSpec((1, W), lambda b, g, c: (0, g)),
            pl.BlockSpec((1, W), lambda b, g, c: (0, g)),
        ],
        out_specs=pl.BlockSpec((1, chunk, W), lambda b, g, c: (b, c, g)),
        out_shape=jax.ShapeDtypeStruct((B, S, M_D_INNER), BF16),
        scratch_shapes=[
            pltpu.VMEM((N, W), F32),
            pltpu.VMEM((8 + chunk, W), F32),
            pltpu.VMEM((8 + chunk, N), F32),
            pltpu.VMEM((8 + chunk, N), F32),
        ],
        compiler_params=_params("parallel", "parallel", "arbitrary"),
        name="mamba_ssd",
    )(zx, zx, zx, zx, dt_t, conv_w, conv_w, conv_w, conv_b, conv_b, conv_b,
      per_head(dt_bias), per_head(a_log), dsk_x, norm_w.reshape(1, M_D_INNER).astype(F32))


def _window_attend(operands, mxu_sums):
    T = D_SPAN
    ri = lax.broadcasted_iota(jnp.int32, (T, T), 0)
    ci = lax.broadcasted_iota(jnp.int32, (T, T), 1)
    ones = jnp.ones((T, LANES), BF16)
    scores = [(lax.dot_general(q, kp, NT_DIMS, preferred_element_type=F32),
               lax.dot_general(q, kc, NT_DIMS, preferred_element_type=F32))
              for q, kp, kc, _, _, _ in operands]
    probs = []
    for (sp, sc), (_, _, _, _, _, has_prev) in zip(scores, operands):
        sp = jnp.where((ci >= ri) & has_prev, sp, NEG_BIG)
        sc = jnp.where(ci <= ri, sc, NEG_BIG)
        mx = jnp.maximum(jnp.max(sp, axis=-1, keepdims=True), jnp.max(sc, axis=-1, keepdims=True))
        pp = jnp.exp(sp - mx)
        pc = jnp.exp(sc - mx)
        l = None if mxu_sums else jnp.sum(pp, axis=-1, keepdims=True) + jnp.sum(pc, axis=-1, keepdims=True)
        probs.append((pp.astype(BF16), pc.astype(BF16), mx, l))
    outs = []
    for (pp, pc, mx, l), (_, _, _, vp, vc, _) in zip(probs, operands):
        o = jnp.dot(pp, vp, preferred_element_type=F32) + jnp.dot(pc, vc, preferred_element_type=F32)
        if mxu_sums:
            l = jnp.dot(pp, ones, preferred_element_type=F32) + jnp.dot(pc, ones, preferred_element_type=F32)
        outs.append((o * (1.0 / l), mx + jnp.log(l)))
    return outs


def _dense_window_kernel(q_ref, kp_ref, kc_ref, vp_ref, vc_ref, o_ref, lse_ref, lse_sc, *, tiles, unroll, mxu_sums):
    n = pl.program_id(1)
    h = pl.program_id(2)
    T = D_SPAN

    @pl.when(h == 0)
    def _():
        lse_sc[...] = jnp.zeros_like(lse_sc)

    lane = lax.broadcasted_iota(jnp.int32, (T, LANES), 1)
    for i0 in range(0, tiles, unroll):
        operands = []
        for i in range(i0, i0 + unroll):
            cur = slice(i * T, (i + 1) * T)
            prev = slice((i - 1) * T, i * T)
            kp = kp_ref[0] if i == 0 else kc_ref[0, prev, :]
            vp = vp_ref[0] if i == 0 else vc_ref[0, prev, :]
            operands.append((q_ref[0, cur, :], kp, kc_ref[0, cur, :], vp, vc_ref[0, cur, :],
                             jnp.logical_or(n > 0, i > 0)))
        for i, (o, lse) in zip(range(i0, i0 + unroll), _window_attend(operands, mxu_sums)):
            cur = slice(i * T, (i + 1) * T)
            o_ref[0, cur, :] = o.astype(o_ref.dtype)
            lse_sc[cur, :] = jnp.where(lane == h, lse, lse_sc[cur, :])

    @pl.when(h == D_HEADS - 1)
    def _():
        lse_ref[0] = lse_sc[...]


def _strided_window_kernel(q_ref, k_ref, v_ref, o_ref, lse_ref, lse_sc, q_sc, k_sc, v_sc, o_sc, kprev_sc, vprev_sc,
                           *, dil, tiles, unroll, mxu_sums):
    n = pl.program_id(1)
    h = pl.program_id(2)
    T = D_SPAN
    span = T * dil
    q_sc[...] = q_ref[0].astype(F32)
    k_sc[...] = k_ref[0].astype(F32)
    v_sc[...] = v_ref[0].astype(F32)

    @pl.when(h == 0)
    def _():
        lse_sc[...] = jnp.zeros_like(lse_sc)

    @pl.when(n == 0)
    def _():
        kprev_sc[h] = jnp.zeros(kprev_sc.shape[1:], BF16)
        vprev_sc[h] = jnp.zeros(vprev_sc.shape[1:], BF16)

    lane = lax.broadcasted_iota(jnp.int32, (T, LANES), 1)
    strided = lambda ref, start: ref[pl.ds(start, T, stride=dil), :]

    def group(tile_ids):
        operands, places = [], []
        for idx in tile_ids:
            i, r = (0, idx) if tiles == 1 else divmod(idx, dil)
            start = i * span + r
            saved = pl.ds(pl.multiple_of(r * T, T), T)
            kc = strided(k_sc, start).astype(BF16)
            vc = strided(v_sc, start).astype(BF16)
            if i == 0:
                kp, vp = kprev_sc[h, saved, :], vprev_sc[h, saved, :]
            else:
                kp = strided(k_sc, start - span).astype(BF16)
                vp = strided(v_sc, start - span).astype(BF16)
            operands.append((strided(q_sc, start).astype(BF16), kp, kc, vp, vc, jnp.logical_or(n > 0, i > 0)))
            places.append((start, saved if i == tiles - 1 else None))
        for (start, saved), (_, _, kc, _, vc, _), (o, lse) in zip(places, operands,
                                                                   _window_attend(operands, mxu_sums)):
            o_sc[pl.ds(start, T, stride=dil), :] = o
            lse_sc[pl.ds(start, T, stride=dil), :] = jnp.where(lane == h, lse, strided(lse_sc, start))
            if saved is not None:
                kprev_sc[h, saved, :] = kc
                vprev_sc[h, saved, :] = vc

    for j0 in range(0, dil * tiles, unroll):
        group(list(range(j0, j0 + unroll)))
    o_ref[0] = o_sc[...].astype(o_ref.dtype)

    @pl.when(h == D_HEADS - 1)
    def _():
        lse_ref[0] = lse_sc[...]


def dilated_group(qkv, g, dil, *, tiles, unroll, mxu_sums):
    B, S, C = qkv.shape
    assert (dil * tiles) % unroll == 0
    span = D_SPAN * dil
    tb = span * tiles
    hd = D_HEAD_DIM
    col = lambda t: (g * 3 + t) * D_HEADS
    cur = lambda t: pl.BlockSpec((1, tb, hd), lambda b, n, h: (b, n, col(t) + h))
    prev = lambda t: pl.BlockSpec((1, span, hd), lambda b, n, h: (b, jnp.maximum(n * tiles - 1, 0), col(t) + h))
    if dil == 1:
        kern = functools.partial(_dense_window_kernel, tiles=tiles, unroll=unroll, mxu_sums=mxu_sums)
        in_specs = [cur(0), prev(1), cur(1), prev(2), cur(2)]
        scratch = []
    else:
        kern = functools.partial(_strided_window_kernel, dil=dil, tiles=tiles, unroll=unroll, mxu_sums=mxu_sums)
        in_specs = [cur(0), cur(1), cur(2)]
        scratch = [pltpu.VMEM((tb, hd), F32)] * 4 + [pltpu.VMEM((D_HEADS, span, hd), BF16)] * 2
    o, lse = pl.pallas_call(
        kern,
        grid=(B, S // tb, D_HEADS),
        in_specs=in_specs,
        out_specs=[
            pl.BlockSpec((1, tb, hd), lambda b, n, h: (b, n, h)),
            pl.BlockSpec((1, tb, LANES), lambda b, n, h: (b, n, 0)),
        ],
        out_shape=[
            jax.ShapeDtypeStruct((B, S, D_HEADS * hd), BF16),
            jax.ShapeDtypeStruct((B, S, LANES), F32),
        ],
        scratch_shapes=[pltpu.VMEM((tb, LANES), F32)] + scratch,
        compiler_params=_params("parallel", "arbitrary", "arbitrary"),
        name="dilated_attention",
    )(*([qkv] * len(in_specs)))
    return o.reshape(B * S, D_HEADS * hd), lse.reshape(B * S, LANES)


def _combine_kernel(o0_ref, o1_ref, o2_ref, l0_ref, l1_ref, l2_ref, o_ref):
    l0, l1, l2 = l0_ref[...], l1_ref[...], l2_ref[...]
    mx = jnp.maximum(jnp.maximum(l0, l1), l2)
    e0, e1, e2 = jnp.exp(l0 - mx), jnp.exp(l1 - mx), jnp.exp(l2 - mx)
    inv = 1.0 / (e0 + e1 + e2)
    ws = (e0 * inv, e1 * inv, e2 * inv)
    for h in range(D_HEADS):
        hs = slice(h * D_HEAD_DIM, (h + 1) * D_HEAD_DIM)
        acc = ws[0][:, h:h + 1] * o0_ref[:, hs].astype(F32)
        acc = acc + ws[1][:, h:h + 1] * o1_ref[:, hs].astype(F32)
        acc = acc + ws[2][:, h:h + 1] * o2_ref[:, hs].astype(F32)
        o_ref[:, hs] = acc.astype(o_ref.dtype)


def combine_groups(outs, lses, *, tm=512):
    T, W = outs[0].shape
    tm = min(tm, T)
    wide = pl.BlockSpec((tm, W), lambda i: (i, 0))
    narrow = pl.BlockSpec((tm, LANES), lambda i: (i, 0))
    return pl.pallas_call(
        _combine_kernel,
        grid=(T // tm,),
        in_specs=[wide, wide, wide, narrow, narrow, narrow],
        out_specs=wide,
        out_shape=jax.ShapeDtypeStruct((T, W), BF16),
        compiler_params=_params("parallel"),
        name="combine_groups",
    )(*outs, *lses)


def _deinterleave_heads(w, heads, dim):
    k = w.shape[0]
    return w.reshape(k, heads, dim // 2, 2).transpose(0, 1, 3, 2).reshape(k, heads * dim)


def mixer_a(xr, B, S, nw, a_w_in, a_q_norm_w, a_k_norm_w, lq1, lk1, lq2, lk2, a_subln_w, a_w_out, *, layer_idx):
    T, D = xr.shape
    colw = jnp.concatenate([
        jnp.tile(a_q_norm_w.astype(F32) * (A_HEAD_DIM ** -0.5 * math.log2(math.e)), 2 * A_HEADS),
        jnp.tile(a_k_norm_w.astype(F32), 2 * A_HEADS)]).reshape(1, 2 * D)
    qk = norm_matmul(xr, nw, a_w_in[:, :2 * D].astype(BF16), colw)
    v_t = norm_matmul_t(xr, nw, a_w_in[:, 2 * D:].T.astype(BF16))
    o = diff_attention(qk.reshape(B, S, 2 * D), v_t, lq1, lk1, lq2, lk2, a_subln_w, layer_idx=layer_idx)
    return matmul_residual(o.reshape(T, D), a_w_out.astype(BF16), xr)


def mixer_b(xr, B, S, nw, b_w_in, b_gn_w, b_w_out):
    T, D = xr.shape
    w_in = jnp.concatenate([
        _deinterleave_heads(b_w_in[:, :D], R_HEADS, R_KDIM),
        _deinterleave_heads(b_w_in[:, D:2 * D], R_HEADS, R_KDIM),
        b_w_in[:, 2 * D:]], axis=1)
    proj = norm_matmul(xr, nw, w_in.astype(BF16))
    o = retention(proj.reshape(B, S, -1), b_gn_w)
    return matmul_residual(o.reshape(T, -1), b_w_out.astype(BF16), xr)


def mixer_c(xr, B, S, nw, c_w_in, c_conv_w, c_conv_b, c_dt_bias, c_a_log, c_d_skip, c_norm_w, c_w_out):
    T, D = xr.shape
    n_main = 2 * M_D_INNER + 2 * M_GROUPS * M_DSTATE
    w_dt = jnp.pad(c_w_in[:, n_main:], ((0, 0), (0, LANES - M_HEADS)))
    zx, dt_raw = norm_matmul(xr, nw, c_w_in[:, :n_main].astype(BF16), side_w=w_dt.astype(BF16))
    y = mamba_ssd(zx.reshape(B, S, n_main), dt_raw.reshape(B, S, LANES), c_conv_w, c_conv_b, c_dt_bias,
                  c_a_log, c_d_skip, c_norm_w)
    return matmul_residual(y.reshape(T, M_D_INNER), c_w_out.astype(BF16), xr)


def mixer_d(xr, B, S, nw, d_w_in, d_q_norm_w, d_k_norm_w, d_w_out):
    T, D = xr.shape
    scale = D_HEAD_DIM ** -0.5
    colw = jnp.concatenate([
        jnp.concatenate([jnp.tile(d_q_norm_w[g].astype(F32) * scale, D_HEADS),
                         jnp.tile(d_k_norm_w[g].astype(F32), D_HEADS),
                         jnp.ones((D,), F32)])
        for g in range(len(D_PATTERNS))]).reshape(1, -1)
    qkv = norm_matmul(xr, nw, d_w_in.astype(BF16), colw).reshape(B, S, -1)
    outs, lses = zip(*[dilated_group(qkv, g, dil, tiles=max(1, 8 // dil), unroll=8 if dil < 16 else 4,
                                     mxu_sums=dil < 16)
                       for g, (_, dil) in enumerate(D_PATTERNS)])
    o = combine_groups(outs, lses)
    return matmul_residual(o, d_w_out.astype(BF16), xr)


@jax.jit
def kernel(x, norm1_w, norm2_w, mlp_w1, mlp_w2, a_w_in, a_q_norm_w, a_k_norm_w, a_lambda_q1, a_lambda_k1, a_lambda_q2, a_lambda_k2, a_subln_w, a_w_out, b_w_in, b_gn_w, b_w_out, c_w_in, c_conv_w, c_conv_b, c_dt_bias, c_a_log, c_d_skip, c_norm_w, c_w_out, d_w_in, d_q_norm_w, d_k_norm_w, d_w_out):
    B, S, D = x.shape
    xr = x.reshape(B * S, D)
    w1, w2 = mlp_w1.astype(BF16), mlp_w2.astype(BF16)
    ffn = lambda t, i: mlp(t, norm2_w[i], w1, w2, i)
    xr = mixer_a(xr, B, S, norm1_w[0], a_w_in, a_q_norm_w, a_k_norm_w, a_lambda_q1, a_lambda_k1,
                 a_lambda_q2, a_lambda_k2, a_subln_w, a_w_out, layer_idx=0)
    xr = ffn(xr, 0)
    xr = mixer_b(xr, B, S, norm1_w[1], b_w_in, b_gn_w, b_w_out)
    xr = ffn(xr, 1)
    xr = mixer_c(xr, B, S, norm1_w[2], c_w_in, c_conv_w, c_conv_b, c_dt_bias, c_a_log, c_d_skip, c_norm_w,
                 c_w_out)
    xr = ffn(xr, 2)
    xr = mixer_d(xr, B, S, norm1_w[3], d_w_in, d_q_norm_w, d_k_norm_w, d_w_out)
    xr = ffn(xr, 3)
    return xr.reshape(B, S, D)
```

```python
import functools
import math

import jax
import jax.numpy as jnp
from jax import lax
from jax.experimental import pallas as pl
from jax.experimental.pallas import tpu as pltpu

F32 = jnp.float32
BF16 = jnp.bfloat16

EPS = 1e-6
D_MODEL = 2048
D_FF = 4 * D_MODEL
LANES = 128
MXU_COLS = 256
ROW_BLOCK = 256

A_HEAD_DIM = 128
A_HEADS = 8
R_HEADS = 8
R_KDIM = 256
R_VDIM = 512
M_D_INNER = 4096
M_HEADDIM = 64
M_HEADS = 64
M_GROUPS = 8
M_DSTATE = 128
M_CONV = 4
M_GROUP_W = M_D_INNER // M_GROUPS
M_HPG = M_HEADS // M_GROUPS
D_HEADS = 16
D_HEAD_DIM = 128
D_PATTERNS = ((128, 1), (512, 4), (2048, 16))
D_SPAN = 128

VMEM_LIMIT_BYTES = 56 * 1024 * 1024
NEG_BIG = -1e30

NT_DIMS = (((1,), (1,)), ((), ()))
TN_DIMS = (((0,), (0,)), ((), ()))


def _params(*sem):
    return pltpu.CompilerParams(dimension_semantics=sem, vmem_limit_bytes=VMEM_LIMIT_BYTES)


def _silu(v):
    h = 0.5 * v
    return h + h * jnp.tanh(h)


def _rms_rows(v, w):
    ms = jnp.mean(v * v, axis=-1, keepdims=True)
    return v * lax.rsqrt(ms + EPS) * w


def _split3(v):
    p1 = v.astype(BF16)
    r1 = v - p1.astype(F32)
    p2 = r1.astype(BF16)
    return p1, p2, (r1 - p2.astype(F32)).astype(BF16)


def _row_blocks(tm):
    rb = min(tm, ROW_BLOCK)
    return [slice(r, r + rb) for r in range(0, tm, rb)]


def _norm_matmul_kernel(x_ref, nw_ref, w_ref, cw_ref, *rest, tm, tn, head_norm, period, count):
    if len(rest) == 4:
        ws_ref, o_ref, side_ref, h_sc = rest
    else:
        (o_ref, h_sc), ws_ref, side_ref = rest, None, None
    j = pl.program_id(1)
    every = slice(0, tm)

    def plain(rows):
        o_ref[rows, :] = jnp.dot(h_sc[rows, :], w_ref[...], preferred_element_type=F32).astype(o_ref.dtype)

    def normed(rows):
        sub = min(tn, MXU_COLS)
        for c in range(tn // sub):
            acc = jnp.dot(h_sc[rows, :], w_ref[:, c * sub:(c + 1) * sub], preferred_element_type=F32)
            for d in range(sub // LANES):
                sl = slice(c * sub + d * LANES, c * sub + (d + 1) * LANES)
                o_ref[rows, sl] = _rms_rows(acc[:, d * LANES:(d + 1) * LANES], cw_ref[:, sl]).astype(o_ref.dtype)

    @pl.when(j == 0)
    def _():
        for rows in _row_blocks(tm):
            h_sc[rows, :] = _rms_rows(x_ref[rows, :], nw_ref[...]).astype(BF16)
            (normed if head_norm else plain)(rows)
            if side_ref is not None:
                side_ref[rows, :] = jnp.dot(h_sc[rows, :], ws_ref[...], preferred_element_type=F32)

    if not head_norm:
        pl.when(j > 0)(functools.partial(plain, every))
        return
    is_norm = (j % period) < count
    pl.when(jnp.logical_and(j > 0, is_norm))(functools.partial(normed, every))
    pl.when(jnp.logical_not(is_norm))(functools.partial(plain, every))


def norm_matmul(x, nw, w, colw=None, side_w=None, *, tm=1024, tn=1024):
    T, K = x.shape
    N = w.shape[1]
    tm = min(tm, T)
    tn = min(tn, N)
    head_norm = colw is not None
    if colw is None:
        colw = jnp.ones((1, N), F32)
    period = 3 * D_MODEL // tn
    count = 2 * D_MODEL // tn
    kern = functools.partial(_norm_matmul_kernel, tm=tm, tn=tn, head_norm=head_norm, period=period, count=count)
    in_specs = [
        pl.BlockSpec((tm, K), lambda i, j: (i, 0)),
        pl.BlockSpec((1, K), lambda i, j: (0, 0)),
        pl.BlockSpec((K, tn), lambda i, j: (0, j)),
        pl.BlockSpec((1, tn), lambda i, j: (0, j)),
    ]
    out_specs = pl.BlockSpec((tm, tn), lambda i, j: (i, j))
    out_shape = jax.ShapeDtypeStruct((T, N), BF16)
    args = (x, nw.reshape(1, K), w, colw)
    if side_w is not None:
        in_specs.append(pl.BlockSpec((K, LANES), lambda i, j: (0, 0)))
        out_specs = [out_specs, pl.BlockSpec((tm, LANES), lambda i, j: (i, 0))]
        out_shape = [out_shape, jax.ShapeDtypeStruct((T, LANES), F32)]
        args += (side_w,)
    return pl.pallas_call(
        kern,
        grid=(T // tm, N // tn),
        in_specs=in_specs,
        out_specs=out_specs,
        out_shape=out_shape,
        scratch_shapes=[pltpu.VMEM((tm, K), BF16)],
        compiler_params=_params("parallel", "arbitrary"),
        name="norm_matmul",
    )(*args)


def _norm_matmul_t_kernel(x_ref, nw_ref, wt_ref, o_ref, h_sc, *, tm):
    j = pl.program_id(1)

    def emit(rows):
        o_ref[:, rows] = lax.dot_general(wt_ref[...], h_sc[rows, :], NT_DIMS,
                                         preferred_element_type=F32).astype(o_ref.dtype)

    @pl.when(j == 0)
    def _():
        for rows in _row_blocks(tm):
            h_sc[rows, :] = _rms_rows(x_ref[rows, :], nw_ref[...]).astype(BF16)
            emit(rows)

    pl.when(j > 0)(functools.partial(emit, slice(0, tm)))


def norm_matmul_t(x, nw, w_t, *, tm=1024, tn=1024):
    T, K = x.shape
    N = w_t.shape[0]
    tm = min(tm, T)
    return pl.pallas_call(
        functools.partial(_norm_matmul_t_kernel, tm=tm),
        grid=(T // tm, N // tn),
        in_specs=[
            pl.BlockSpec((tm, K), lambda i, j: (i, 0)),
            pl.BlockSpec((1, K), lambda i, j: (0, 0)),
            pl.BlockSpec((tn, K), lambda i, j: (j, 0)),
        ],
        out_specs=pl.BlockSpec((tn, tm), lambda i, j: (j, i)),
        out_shape=jax.ShapeDtypeStruct((N, T), BF16),
        scratch_shapes=[pltpu.VMEM((tm, K), BF16)],
        compiler_params=_params("parallel", "arbitrary"),
        name="norm_matmul_t",
    )(x, nw.reshape(1, K), w_t)


def _matmul_residual_kernel(a_ref, w_ref, r_ref, o_ref):
    o_ref[...] = r_ref[...] + jnp.dot(a_ref[...], w_ref[...], preferred_element_type=F32)


def matmul_residual(a, w, res, *, tm=1024, tn=1024):
    T, K = a.shape
    N = w.shape[1]
    tm = min(tm, T)
    return pl.pallas_call(
        _matmul_residual_kernel,
        grid=(T // tm, N // tn),
        in_specs=[
            pl.BlockSpec((tm, K), lambda i, j: (i, 0)),
            pl.BlockSpec((K, tn), lambda i, j: (0, j)),
            pl.BlockSpec((tm, tn), lambda i, j: (i, j)),
        ],
        out_specs=pl.BlockSpec((tm, tn), lambda i, j: (i, j)),
        out_shape=jax.ShapeDtypeStruct((T, N), F32),
        compiler_params=_params("parallel", "arbitrary"),
        name="matmul_residual",
    )(a, w, res)


def _mlp_kernel(x_ref, nw_ref, w1_ref, w2_ref, o_ref, h_sc, *, tm):
    f = pl.program_id(1)

    def accumulate(rows):
        u = jnp.dot(h_sc[rows, :], w1_ref[0], preferred_element_type=F32)
        u = jnp.square(jnp.maximum(u, 0.0)).astype(BF16)
        o_ref[rows, :] += jnp.dot(u, w2_ref[0], preferred_element_type=F32)

    @pl.when(f == 0)
    def _():
        for rows in _row_blocks(tm):
            x = x_ref[rows, :]
            h_sc[rows, :] = _rms_rows(x, nw_ref[...]).astype(BF16)
            o_ref[rows, :] = x
            accumulate(rows)

    pl.when(f > 0)(functools.partial(accumulate, slice(0, tm)))


def mlp(x, nw, w1, w2, layer, *, tm=512, tf=1024):
    T, D = x.shape
    FF = w1.shape[2]
    tm = min(tm, T)
    return pl.pallas_call(
        functools.partial(_mlp_kernel, tm=tm),
        grid=(T // tm, FF // tf),
        in_specs=[
            pl.BlockSpec((tm, D), lambda i, f: (i, 0)),
            pl.BlockSpec((1, D), lambda i, f: (0, 0)),
            pl.BlockSpec((1, D, tf), lambda i, f: (layer, 0, f)),
            pl.BlockSpec((1, tf, D), lambda i, f: (layer, f, 0)),
        ],
        out_specs=pl.BlockSpec((tm, D), lambda i, f: (i, 0)),
        out_shape=jax.ShapeDtypeStruct((T, D), F32),
        scratch_shapes=[pltpu.VMEM((tm, D), BF16)],
        compiler_params=_params("parallel", "arbitrary"),
        name="mlp",
    )(x, nw.reshape(1, D), w1, w2)


def _diff_attn_kernel(q_ref, k_ref, vt_ref, lq1_ref, lk1_ref, lq2_ref, lk2_ref, sw_ref, o_ref,
                      m_sc, l_sc, acc_sc, sa_sc, sb_sc, *, tq, lambda_init):
    qi = pl.program_id(2)
    m_sc[...] = jnp.full_like(m_sc, NEG_BIG)
    l_sc[...] = jnp.zeros_like(l_sc)
    acc_sc[...] = jnp.zeros_like(acc_sc)
    key = lax.broadcasted_iota(jnp.int32, (tq, tq), 0)
    qry = lax.broadcasted_iota(jnp.int32, (tq, tq), 1)

    def scores(m, start):
        hs = slice(m * A_HEAD_DIM, (m + 1) * A_HEAD_DIM)
        return lax.dot_general(k_ref[0, pl.ds(start, tq), hs], q_ref[0, :, hs], NT_DIMS,
                               preferred_element_type=F32)

    def update(m, s, start, masked):
        if masked:
            s = jnp.where(key <= qry, s, NEG_BIG)
        m_prev = m_sc[m]
        m_new = jnp.maximum(m_prev, jnp.max(s, axis=0, keepdims=True))
        alpha = jnp.exp2(m_prev - m_new)
        p = jnp.exp2(s - m_new)
        m_sc[m] = m_new
        l_sc[m] = alpha * l_sc[m] + jnp.sum(p, axis=0, keepdims=True)
        acc_sc[m] = alpha * acc_sc[m] + jnp.dot(vt_ref[:, pl.ds(start, tq)], p.astype(BF16),
                                                preferred_element_type=F32)

    def put_scores(s_sc, j):
        start = pl.multiple_of(j * tq, tq)
        s_sc[0] = scores(0, start)
        s_sc[1] = scores(1, start)

    def updates(s_sc, j, masked):
        start = pl.multiple_of(j * tq, tq)
        update(0, s_sc[0], start, masked)
        update(1, s_sc[1], start, masked)

    put_scores(sa_sc, 0)

    def body(jj, carry):
        j = 2 * jj
        put_scores(sb_sc, j + 1)
        updates(sa_sc, j, False)
        put_scores(sa_sc, j + 2)
        updates(sb_sc, j + 1, False)
        return carry

    lax.fori_loop(0, qi // 2, body, 0)

    @pl.when(qi % 2 == 0)
    def _():
        updates(sa_sc, qi, True)

    @pl.when(qi % 2 == 1)
    def _():
        put_scores(sb_sc, qi)
        updates(sa_sc, qi - 1, False)
        updates(sb_sc, qi, True)

    lam = (jnp.exp(jnp.sum(lq1_ref[...] * lk1_ref[...], axis=-1, keepdims=True))
           - jnp.exp(jnp.sum(lq2_ref[...] * lk2_ref[...], axis=-1, keepdims=True)) + lambda_init)
    o_t = acc_sc[0] * (1.0 / l_sc[0]) - lam * (acc_sc[1] * (1.0 / l_sc[1]))
    o_ref[0] = (_rms_rows(o_t.T, sw_ref[...]) * (1.0 - lambda_init)).astype(o_ref.dtype)


def diff_attention(qk, v_t, lq1, lk1, lq2, lk2, subln_w, *, layer_idx, tq=512):
    B, S, _ = qk.shape
    tq = min(tq, S)
    lambda_init = 0.8 - 0.6 * math.exp(-0.3 * layer_idx)
    pw = 2 * A_HEAD_DIM
    vec = lambda a: a.reshape(1, -1).astype(F32)
    small = lambda n: pl.BlockSpec((1, n), lambda b, h, i: (0, 0))
    kern = functools.partial(_diff_attn_kernel, tq=tq, lambda_init=lambda_init)
    return pl.pallas_call(
        kern,
        grid=(B, A_HEADS, S // tq),
        in_specs=[
            pl.BlockSpec((1, tq, pw), lambda b, h, i: (b, i, h)),
            pl.BlockSpec((1, S, pw), lambda b, h, i: (b, 0, A_HEADS + h)),
            pl.BlockSpec((pw, S), lambda b, h, i: (h, b)),
            small(A_HEAD_DIM), small(A_HEAD_DIM), small(A_HEAD_DIM), small(A_HEAD_DIM), small(pw),
        ],
        out_specs=pl.BlockSpec((1, tq, pw), lambda b, h, i: (b, i, h)),
        out_shape=jax.ShapeDtypeStruct((B, S, D_MODEL), BF16),
        scratch_shapes=[
            pltpu.VMEM((2, 1, tq), F32),
            pltpu.VMEM((2, 1, tq), F32),
            pltpu.VMEM((2, pw, tq), F32),
            pltpu.VMEM((2, tq, tq), F32),
            pltpu.VMEM((2, tq, tq), F32),
        ],
        compiler_params=_params("parallel", "parallel", "arbitrary"),
        name="diff_attention",
    )(qk, qk, v_t, vec(lq1), vec(lk1), vec(lq2), vec(lk2), vec(subln_w))


def _retention_kernel(q_ref, k_ref, v_ref, g_ref, cos_ref, sin_ref, dm_ref, xi_ref, zeta_ref, cd_ref,
                      gw_ref, o_ref, r_sc, *, tr, chunk):
    @pl.when(pl.program_id(2) == 0)
    def _():
        r_sc[...] = jnp.zeros_like(r_sc)

    half = R_KDIM // 2
    dmask = dm_ref[0]
    xi = xi_ref[0]
    zeta = zeta_ref[0]
    cdecay = cd_ref[0, 0:1, 0:1]
    gw = gw_ref[0]

    def rotate(t, c, s):
        te, to = t[:, :half], t[:, half:]
        return te * c - to * s, to * c + te * s

    for ci in range(tr // chunk):
        rows = slice(ci * chunk, (ci + 1) * chunk)
        c = cos_ref[rows, :]
        s = sin_ref[rows, :]
        qe, qo = rotate(q_ref[0, rows, :].astype(F32), c, s)
        ke, ko = rotate(k_ref[0, rows, :].astype(F32) * (R_KDIM ** -0.5), c, s)
        v = v_ref[0, rows, :]
        q_r = jnp.concatenate([qe, qo], axis=1).astype(BF16)
        k_r = jnp.concatenate([ke, ko], axis=1).astype(BF16)
        q_x = jnp.concatenate([qe * xi, qo * xi], axis=1).astype(BF16)
        k_z = jnp.concatenate([ke * zeta, ko * zeta], axis=1).astype(BF16)
        r_old = r_sc[...]
        sc = lax.dot_general(q_r, k_r, NT_DIMS, preferred_element_type=F32) * dmask
        o = (jnp.dot(sc.astype(BF16), v, preferred_element_type=F32)
             + jnp.dot(q_x, r_old.astype(BF16), preferred_element_type=F32))
        r_sc[...] = r_old * cdecay + lax.dot_general(k_z, v, TN_DIMS, preferred_element_type=F32)
        gate = _silu(g_ref[0, rows, :].astype(F32))
        o_ref[0, rows, :] = (gate * _rms_rows(o, gw)).astype(o_ref.dtype)


def retention(proj, gn_w, *, tr=1024, chunk=256):
    B, S, _ = proj.shape
    tr = min(tr, S)
    chunk = min(chunk, tr)
    half = R_KDIM // 2
    angle = 1.0 / (10000.0 ** jnp.linspace(0.0, 1.0, half, dtype=F32))
    ang = jnp.arange(S, dtype=F32)[:, None] * angle[None, :]
    cos, sin = jnp.cos(ang), jnp.sin(ang)
    log_g = jnp.log(1.0 - 2.0 ** (-5.0 - jnp.arange(R_HEADS, dtype=F32)))
    idx = jnp.arange(chunk, dtype=F32)
    rel = idx[:, None] - idx[None, :]
    dmask = jnp.where(rel[None] >= 0, jnp.exp(jnp.maximum(rel, 0.0)[None] * log_g[:, None, None]), 0.0)
    xi = jnp.exp((idx + 1.0)[None, :] * log_g[:, None])
    zeta = jnp.exp((chunk - 1.0 - idx)[None, :] * log_g[:, None])
    cdecay = jnp.exp(chunk * log_g)
    bc = lambda t: jnp.broadcast_to(t[:, :, None], (R_HEADS, chunk, half))
    cd = jnp.broadcast_to(cdecay[:, None, None], (R_HEADS, 8, LANES))
    nq = D_MODEL // R_KDIM
    nv = 2 * D_MODEL // R_VDIM
    kern = functools.partial(_retention_kernel, tr=tr, chunk=chunk)
    head = lambda shape: pl.BlockSpec(shape, lambda b, h, i: (h, 0, 0))
    return pl.pallas_call(
        kern,
        grid=(B, R_HEADS, S // tr),
        in_specs=[
            pl.BlockSpec((1, tr, R_KDIM), lambda b, h, i: (b, i, h)),
            pl.BlockSpec((1, tr, R_KDIM), lambda b, h, i: (b, i, nq + h)),
            pl.BlockSpec((1, tr, R_VDIM), lambda b, h, i: (b, i, nv + h)),
            pl.BlockSpec((1, tr, R_VDIM), lambda b, h, i: (b, i, nv + R_HEADS + h)),
            pl.BlockSpec((tr, half), lambda b, h, i: (i, 0)),
            pl.BlockSpec((tr, half), lambda b, h, i: (i, 0)),
            head((1, chunk, chunk)), head((1, chunk, half)), head((1, chunk, half)),
            head((1, 8, LANES)), head((1, 1, R_VDIM)),
        ],
        out_specs=pl.BlockSpec((1, tr, R_VDIM), lambda b, h, i: (b, i, h)),
        out_shape=jax.ShapeDtypeStruct((B, S, R_HEADS * R_VDIM), BF16),
        scratch_shapes=[pltpu.VMEM((R_KDIM, R_VDIM), F32)],
        compiler_params=_params("parallel", "parallel", "arbitrary"),
        name="retention",
    )(proj, proj, proj, proj, cos, sin, dmask, bc(xi), bc(zeta), cd, gn_w.reshape(R_HEADS, 1, R_VDIM).astype(F32))


def _mamba_kernel(z_ref, x_ref, b_ref, c_ref, dt_ref, wx_ref, wb_ref, wc_ref, bx_ref, bb_ref, bc_ref,
                  dtb_ref, alog_ref, dsk_ref, nw_ref, o_ref,
                  st_sc, tx_sc, tb_sc, tc_sc, *, chunk):
    L = chunk

    @pl.when(pl.program_id(2) == 0)
    def _():
        st_sc[...] = jnp.zeros_like(st_sc)
        for ext_sc in (tx_sc, tb_sc, tc_sc):
            ext_sc[0:8, :] = jnp.zeros((8, ext_sc.shape[1]), F32)

    def conv_silu(cur_ref, ext_sc, w_ref, bias_ref):
        ext_sc[8:, :] = cur_ref[0].astype(F32)
        w = w_ref[...]
        acc = bias_ref[...] + ext_sc[8:, :] * w[M_CONV - 1:M_CONV, :]
        for kk in range(M_CONV - 1):
            sh = M_CONV - 1 - kk
            acc = acc + ext_sc[8 - sh:8 - sh + L, :] * w[kk:kk + 1, :]
        ext_sc[0:8, :] = ext_sc[L:, :]
        return _silu(acc)

    raw = dt_ref[0, 0] + dtb_ref[0]
    dt_t = jnp.maximum(raw, 0.0) + jnp.log(1.0 + jnp.exp(-jnp.abs(raw)))
    a_t = dt_t * (-jnp.exp(alog_ref[0])) * math.log2(math.e)

    ri = lax.broadcasted_iota(jnp.int32, (L, L), 0)
    ci = lax.broadcasted_iota(jnp.int32, (L, L), 1)
    tril = ci <= ri
    eye = jnp.where(ri == ci, 1.0, 0.0).astype(BF16)
    lower = jnp.where(tril, 1.0, 0.0).astype(BF16)
    upper = jnp.where(ri <= ci, 1.0, 0.0).astype(BF16)

    def per_column(t):
        return jnp.concatenate([jnp.broadcast_to(t[r:r + 1, :], (M_HEADDIM, L)) for r in range(M_HPG)], axis=0)

    a_parts = _split3(a_t)
    acs_row = sum(jnp.dot(p, upper, preferred_element_type=F32) for p in a_parts)

    def expand_dot(mat01, parts):
        return sum(lax.dot_general(mat01, per_column(p.astype(F32)).astype(BF16), NT_DIMS,
                                   preferred_element_type=F32) for p in parts)

    acs_x = expand_dot(lower, a_parts)
    dt_x = expand_dot(eye, _split3(dt_t))

    xs = conv_silu(x_ref, tx_sc, wx_ref, bx_ref)
    bm = conv_silu(b_ref, tb_sc, wb_ref, bb_ref)
    cm = conv_silu(c_ref, tc_sc, wc_ref, bc_ref)

    xdt = xs * dt_x
    cb = lax.dot_general(cm.astype(BF16), bm.astype(BF16), NT_DIMS, preferred_element_type=F32)
    xdt_b = xdt.astype(BF16)
    parts = []
    for r in range(M_HPG):
        diff = acs_x[:, r * M_HEADDIM:r * M_HEADDIM + 1] - acs_row[r:r + 1, :]
        lmat = jnp.exp2(jnp.where(tril, diff, NEG_BIG))
        mr = (cb * lmat).astype(BF16)
        parts.append(jnp.dot(mr, xdt_b[:, r * M_HEADDIM:(r + 1) * M_HEADDIM], preferred_element_type=F32))
    y = jnp.concatenate(parts, axis=1)

    state = st_sc[...]
    y = y + jnp.exp2(acs_x) * jnp.dot(cm.astype(BF16), state.astype(BF16), preferred_element_type=F32)
    last = acs_x[L - 1:L, :]
    decay_end = jnp.exp2(last - acs_x)
    st_sc[...] = state * jnp.exp2(last) + lax.dot_general(
        bm.astype(BF16), (xdt * decay_end).astype(BF16), TN_DIMS, preferred_element_type=F32)

    y = y + xs * dsk_ref[...]
    y = y * _silu(z_ref[0].astype(F32))
    o_ref[0] = _rms_rows(y, nw_ref[...]).astype(o_ref.dtype)


def mamba_ssd(zx, dt_raw, conv_w, conv_b, dt_bias, a_log, d_skip, norm_w, *, chunk=256):
    B, S, _ = zx.shape
    chunk = min(chunk, S)
    G, W, N = M_GROUPS, M_GROUP_W, M_DSTATE
    dt_t = dt_raw[:, :, :M_HEADS].reshape(B, S, G, M_HPG).transpose(0, 2, 3, 1)
    xoff = M_D_INNER // W
    boff = 2 * M_D_INNER // N
    coff = boff + G
    cwb = M_D_INNER // N
    conv_w = conv_w.astype(F32)
    conv_b = conv_b.reshape(1, -1).astype(F32)
    per_head = lambda t: t.reshape(G, M_HPG, 1).astype(F32)
    dsk_x = jnp.repeat(d_skip.astype(F32), M_HEADDIM).reshape(1, M_D_INNER)
    kern = functools.partial(_mamba_kernel, chunk=chunk)
    return pl.pallas_call(
        kern,
        grid=(B, G, S // chunk),
        in_specs=[
            pl.BlockSpec((1, chunk, W), lambda b, g, c: (b, c, g)),
            pl.BlockSpec((1, chunk, W), lambda b, g, c: (b, c, xoff + g)),
            pl.BlockSpec((1, chunk, N), lambda b, g, c: (b, c, boff + g)),
            pl.BlockSpec((1, chunk, N), lambda b, g, c: (b, c, coff + g)),
            pl.BlockSpec((1, 1, M_HPG, chunk), lambda b, g, c: (b, g, 0, c)),
            pl.BlockSpec((M_CONV, W), lambda b, g, c: (0, g)),
            pl.BlockSpec((M_CONV, N), lambda b, g, c: (0, cwb + g)),
            pl.BlockSpec((M_CONV, N), lambda b, g, c: (0, cwb + G + g)),
            pl.BlockSpec((1, W), lambda b, g, c: (0, g)),
            pl.BlockSpec((1, N), lambda b, g, c: (0, cwb + g)),
            pl.BlockSpec((1, N), lambda b, g, c: (0, cwb + G + g)),
            pl.BlockSpec((1, M_HPG, 1), lambda b, g, c: (g, 0, 0)),
            pl.BlockSpec((1, M_HPG, 1), lambda b, g, c: (g, 0, 0)),
            pl.BlockSpec((1, W), lambda b, g, c: (0, g)),
            pl.BlockSpec((1, W), lambda b, g, c: (0, g)),
        ],
        out_specs=pl.BlockSpec((1, chunk, W), lambda b, g, c: (b, c, g)),
        out_shape=jax.ShapeDtypeStruct((B, S, M_D_INNER), BF16),
        scratch_shapes=[
            pltpu.VMEM((N, W), F32),
            pltpu.VMEM((8 + chunk, W), F32),
            pltpu.VMEM((8 + chunk, N), F32),
            pltpu.VMEM((8 + chunk, N), F32),
        ],
        compiler_params=_params("parallel", "parallel", "arbitrary"),
        name="mamba_ssd",
    )(zx, zx, zx, zx, dt_t, conv_w, conv_w, conv_w, conv_b, conv_b, conv_b,
      per_head(dt_bias), per_head(a_log), dsk_x, norm_w.reshape(1, M_D_INNER).astype(F32))


def _window_attend(operands, mxu_sums):
    T = D_SPAN
    ri = lax.broadcasted_iota(jnp.int32, (T, T), 0)
    ci = lax.broadcasted_iota(jnp.int32, (T, T), 1)
    ones = jnp.ones((T, LANES), BF16)
    scores = [(lax.dot_general(q, kp, NT_DIMS, preferred_element_type=F32),
               lax.dot_general(q, kc, NT_DIMS, preferred_element_type=F32))
              for q, kp, kc, _, _, _ in operands]
    probs = []
    for (sp, sc), (_, _, _, _, _, has_prev) in zip(scores, operands):
        sp = jnp.where((ci >= ri) & has_prev, sp, NEG_BIG)
        sc = jnp.where(ci <= ri, sc, NEG_BIG)
        mx = jnp.maximum(jnp.max(sp, axis=-1, keepdims=True), jnp.max(sc, axis=-1, keepdims=True))
        pp = jnp.exp(sp - mx)
        pc = jnp.exp(sc - mx)
        l = None if mxu_sums else jnp.sum(pp, axis=-1, keepdims=True) + jnp.sum(pc, axis=-1, keepdims=True)
        probs.append((pp.astype(BF16), pc.astype(BF16), mx, l))
    outs = []
    for (pp, pc, mx, l), (_, _, _, vp, vc, _) in zip(probs, operands):
        o = jnp.dot(pp, vp, preferred_element_type=F32) + jnp.dot(pc, vc, preferred_element_type=F32)
        if mxu_sums:
            l = jnp.dot(pp, ones, preferred_element_type=F32) + jnp.dot(pc, ones, preferred_element_type=F32)
        outs.append((o * (1.0 / l), mx + jnp.log(l)))
    return outs


def _dense_window_kernel(q_ref, kp_ref, kc_ref, vp_ref, vc_ref, o_ref, lse_ref, lse_sc, *, tiles, unroll, mxu_sums):
    n = pl.program_id(1)
    h = pl.program_id(2)
    T = D_SPAN

    @pl.when(h == 0)
    def _():
        lse_sc[...] = jnp.zeros_like(lse_sc)

    lane = lax.broadcasted_iota(jnp.int32, (T, LANES), 1)
    for i0 in range(0, tiles, unroll):
        operands = []
        for i in range(i0, i0 + unroll):
            cur = slice(i * T, (i + 1) * T)
            prev = slice((i - 1) * T, i * T)
            kp = kp_ref[0] if i == 0 else kc_ref[0, prev, :]
            vp = vp_ref[0] if i == 0 else vc_ref[0, prev, :]
            operands.append((q_ref[0, cur, :], kp, kc_ref[0, cur, :], vp, vc_ref[0, cur, :],
                             jnp.logical_or(n > 0, i > 0)))
        for i, (o, lse) in zip(range(i0, i0 + unroll), _window_attend(operands, mxu_sums)):
            cur = slice(i * T, (i + 1) * T)
            o_ref[0, cur, :] = o.astype(o_ref.dtype)
            lse_sc[cur, :] = jnp.where(lane == h, lse, lse_sc[cur, :])

    @pl.when(h == D_HEADS - 1)
    def _():
        lse_ref[0] = lse_sc[...]


def _strided_window_kernel(q_ref, k_ref, v_ref, o_ref, lse_ref, lse_sc, q_sc, k_sc, v_sc, o_sc, kprev_sc, vprev_sc,
                           *, dil, tiles, unroll, mxu_sums):
    n = pl.program_id(1)
    h = pl.program_id(2)
    T = D_SPAN
    span = T * dil
    q_sc[...] = q_ref[0].astype(F32)
    k_sc[...] = k_ref[0].astype(F32)
    v_sc[...] = v_ref[0].astype(F32)

    @pl.when(h == 0)
    def _():
        lse_sc[...] = jnp.zeros_like(lse_sc)

    @pl.when(n == 0)
    def _():
        kprev_sc[h] = jnp.zeros(kprev_sc.shape[1:], BF16)
        vprev_sc[h] = jnp.zeros(vprev_sc.shape[1:], BF16)

    lane = lax.broadcasted_iota(jnp.int32, (T, LANES), 1)
    strided = lambda ref, start: ref[pl.ds(start, T, stride=dil), :]

    def group(tile_ids):
        operands, places = [], []
        for idx in tile_ids:
            i, r = (0, idx) if tiles == 1 else divmod(idx, dil)
            start = i * span + r
            saved = pl.ds(pl.multiple_of(r * T, T), T)
            kc = strided(k_sc, start).astype(BF16)
            vc = strided(v_sc, start).astype(BF16)
            if i == 0:
                kp, vp = kprev_sc[h, saved, :], vprev_sc[h, saved, :]
            else:
                kp = strided(k_sc, start - span).astype(BF16)
                vp = strided(v_sc, start - span).astype(BF16)
            operands.append((strided(q_sc, start).astype(BF16), kp, kc, vp, vc, jnp.logical_or(n > 0, i > 0)))
            places.append((start, saved if i == tiles - 1 else None))
        for (start, saved), (_, _, kc, _, vc, _), (o, lse) in zip(places, operands,
                                                                   _window_attend(operands, mxu_sums)):
            o_sc[pl.ds(start, T, stride=dil), :] = o
            lse_sc[pl.ds(start, T, stride=dil), :] = jnp.where(lane == h, lse, strided(lse_sc, start))
            if saved is not None:
                kprev_sc[h, saved, :] = kc
                vprev_sc[h, saved, :] = vc

    for j0 in range(0, dil * tiles, unroll):
        group(list(range(j0, j0 + unroll)))
    o_ref[0] = o_sc[...].astype(o_ref.dtype)

    @pl.when(h == D_HEADS - 1)
    def _():
        lse_ref[0] = lse_sc[...]


def dilated_group(qkv, g, dil, *, tiles, unroll, mxu_sums):
    B, S, C = qkv.shape
    assert (dil * tiles) % unroll == 0
    span = D_SPAN * dil
    tb = span * tiles
    hd = D_HEAD_DIM
    col = lambda t: (g * 3 + t) * D_HEADS
    cur = lambda t: pl.BlockSpec((1, tb, hd), lambda b, n, h: (b, n, col(t) + h))
    prev = lambda t: pl.BlockSpec((1, span, hd), lambda b, n, h: (b, jnp.maximum(n * tiles - 1, 0), col(t) + h))
    if dil == 1:
        kern = functools.partial(_dense_window_kernel, tiles=tiles, unroll=unroll, mxu_sums=mxu_sums)
        in_specs = [cur(0), prev(1), cur(1), prev(2), cur(2)]
        scratch = []
    else:
        kern = functools.partial(_strided_window_kernel, dil=dil, tiles=tiles, unroll=unroll, mxu_sums=mxu_sums)
        in_specs = [cur(0), cur(1), cur(2)]
        scratch = [pltpu.VMEM((tb, hd), F32)] * 4 + [pltpu.VMEM((D_HEADS, span, hd), BF16)] * 2
    o, lse = pl.pallas_call(
        kern,
        grid=(B, S // tb, D_HEADS),
        in_specs=in_specs,
        out_specs=[
            pl.BlockSpec((1, tb, hd), lambda b, n, h: (b, n, h)),
            pl.BlockSpec((1, tb, LANES), lambda b, n, h: (b, n, 0)),
        ],
        out_shape=[
            jax.ShapeDtypeStruct((B, S, D_HEADS * hd), BF16),
            jax.ShapeDtypeStruct((B, S, LANES), F32),
        ],
        scratch_shapes=[pltpu.VMEM((tb, LANES), F32)] + scratch,
        compiler_params=_params("parallel", "arbitrary", "arbitrary"),
        name="dilated_attention",
    )(*([qkv] * len(in_specs)))
    return o.reshape(B * S, D_HEADS * hd), lse.reshape(B * S, LANES)


def _combine_kernel(o0_ref, o1_ref, o2_ref, l0_ref, l1_ref, l2_ref, o_ref):
    l0, l1, l2 = l0_ref[...], l1_ref[...], l2_ref[...]
    mx = jnp.maximum(jnp.maximum(l0, l1), l2)
    e0, e1, e2 = jnp.exp(l0 - mx), jnp.exp(l1 - mx), jnp.exp(l2 - mx)
    inv = 1.0 / (e0 + e1 + e2)
    w0, w1 = e0 * inv, e1 * inv
    for h in range(D_HEADS):
        hs = slice(h * D_HEAD_DIM, (h + 1) * D_HEAD_DIM)
        o2 = o2_ref[:, hs].astype(F32)
        acc = o2 + w0[:, h:h + 1] * (o0_ref[:, hs].astype(F32) - o2)
        acc = acc + w1[:, h:h + 1] * (o1_ref[:, hs].astype(F32) - o2)
        o_ref[:, hs] = acc.astype(o_ref.dtype)


def combine_groups(outs, lses, *, tm=512):
    T, W = outs[0].shape
    tm = min(tm, T)
    wide = pl.BlockSpec((tm, W), lambda i: (i, 0))
    narrow = pl.BlockSpec((tm, LANES), lambda i: (i, 0))
    return pl.pallas_call(
        _combine_kernel,
        grid=(T // tm,),
        in_specs=[wide, wide, wide, narrow, narrow, narrow],
        out_specs=wide,
        out_shape=jax.ShapeDtypeStruct((T, W), BF16),
        compiler_params=_params("parallel"),
        name="combine_groups",
    )(*outs, *lses)


def _deinterleave_heads(w, heads, dim):
    k = w.shape[0]
    return w.reshape(k, heads, dim // 2, 2).transpose(0, 1, 3, 2).reshape(k, heads * dim)


def mixer_a(xr, B, S, nw, a_w_in, a_q_norm_w, a_k_norm_w, lq1, lk1, lq2, lk2, a_subln_w, a_w_out, *, layer_idx):
    T, D = xr.shape
    colw = jnp.concatenate([
        jnp.tile(a_q_norm_w.astype(F32) * (A_HEAD_DIM ** -0.5 * math.log2(math.e)), 2 * A_HEADS),
        jnp.tile(a_k_norm_w.astype(F32), 2 * A_HEADS)]).reshape(1, 2 * D)
    qk = norm_matmul(xr, nw, a_w_in[:, :2 * D].astype(BF16), colw)
    v_t = norm_matmul_t(xr, nw, a_w_in[:, 2 * D:].T.astype(BF16))
    o = diff_attention(qk.reshape(B, S, 2 * D), v_t, lq1, lk1, lq2, lk2, a_subln_w, layer_idx=layer_idx)
    return matmul_residual(o.reshape(T, D), a_w_out.astype(BF16), xr)


def mixer_b(xr, B, S, nw, b_w_in, b_gn_w, b_w_out):
    T, D = xr.shape
    w_in = jnp.concatenate([
        _deinterleave_heads(b_w_in[:, :D], R_HEADS, R_KDIM),
        _deinterleave_heads(b_w_in[:, D:2 * D], R_HEADS, R_KDIM),
        b_w_in[:, 2 * D:]], axis=1)
    proj = norm_matmul(xr, nw, w_in.astype(BF16))
    o = retention(proj.reshape(B, S, -1), b_gn_w)
    return matmul_residual(o.reshape(T, -1), b_w_out.astype(BF16), xr)


def mixer_c(xr, B, S, nw, c_w_in, c_conv_w, c_conv_b, c_dt_bias, c_a_log, c_d_skip, c_norm_w, c_w_out):
    T, D = xr.shape
    n_main = 2 * M_D_INNER + 2 * M_GROUPS * M_DSTATE
    w_dt = jnp.pad(c_w_in[:, n_main:], ((0, 0), (0, LANES - M_HEADS)))
    zx, dt_raw = norm_matmul(xr, nw, c_w_in[:, :n_main].astype(BF16), side_w=w_dt.astype(BF16))
    y = mamba_ssd(zx.reshape(B, S, n_main), dt_raw.reshape(B, S, LANES), c_conv_w, c_conv_b, c_dt_bias,
                  c_a_log, c_d_skip, c_norm_w)
    return matmul_residual(y.reshape(T, M_D_INNER), c_w_out.astype(BF16), xr)


def mixer_d(xr, B, S, nw, d_w_in, d_q_norm_w, d_k_norm_w, d_w_out):
    T, D = xr.shape
    scale = D_HEAD_DIM ** -0.5
    colw = jnp.concatenate([
        jnp.concatenate([jnp.tile(d_q_norm_w[g].astype(F32) * scale, D_HEADS),
                         jnp.tile(d_k_norm_w[g].astype(F32), D_HEADS),
                         jnp.ones((D,), F32)])
        for g in range(len(D_PATTERNS))]).reshape(1, -1)
    qkv = norm_matmul(xr, nw, d_w_in.astype(BF16), colw).reshape(B, S, -1)
    outs, lses = zip(*[dilated_group(qkv, g, dil, tiles=max(1, 8 // dil), unroll=8 if dil < 16 else 4,
                                     mxu_sums=dil < 16)
                       for g, (_, dil) in enumerate(D_PATTERNS)])
    o = combine_groups(outs, lses)
    return matmul_residual(o, d_w_out.astype(BF16), xr)


@jax.jit
def kernel(x, norm1_w, norm2_w, mlp_w1, mlp_w2, a_w_in, a_q_norm_w, a_k_norm_w, a_lambda_q1, a_lambda_k1, a_lambda_q2, a_lambda_k2, a_subln_w, a_w_out, b_w_in, b_gn_w, b_w_out, c_w_in, c_conv_w, c_conv_b, c_dt_bias, c_a_log, c_d_skip, c_norm_w, c_w_out, d_w_in, d_q_norm_w, d_k_norm_w, d_w_out):
    B, S, D = x.shape
    xr = x.reshape(B * S, D)
    w1, w2 = mlp_w1.astype(BF16), mlp_w2.astype(BF16)
    ffn = lambda t, i: mlp(t, norm2_w[i], w1, w2, i)
    xr = mixer_a(xr, B, S, norm1_w[0], a_w_in, a_q_norm_w, a_k_norm_w, a_lambda_q1, a_lambda_k1,
                 a_lambda_q2, a_lambda_k2, a_subln_w, a_w_out, layer_idx=0)
    xr = ffn(xr, 0)
    xr = mixer_b(xr, B, S, norm1_w[1], b_w_in, b_gn_w, b_w_out)
    xr = ffn(xr, 1)
    xr = mixer_c(xr, B, S, norm1_w[2], c_w_in, c_conv_w, c_conv_b, c_dt_bias, c_a_log, c_d_skip, c_norm_w,
                 c_w_out)
    xr = ffn(xr, 2)
    xr = mixer_d(xr, B, S, norm1_w[3], d_w_in, d_q_norm_w, d_k_norm_w, d_w_out)
    xr = ffn(xr, 3)
    return xr.reshape(B, S, D)
```

```python
import functools
import math

import jax
import jax.numpy as jnp
from jax import lax
from jax.experimental import pallas as pl
from jax.experimental.pallas import tpu as pltpu

F32 = jnp.float32
BF16 = jnp.bfloat16

EPS = 1e-6
D_MODEL = 2048
D_FF = 4 * D_MODEL
LANES = 128
MXU_COLS = 256
ROW_BLOCK = 256

A_HEAD_DIM = 128
A_HEADS = 8
R_HEADS = 8
R_KDIM = 256
R_VDIM = 512
M_D_INNER = 4096
M_HEADDIM = 64
M_HEADS = 64
M_GROUPS = 8
M_DSTATE = 128
M_CONV = 4
M_GROUP_W = M_D_INNER // M_GROUPS
M_HPG = M_HEADS // M_GROUPS
D_HEADS = 16
D_HEAD_DIM = 128
D_PATTERNS = ((128, 1), (512, 4), (2048, 16))
D_SPAN = 128

VMEM_LIMIT_BYTES = 56 * 1024 * 1024
NEG_BIG = -1e30

NT_DIMS = (((1,), (1,)), ((), ()))
TN_DIMS = (((0,), (0,)), ((), ()))


def _params(*sem):
    return pltpu.CompilerParams(dimension_semantics=sem, vmem_limit_bytes=VMEM_LIMIT_BYTES)


def _silu(v):
    h = 0.5 * v
    return h + h * jnp.tanh(h)


def _rms_rows(v, w):
    ms = jnp.mean(v * v, axis=-1, keepdims=True)
    return v * lax.rsqrt(ms + EPS) * w


def _split3(v):
    p1 = v.astype(BF16)
    r1 = v - p1.astype(F32)
    p2 = r1.astype(BF16)
    return p1, p2, (r1 - p2.astype(F32)).astype(BF16)


def _row_blocks(tm):
    rb = min(tm, ROW_BLOCK)
    return [slice(r, r + rb) for r in range(0, tm, rb)]


def _norm_matmul_kernel(x_ref, nw_ref, w_ref, cw_ref, *rest, tm, tn, head_norm, period, count):
    if len(rest) == 4:
        ws_ref, o_ref, side_ref, h_sc = rest
    else:
        (o_ref, h_sc), ws_ref, side_ref = rest, None, None
    j = pl.program_id(1)
    every = slice(0, tm)

    def plain(rows):
        o_ref[rows, :] = jnp.dot(h_sc[rows, :], w_ref[...], preferred_element_type=F32).astype(o_ref.dtype)

    def normed(rows):
        sub = min(tn, MXU_COLS)
        for c in range(tn // sub):
            acc = jnp.dot(h_sc[rows, :], w_ref[:, c * sub:(c + 1) * sub], preferred_element_type=F32)
            for d in range(sub // LANES):
                sl = slice(c * sub + d * LANES, c * sub + (d + 1) * LANES)
                o_ref[rows, sl] = _rms_rows(acc[:, d * LANES:(d + 1) * LANES], cw_ref[:, sl]).astype(o_ref.dtype)

    @pl.when(j == 0)
    def _():
        for rows in _row_blocks(tm):
            h_sc[rows, :] = _rms_rows(x_ref[rows, :], nw_ref[...]).astype(BF16)
            (normed if head_norm else plain)(rows)
            if side_ref is not None:
                side_ref[rows, :] = jnp.dot(h_sc[rows, :], ws_ref[...], preferred_element_type=F32)

    if not head_norm:
        pl.when(j > 0)(functools.partial(plain, every))
        return
    is_norm = (j % period) < count
    pl.when(jnp.logical_and(j > 0, is_norm))(functools.partial(normed, every))
    pl.when(jnp.logical_not(is_norm))(functools.partial(plain, every))


def norm_matmul(x, nw, w, colw=None, side_w=None, *, tm=1024, tn=1024):
    T, K = x.shape
    N = w.shape[1]
    tm = min(tm, T)
    tn = min(tn, N)
    head_norm = colw is not None
    if colw is None:
        colw = jnp.ones((1, N), F32)
    period = 3 * D_MODEL // tn
    count = 2 * D_MODEL // tn
    kern = functools.partial(_norm_matmul_kernel, tm=tm, tn=tn, head_norm=head_norm, period=period, count=count)
    in_specs = [
        pl.BlockSpec((tm, K), lambda i, j: (i, 0)),
        pl.BlockSpec((1, K), lambda i, j: (0, 0)),
        pl.BlockSpec((K, tn), lambda i, j: (0, j)),
        pl.BlockSpec((1, tn), lambda i, j: (0, j)),
    ]
    out_specs = pl.BlockSpec((tm, tn), lambda i, j: (i, j))
    out_shape = jax.ShapeDtypeStruct((T, N), BF16)
    args = (x, nw.reshape(1, K), w, colw)
    if side_w is not None:
        in_specs.append(pl.BlockSpec((K, LANES), lambda i, j: (0, 0)))
        out_specs = [out_specs, pl.BlockSpec((tm, LANES), lambda i, j: (i, 0))]
        out_shape = [out_shape, jax.ShapeDtypeStruct((T, LANES), F32)]
        args += (side_w,)
    return pl.pallas_call(
        kern,
        grid=(T // tm, N // tn),
        in_specs=in_specs,
        out_specs=out_specs,
        out_shape=out_shape,
        scratch_shapes=[pltpu.VMEM((tm, K), BF16)],
        compiler_params=_params("parallel", "arbitrary"),
        name="norm_matmul",
    )(*args)


def _norm_matmul_t_kernel(x_ref, nw_ref, wt_ref, o_ref, h_sc, *, tm):
    j = pl.program_id(1)

    def emit(rows):
        o_ref[:, rows] = lax.dot_general(wt_ref[...], h_sc[rows, :], NT_DIMS,
                                         preferred_element_type=F32).astype(o_ref.dtype)

    @pl.when(j == 0)
    def _():
        for rows in _row_blocks(tm):
            h_sc[rows, :] = _rms_rows(x_ref[rows, :], nw_ref[...]).astype(BF16)
            emit(rows)

    pl.when(j > 0)(functools.partial(emit, slice(0, tm)))


def norm_matmul_t(x, nw, w_t, *, tm=1024, tn=1024):
    T, K = x.shape
    N = w_t.shape[0]
    tm = min(tm, T)
    return pl.pallas_call(
        functools.partial(_norm_matmul_t_kernel, tm=tm),
        grid=(T // tm, N // tn),
        in_specs=[
            pl.BlockSpec((tm, K), lambda i, j: (i, 0)),
            pl.BlockSpec((1, K), lambda i, j: (0, 0)),
            pl.BlockSpec((tn, K), lambda i, j: (j, 0)),
        ],
        out_specs=pl.BlockSpec((tn, tm), lambda i, j: (j, i)),
        out_shape=jax.ShapeDtypeStruct((N, T), BF16),
        scratch_shapes=[pltpu.VMEM((tm, K), BF16)],
        compiler_params=_params("parallel", "arbitrary"),
        name="norm_matmul_t",
    )(x, nw.reshape(1, K), w_t)


def _matmul_residual_kernel(a_ref, w_ref, r_ref, o_ref):
    o_ref[...] = r_ref[...] + jnp.dot(a_ref[...], w_ref[...], preferred_element_type=F32)


def matmul_residual(a, w, res, *, tm=1024, tn=1024):
    T, K = a.shape
    N = w.shape[1]
    tm = min(tm, T)
    return pl.pallas_call(
        _matmul_residual_kernel,
        grid=(T // tm, N // tn),
        in_specs=[
            pl.BlockSpec((tm, K), lambda i, j: (i, 0)),
            pl.BlockSpec((K, tn), lambda i, j: (0, j)),
            pl.BlockSpec((tm, tn), lambda i, j: (i, j)),
        ],
        out_specs=pl.BlockSpec((tm, tn), lambda i, j: (i, j)),
        out_shape=jax.ShapeDtypeStruct((T, N), F32),
        compiler_params=_params("parallel", "arbitrary"),
        name="matmul_residual",
    )(a, w, res)


def _mlp_kernel(x_ref, nw_ref, w1_ref, w2_ref, o_ref, h_sc, *, tm):
    f = pl.program_id(1)

    def accumulate(rows):
        u = jnp.dot(h_sc[rows, :], w1_ref[0], preferred_element_type=F32)
        u = jnp.square(jnp.maximum(u, 0.0)).astype(BF16)
        o_ref[rows, :] += jnp.dot(u, w2_ref[0], preferred_element_type=F32)

    @pl.when(f == 0)
    def _():
        for rows in _row_blocks(tm):
            x = x_ref[rows, :]
            h_sc[rows, :] = _rms_rows(x, nw_ref[...]).astype(BF16)
            o_ref[rows, :] = x
            accumulate(rows)

    pl.when(f > 0)(functools.partial(accumulate, slice(0, tm)))


def mlp(x, nw, w1, w2, layer, *, tm=512, tf=1024):
    T, D = x.shape
    FF = w1.shape[2]
    tm = min(tm, T)
    return pl.pallas_call(
        functools.partial(_mlp_kernel, tm=tm),
        grid=(T // tm, FF // tf),
        in_specs=[
            pl.BlockSpec((tm, D), lambda i, f: (i, 0)),
            pl.BlockSpec((1, D), lambda i, f: (0, 0)),
            pl.BlockSpec((1, D, tf), lambda i, f: (layer, 0, f)),
            pl.BlockSpec((1, tf, D), lambda i, f: (layer, f, 0)),
        ],
        out_specs=pl.BlockSpec((tm, D), lambda i, f: (i, 0)),
        out_shape=jax.ShapeDtypeStruct((T, D), F32),
        scratch_shapes=[pltpu.VMEM((tm, D), BF16)],
        compiler_params=_params("parallel", "arbitrary"),
        name="mlp",
    )(x, nw.reshape(1, D), w1, w2)


def _diff_attn_kernel(q_ref, k_ref, vt_ref, lq1_ref, lk1_ref, lq2_ref, lk2_ref, sw_ref, o_ref,
                      m_sc, l_sc, acc_sc, sa_sc, sb_sc, *, tq, lambda_init):
    qi = pl.program_id(2)
    m_sc[...] = jnp.full_like(m_sc, NEG_BIG)
    l_sc[...] = jnp.zeros_like(l_sc)
    acc_sc[...] = jnp.zeros_like(acc_sc)
    key = lax.broadcasted_iota(jnp.int32, (tq, tq), 0)
    qry = lax.broadcasted_iota(jnp.int32, (tq, tq), 1)

    def scores(m, start):
        hs = slice(m * A_HEAD_DIM, (m + 1) * A_HEAD_DIM)
        return lax.dot_general(k_ref[0, pl.ds(start, tq), hs], q_ref[0, :, hs], NT_DIMS,
                               preferred_element_type=F32)

    def update(m, s, start, masked):
        if masked:
            s = jnp.where(key <= qry, s, NEG_BIG)
        m_prev = m_sc[m]
        m_new = jnp.maximum(m_prev, jnp.max(s, axis=0, keepdims=True))
        alpha = jnp.exp2(m_prev - m_new)
        p = jnp.exp2(s - m_new)
        m_sc[m] = m_new
        l_sc[m] = alpha * l_sc[m] + jnp.sum(p, axis=0, keepdims=True)
        acc_sc[m] = alpha * acc_sc[m] + jnp.dot(vt_ref[:, pl.ds(start, tq)], p.astype(BF16),
                                                preferred_element_type=F32)

    def put_scores(s_sc, j):
        start = pl.multiple_of(j * tq, tq)
        s_sc[0] = scores(0, start)
        s_sc[1] = scores(1, start)

    def updates(s_sc, j, masked):
        start = pl.multiple_of(j * tq, tq)
        update(0, s_sc[0], start, masked)
        update(1, s_sc[1], start, masked)

    put_scores(sa_sc, 0)

    def body(jj, carry):
        j = 2 * jj
        put_scores(sb_sc, j + 1)
        updates(sa_sc, j, False)
        put_scores(sa_sc, j + 2)
        updates(sb_sc, j + 1, False)
        return carry

    lax.fori_loop(0, qi // 2, body, 0)

    @pl.when(qi % 2 == 0)
    def _():
        updates(sa_sc, qi, True)

    @pl.when(qi % 2 == 1)
    def _():
        put_scores(sb_sc, qi)
        updates(sa_sc, qi - 1, False)
        updates(sb_sc, qi, True)

    lam = (jnp.exp(jnp.sum(lq1_ref[...] * lk1_ref[...], axis=-1, keepdims=True))
           - jnp.exp(jnp.sum(lq2_ref[...] * lk2_ref[...], axis=-1, keepdims=True)) + lambda_init)
    o_t = acc_sc[0] * (1.0 / l_sc[0]) - lam * (acc_sc[1] * (1.0 / l_sc[1]))
    o_ref[0] = (_rms_rows(o_t.T, sw_ref[...]) * (1.0 - lambda_init)).astype(o_ref.dtype)


def diff_attention(qk, v_t, lq1, lk1, lq2, lk2, subln_w, *, layer_idx, tq=512):
    B, S, _ = qk.shape
    tq = min(tq, S)
    lambda_init = 0.8 - 0.6 * math.exp(-0.3 * layer_idx)
    pw = 2 * A_HEAD_DIM
    vec = lambda a: a.reshape(1, -1).astype(F32)
    small = lambda n: pl.BlockSpec((1, n), lambda b, h, i: (0, 0))
    kern = functools.partial(_diff_attn_kernel, tq=tq, lambda_init=lambda_init)
    return pl.pallas_call(
        kern,
        grid=(B, A_HEADS, S // tq),
        in_specs=[
            pl.BlockSpec((1, tq, pw), lambda b, h, i: (b, i, h)),
            pl.BlockSpec((1, S, pw), lambda b, h, i: (b, 0, A_HEADS + h)),
            pl.BlockSpec((pw, S), lambda b, h, i: (h, b)),
            small(A_HEAD_DIM), small(A_HEAD_DIM), small(A_HEAD_DIM), small(A_HEAD_DIM), small(pw),
        ],
        out_specs=pl.BlockSpec((1, tq, pw), lambda b, h, i: (b, i, h)),
        out_shape=jax.ShapeDtypeStruct((B, S, D_MODEL), BF16),
        scratch_shapes=[
            pltpu.VMEM((2, 1, tq), F32),
            pltpu.VMEM((2, 1, tq), F32),
            pltpu.VMEM((2, pw, tq), F32),
            pltpu.VMEM((2, tq, tq), F32),
            pltpu.VMEM((2, tq, tq), F32),
        ],
        compiler_params=_params("parallel", "parallel", "arbitrary"),
        name="diff_attention",
    )(qk, qk, v_t, vec(lq1), vec(lk1), vec(lq2), vec(lk2), vec(subln_w))


def _retention_kernel(q_ref, k_ref, v_ref, g_ref, cos_ref, sin_ref, dm_ref, xi_ref, zeta_ref, cd_ref,
                      gw_ref, o_ref, r_sc, *, tr, chunk):
    @pl.when(pl.program_id(2) == 0)
    def _():
        r_sc[...] = jnp.zeros_like(r_sc)

    half = R_KDIM // 2
    dmask = dm_ref[0]
    xi = xi_ref[0]
    zeta = zeta_ref[0]
    cdecay = cd_ref[0, 0:1, 0:1]
    gw = gw_ref[0]

    def rotate(t, c, s):
        te, to = t[:, :half], t[:, half:]
        return te * c - to * s, to * c + te * s

    for ci in range(tr // chunk):
        rows = slice(ci * chunk, (ci + 1) * chunk)
        c = cos_ref[rows, :]
        s = sin_ref[rows, :]
        qe, qo = rotate(q_ref[0, rows, :].astype(F32), c, s)
        ke, ko = rotate(k_ref[0, rows, :].astype(F32) * (R_KDIM ** -0.5), c, s)
        v = v_ref[0, rows, :]
        q_r = jnp.concatenate([qe, qo], axis=1).astype(BF16)
        k_r = jnp.concatenate([ke, ko], axis=1).astype(BF16)
        q_x = jnp.concatenate([qe * xi, qo * xi], axis=1).astype(BF16)
        k_z = jnp.concatenate([ke * zeta, ko * zeta], axis=1).astype(BF16)
        r_old = r_sc[...]
        sc = lax.dot_general(q_r, k_r, NT_DIMS, preferred_element_type=F32) * dmask
        o = (jnp.dot(sc.astype(BF16), v, preferred_element_type=F32)
             + jnp.dot(q_x, r_old.astype(BF16), preferred_element_type=F32))
        r_sc[...] = r_old * cdecay + lax.dot_general(k_z, v, TN_DIMS, preferred_element_type=F32)
        gate = _silu(g_ref[0, rows, :].astype(F32))
        o_ref[0, rows, :] = (gate * _rms_rows(o, gw)).astype(o_ref.dtype)


def retention(proj, gn_w, *, tr=1024, chunk=256):
    B, S, _ = proj.shape
    tr = min(tr, S)
    chunk = min(chunk, tr)
    half = R_KDIM // 2
    angle = 1.0 / (10000.0 ** jnp.linspace(0.0, 1.0, half, dtype=F32))
    ang = jnp.arange(S, dtype=F32)[:, None] * angle[None, :]
    cos, sin = jnp.cos(ang), jnp.sin(ang)
    log_g = jnp.log(1.0 - 2.0 ** (-5.0 - jnp.arange(R_HEADS, dtype=F32)))
    idx = jnp.arange(chunk, dtype=F32)
    rel = idx[:, None] - idx[None, :]
    dmask = jnp.where(rel[None] >= 0, jnp.exp(jnp.maximum(rel, 0.0)[None] * log_g[:, None, None]), 0.0)
    xi = jnp.exp((idx + 1.0)[None, :] * log_g[:, None])
    zeta = jnp.exp((chunk - 1.0 - idx)[None, :] * log_g[:, None])
    cdecay = jnp.exp(chunk * log_g)
    bc = lambda t: jnp.broadcast_to(t[:, :, None], (R_HEADS, chunk, half))
    cd = jnp.broadcast_to(cdecay[:, None, None], (R_HEADS, 8, LANES))
    nq = D_MODEL // R_KDIM
    nv = 2 * D_MODEL // R_VDIM
    kern = functools.partial(_retention_kernel, tr=tr, chunk=chunk)
    head = lambda shape: pl.BlockSpec(shape, lambda b, h, i: (h, 0, 0))
    return pl.pallas_call(
        kern,
        grid=(B, R_HEADS, S // tr),
        in_specs=[
            pl.BlockSpec((1, tr, R_KDIM), lambda b, h, i: (b, i, h)),
            pl.BlockSpec((1, tr, R_KDIM), lambda b, h, i: (b, i, nq + h)),
            pl.BlockSpec((1, tr, R_VDIM), lambda b, h, i: (b, i, nv + h)),
            pl.BlockSpec((1, tr, R_VDIM), lambda b, h, i: (b, i, nv + R_HEADS + h)),
            pl.BlockSpec((tr, half), lambda b, h, i: (i, 0)),
            pl.BlockSpec((tr, half), lambda b, h, i: (i, 0)),
            head((1, chunk, chunk)), head((1, chunk, half)), head((1, chunk, half)),
            head((1, 8, LANES)), head((1, 1, R_VDIM)),
        ],
        out_specs=pl.BlockSpec((1, tr, R_VDIM), lambda b, h, i: (b, i, h)),
        out_shape=jax.ShapeDtypeStruct((B, S, R_HEADS * R_VDIM), BF16),
        scratch_shapes=[pltpu.VMEM((R_KDIM, R_VDIM), F32)],
        compiler_params=_params("parallel", "parallel", "arbitrary"),
        name="retention",
    )(proj, proj, proj, proj, cos, sin, dmask, bc(xi), bc(zeta), cd, gn_w.reshape(R_HEADS, 1, R_VDIM).astype(F32))


def _mamba_kernel(z_ref, x_ref, b_ref, c_ref, dt_ref, tri_ref, conv_ref, cols_ref, heads_ref, o_ref,
                  st_sc, tx_sc, tb_sc, tc_sc, *, chunk):
    L = chunk
    W, N = M_GROUP_W, M_DSTATE
    x_cols, b_cols, c_cols = slice(0, W), slice(W, W + N), slice(W + N, W + 2 * N)

    @pl.when(pl.program_id(2) == 0)
    def _():
        st_sc[...] = jnp.zeros_like(st_sc)
        for ext_sc in (tx_sc, tb_sc, tc_sc):
            ext_sc[0:8, :] = jnp.zeros((8, ext_sc.shape[1]), F32)

    def conv_silu(cur_ref, ext_sc, cols):
        ext_sc[8:, :] = cur_ref[0].astype(F32)
        w = conv_ref[0, 0:M_CONV, cols]
        acc = conv_ref[0, M_CONV:M_CONV + 1, cols] + ext_sc[8:, :] * w[M_CONV - 1:M_CONV, :]
        for kk in range(M_CONV - 1):
            sh = M_CONV - 1 - kk
            acc = acc + ext_sc[8 - sh:8 - sh + L, :] * w[kk:kk + 1, :]
        ext_sc[0:8, :] = ext_sc[L:, :]
        return _silu(acc)

    raw = dt_ref[0, 0] + heads_ref[0, :, 0:1]
    dt_t = jnp.maximum(raw, 0.0) + jnp.log(1.0 + jnp.exp(-jnp.abs(raw)))
    a_t = dt_t * (-jnp.exp(heads_ref[0, :, 1:2])) * math.log2(math.e)

    tril = lax.broadcasted_iota(jnp.int32, (L, L), 1) <= lax.broadcasted_iota(jnp.int32, (L, L), 0)
    eye, lower, upper = tri_ref[0], tri_ref[1], tri_ref[2]

    def per_column(t):
        return jnp.concatenate([jnp.broadcast_to(t[r:r + 1, :], (M_HEADDIM, L)) for r in range(M_HPG)], axis=0)

    a_parts = _split3(a_t)
    acs_row = sum(jnp.dot(p, upper, preferred_element_type=F32) for p in a_parts)

    def expand_dot(mat01, parts):
        return sum(lax.dot_general(mat01, per_column(p.astype(F32)).astype(BF16), NT_DIMS,
                                   preferred_element_type=F32) for p in parts)

    acs_x = expand_dot(lower, a_parts)
    dt_x = expand_dot(eye, _split3(dt_t))

    xs = conv_silu(x_ref, tx_sc, x_cols)
    bm = conv_silu(b_ref, tb_sc, b_cols)
    cm = conv_silu(c_ref, tc_sc, c_cols)

    xdt = xs * dt_x
    cb = lax.dot_general(cm.astype(BF16), bm.astype(BF16), NT_DIMS, preferred_element_type=F32)
    xdt_b = xdt.astype(BF16)
    parts = []
    for r in range(M_HPG):
        diff = acs_x[:, r * M_HEADDIM:r * M_HEADDIM + 1] - acs_row[r:r + 1, :]
        lmat = jnp.exp2(jnp.where(tril, diff, NEG_BIG))
        mr = (cb * lmat).astype(BF16)
        parts.append(jnp.dot(mr, xdt_b[:, r * M_HEADDIM:(r + 1) * M_HEADDIM], preferred_element_type=F32))
    y = jnp.concatenate(parts, axis=1)

    state = st_sc[...]
    y = y + jnp.exp2(acs_x) * jnp.dot(cm.astype(BF16), state.astype(BF16), preferred_element_type=F32)
    last = acs_x[L - 1:L, :]
    decay_end = jnp.exp2(last - acs_x)
    st_sc[...] = state * jnp.exp2(last) + lax.dot_general(
        bm.astype(BF16), (xdt * decay_end).astype(BF16), TN_DIMS, preferred_element_type=F32)

    y = y + xs * cols_ref[0, 0:1, :]
    y = y * _silu(z_ref[0].astype(F32))
    o_ref[0] = _rms_rows(y, cols_ref[0, 1:2, :]).astype(o_ref.dtype)


def mamba_ssd(zx, dt_raw, conv_w, conv_b, dt_bias, a_log, d_skip, norm_w, *, chunk=256):
    B, S, _ = zx.shape
    chunk = min(chunk, S)
    G, W, N = M_GROUPS, M_GROUP_W, M_DSTATE
    dt_t = dt_raw[:, :, :M_HEADS].reshape(B, S, G, M_HPG).transpose(0, 2, 3, 1)
    xoff = M_D_INNER // W
    boff = 2 * M_D_INNER // N
    coff = boff + G
    taps = jnp.concatenate([conv_w.astype(F32), conv_b.reshape(1, -1).astype(F32)], axis=0)
    split = lambda lo, width: taps[:, lo:lo + G * width].reshape(M_CONV + 1, G, width)
    conv_p = jnp.concatenate([split(0, W), split(M_D_INNER, N), split(M_D_INNER + G * N, N)], axis=2)
    conv_p = jnp.pad(conv_p.transpose(1, 0, 2), ((0, 0), (0, 8 - (M_CONV + 1)), (0, 0)))
    cols_p = jnp.stack([jnp.repeat(d_skip.astype(F32), M_HEADDIM), norm_w.astype(F32)])
    cols_p = jnp.pad(cols_p.reshape(2, G, W).transpose(1, 0, 2), ((0, 0), (0, 6), (0, 0)))
    heads_p = jnp.stack([dt_bias.astype(F32), a_log.astype(F32)], axis=-1).reshape(G, M_HPG, 2)
    heads_p = jnp.pad(heads_p, ((0, 0), (0, 0), (0, LANES - 2)))
    idx = jnp.arange(chunk)
    tri = jnp.stack([idx[:, None] == idx[None, :], idx[:, None] >= idx[None, :],
                     idx[:, None] <= idx[None, :]]).astype(BF16)
    group = lambda shape: pl.BlockSpec(shape, lambda b, g, c: (g, 0, 0))
    kern = functools.partial(_mamba_kernel, chunk=chunk)
    return pl.pallas_call(
        kern,
        grid=(B, G, S // chunk),
        in_specs=[
            pl.BlockSpec((1, chunk, W), lambda b, g, c: (b, c, g)),
            pl.BlockSpec((1, chunk, W), lambda b, g, c: (b, c, xoff + g)),
            pl.BlockSpec((1, chunk, N), lambda b, g, c: (b, c, boff + g)),
            pl.BlockSpec((1, chunk, N), lambda b, g, c: (b, c, coff + g)),
            pl.BlockSpec((1, 1, M_HPG, chunk), lambda b, g, c: (b, g, 0, c)),
            pl.BlockSpec((3, chunk, chunk), lambda b, g, c: (0, 0, 0)),
            group((1, 8, W + 2 * N)), group((1, 8, W)), group((1, 8, LANES)),
        ],
        out_specs=pl.BlockSpec((1, chunk, W), lambda b, g, c: (b, c, g)),
        out_shape=jax.ShapeDtypeStruct((B, S, M_D_INNER), BF16),
        scratch_shapes=[
            pltpu.VMEM((N, W), F32),
            pltpu.VMEM((8 + chunk, W), F32),
            pltpu.VMEM((8 + chunk, N), F32),
            pltpu.VMEM((8 + chunk, N), F32),
        ],
        compiler_params=_params("parallel", "parallel", "arbitrary"),
        name="mamba_ssd",
    )(zx, zx, zx, zx, dt_t, tri, conv_p, cols_p, heads_p)


def _window_attend(operands, mxu_sums):
    T = D_SPAN
    ri = lax.broadcasted_iota(jnp.int32, (T, T), 0)
    ci = lax.broadcasted_iota(jnp.int32, (T, T), 1)
    ones = jnp.ones((T, LANES), BF16)
    scores = [(lax.dot_general(q, kp, NT_DIMS, preferred_element_type=F32),
               lax.dot_general(q, kc, NT_DIMS, preferred_element_type=F32))
              for q, kp, kc, _, _, _ in operands]
    probs = []
    for (sp, sc), (_, _, _, _, _, has_prev) in zip(scores, operands):
        sp = jnp.where((ci >= ri) & has_prev, sp, NEG_BIG)
        sc = jnp.where(ci <= ri, sc, NEG_BIG)
        mx = jnp.maximum(jnp.max(sp, axis=-1, keepdims=True), jnp.max(sc, axis=-1, keepdims=True))
        pp = jnp.exp(sp - mx)
        pc = jnp.exp(sc - mx)
        l = None if mxu_sums else jnp.sum(pp, axis=-1, keepdims=True) + jnp.sum(pc, axis=-1, keepdims=True)
        probs.append((pp.astype(BF16), pc.astype(BF16), mx, l))
    outs = []
    for (pp, pc, mx, l), (_, _, _, vp, vc, _) in zip(probs, operands):
        o = jnp.dot(pp, vp, preferred_element_type=F32) + jnp.dot(pc, vc, preferred_element_type=F32)
        if mxu_sums:
            l = jnp.dot(pp, ones, preferred_element_type=F32) + jnp.dot(pc, ones, preferred_element_type=F32)
        outs.append((o * (1.0 / l), mx + jnp.log(l)))
    return outs


def _dense_window_kernel(q_ref, kp_ref, kc_ref, vp_ref, vc_ref, o_ref, lse_ref, lse_sc, *, tiles, unroll, mxu_sums):
    n = pl.program_id(1)
    h = pl.program_id(2)
    T = D_SPAN

    @pl.when(h == 0)
    def _():
        lse_sc[...] = jnp.zeros_like(lse_sc)

    lane = lax.broadcasted_iota(jnp.int32, (T, LANES), 1)
    for i0 in range(0, tiles, unroll):
        operands = []
        for i in range(i0, i0 + unroll):
            cur = slice(i * T, (i + 1) * T)
            prev = slice((i - 1) * T, i * T)
            kp = kp_ref[0] if i == 0 else kc_ref[0, prev, :]
            vp = vp_ref[0] if i == 0 else vc_ref[0, prev, :]
            operands.append((q_ref[0, cur, :], kp, kc_ref[0, cur, :], vp, vc_ref[0, cur, :],
                             jnp.logical_or(n > 0, i > 0)))
        for i, (o, lse) in zip(range(i0, i0 + unroll), _window_attend(operands, mxu_sums)):
            cur = slice(i * T, (i + 1) * T)
            o_ref[0, cur, :] = o.astype(o_ref.dtype)
            lse_sc[cur, :] = jnp.where(lane == h, lse, lse_sc[cur, :])

    @pl.when(h == D_HEADS - 1)
    def _():
        lse_ref[0] = lse_sc[...]


def _strided_window_kernel(q_ref, k_ref, v_ref, o_ref, lse_ref, lse_sc, q_sc, k_sc, v_sc, o_sc, kprev_sc, vprev_sc,
                           *, dil, tiles, unroll, mxu_sums):
    n = pl.program_id(1)
    h = pl.program_id(2)
    T = D_SPAN
    span = T * dil
    q_sc[...] = q_ref[0].astype(F32)
    k_sc[...] = k_ref[0].astype(F32)
    v_sc[...] = v_ref[0].astype(F32)

    @pl.when(h == 0)
    def _():
        lse_sc[...] = jnp.zeros_like(lse_sc)

    @pl.when(n == 0)
    def _():
        kprev_sc[h] = jnp.zeros(kprev_sc.shape[1:], BF16)
        vprev_sc[h] = jnp.zeros(vprev_sc.shape[1:], BF16)

    lane = lax.broadcasted_iota(jnp.int32, (T, LANES), 1)
    strided = lambda ref, start: ref[pl.ds(start, T, stride=dil), :]

    def group(tile_ids):
        operands, places = [], []
        for idx in tile_ids:
            i, r = (0, idx) if tiles == 1 else divmod(idx, dil)
            start = i * span + r
            saved = pl.ds(pl.multiple_of(r * T, T), T)
            kc = strided(k_sc, start).astype(BF16)
            vc = strided(v_sc, start).astype(BF16)
            if i == 0:
                kp, vp = kprev_sc[h, saved, :], vprev_sc[h, saved, :]
            else:
                kp = strided(k_sc, start - span).astype(BF16)
                vp = strided(v_sc, start - span).astype(BF16)
            operands.append((strided(q_sc, start).astype(BF16), kp, kc, vp, vc, jnp.logical_or(n > 0, i > 0)))
            places.append((start, saved if i == tiles - 1 else None))
        for (start, saved), (_, _, kc, _, vc, _), (o, lse) in zip(places, operands,
                                                                   _window_attend(operands, mxu_sums)):
            o_sc[pl.ds(start, T, stride=dil), :] = o
            lse_sc[pl.ds(start, T, stride=dil), :] = jnp.where(lane == h, lse, strided(lse_sc, start))
            if saved is not None:
                kprev_sc[h, saved, :] = kc
                vprev_sc[h, saved, :] = vc

    for j0 in range(0, dil * tiles, unroll):
        group(list(range(j0, j0 + unroll)))
    o_ref[0] = o_sc[...].astype(o_ref.dtype)

    @pl.when(h == D_HEADS - 1)
    def _():
        lse_ref[0] = lse_sc[...]


def dilated_group(qkv, g, dil, *, tiles, unroll, mxu_sums):
    B, S, C = qkv.shape
    assert (dil * tiles) % unroll == 0
    span = D_SPAN * dil
    tb = span * tiles
    hd = D_HEAD_DIM
    col = lambda t: (g * 3 + t) * D_HEADS
    cur = lambda t: pl.BlockSpec((1, tb, hd), lambda b, n, h: (b, n, col(t) + h))
    prev = lambda t: pl.BlockSpec((1, span, hd), lambda b, n, h: (b, jnp.maximum(n * tiles - 1, 0), col(t) + h))
    if dil == 1:
        kern = functools.partial(_dense_window_kernel, tiles=tiles, unroll=unroll, mxu_sums=mxu_sums)
        in_specs = [cur(0), prev(1), cur(1), prev(2), cur(2)]
        scratch = []
    else:
        kern = functools.partial(_strided_window_kernel, dil=dil, tiles=tiles, unroll=unroll, mxu_sums=mxu_sums)
        in_specs = [cur(0), cur(1), cur(2)]
        scratch = [pltpu.VMEM((tb, hd), F32)] * 4 + [pltpu.VMEM((D_HEADS, span, hd), BF16)] * 2
    o, lse = pl.pallas_call(
        kern,
        grid=(B, S // tb, D_HEADS),
        in_specs=in_specs,
        out_specs=[
            pl.BlockSpec((1, tb, hd), lambda b, n, h: (b, n, h)),
            pl.BlockSpec((1, tb, LANES), lambda b, n, h: (b, n, 0)),
        ],
        out_shape=[
            jax.ShapeDtypeStruct((B, S, D_HEADS * hd), BF16),
            jax.ShapeDtypeStruct((B, S, LANES), F32),
        ],
        scratch_shapes=[pltpu.VMEM((tb, LANES), F32)] + scratch,
        compiler_params=_params("parallel", "arbitrary", "arbitrary"),
        name="dilated_attention",
    )(*([qkv] * len(in_specs)))
    return o.reshape(B * S, D_HEADS * hd), lse.reshape(B * S, LANES)


def _combine_kernel(o0_ref, o1_ref, o2_ref, l0_ref, l1_ref, l2_ref, o_ref):
    l0, l1, l2 = l0_ref[...], l1_ref[...], l2_ref[...]
    mx = jnp.maximum(jnp.maximum(l0, l1), l2)
    e0, e1, e2 = jnp.exp(l0 - mx), jnp.exp(l1 - mx), jnp.exp(l2 - mx)
    inv = 1.0 / (e0 + e1 + e2)
    w0, w1 = e0 * inv, e1 * inv
    for h in range(D_HEADS):
        hs = slice(h * D_HEAD_DIM, (h + 1) * D_HEAD_DIM)
        o2 = o2_ref[:, hs].astype(F32)
        acc = o2 + w0[:, h:h + 1] * (o0_ref[:, hs].astype(F32) - o2)
        acc = acc + w1[:, h:h + 1] * (o1_ref[:, hs].astype(F32) - o2)
        o_ref[:, hs] = acc.astype(o_ref.dtype)


def combine_groups(outs, lses, *, tm=512):
    T, W = outs[0].shape
    tm = min(tm, T)
    wide = pl.BlockSpec((tm, W), lambda i: (i, 0))
    narrow = pl.BlockSpec((tm, LANES), lambda i: (i, 0))
    return pl.pallas_call(
        _combine_kernel,
        grid=(T // tm,),
        in_specs=[wide, wide, wide, narrow, narrow, narrow],
        out_specs=wide,
        out_shape=jax.ShapeDtypeStruct((T, W), BF16),
        compiler_params=_params("parallel"),
        name="combine_groups",
    )(*outs, *lses)


def _deinterleave_heads(w, heads, dim):
    k = w.shape[0]
    return w.reshape(k, heads, dim // 2, 2).transpose(0, 1, 3, 2).reshape(k, heads * dim)


def mixer_a(xr, B, S, nw, a_w_in, a_q_norm_w, a_k_norm_w, lq1, lk1, lq2, lk2, a_subln_w, a_w_out, *, layer_idx):
    T, D = xr.shape
    colw = jnp.concatenate([
        jnp.tile(a_q_norm_w.astype(F32) * (A_HEAD_DIM ** -0.5 * math.log2(math.e)), 2 * A_HEADS),
        jnp.tile(a_k_norm_w.astype(F32), 2 * A_HEADS)]).reshape(1, 2 * D)
    qk = norm_matmul(xr, nw, a_w_in[:, :2 * D].astype(BF16), colw)
    v_t = norm_matmul_t(xr, nw, a_w_in[:, 2 * D:].T.astype(BF16))
    o = diff_attention(qk.reshape(B, S, 2 * D), v_t, lq1, lk1, lq2, lk2, a_subln_w, layer_idx=layer_idx)
    return matmul_residual(o.reshape(T, D), a_w_out.astype(BF16), xr)


def mixer_b(xr, B, S, nw, b_w_in, b_gn_w, b_w_out):
    T, D = xr.shape
    w_in = jnp.concatenate([
        _deinterleave_heads(b_w_in[:, :D], R_HEADS, R_KDIM),
        _deinterleave_heads(b_w_in[:, D:2 * D], R_HEADS, R_KDIM),
        b_w_in[:, 2 * D:]], axis=1)
    proj = norm_matmul(xr, nw, w_in.astype(BF16))
    o = retention(proj.reshape(B, S, -1), b_gn_w)
    return matmul_residual(o.reshape(T, -1), b_w_out.astype(BF16), xr)


def mixer_c(xr, B, S, nw, c_w_in, c_conv_w, c_conv_b, c_dt_bias, c_a_log, c_d_skip, c_norm_w, c_w_out):
    T, D = xr.shape
    n_main = 2 * M_D_INNER + 2 * M_GROUPS * M_DSTATE
    w_dt = jnp.pad(c_w_in[:, n_main:], ((0, 0), (0, LANES - M_HEADS)))
    zx, dt_raw = norm_matmul(xr, nw, c_w_in[:, :n_main].astype(BF16), side_w=w_dt.astype(BF16))
    y = mamba_ssd(zx.reshape(B, S, n_main), dt_raw.reshape(B, S, LANES), c_conv_w, c_conv_b, c_dt_bias,
                  c_a_log, c_d_skip, c_norm_w)
    return matmul_residual(y.reshape(T, M_D_INNER), c_w_out.astype(BF16), xr)


def mixer_d(xr, B, S, nw, d_w_in, d_q_norm_w, d_k_norm_w, d_w_out):
    T, D = xr.shape
    scale = D_HEAD_DIM ** -0.5
    colw = jnp.concatenate([
        jnp.concatenate([jnp.tile(d_q_norm_w[g].astype(F32) * scale, D_HEADS),
                         jnp.tile(d_k_norm_w[g].astype(F32), D_HEADS),
                         jnp.ones((D,), F32)])
        for g in range(len(D_PATTERNS))]).reshape(1, -1)
    qkv = norm_matmul(xr, nw, d_w_in.astype(BF16), colw).reshape(B, S, -1)
    outs, lses = zip(*[dilated_group(qkv, g, dil, tiles=max(1, 8 // dil), unroll=8 if dil < 16 else 4,
                                     mxu_sums=dil < 16)
                       for g, (_, dil) in enumerate(D_PATTERNS)])
    o = combine_groups(outs, lses)
    return matmul_residual(o, d_w_out.astype(BF16), xr)


@jax.jit
def kernel(x, norm1_w, norm2_w, mlp_w1, mlp_w2, a_w_in, a_q_norm_w, a_k_norm_w, a_lambda_q1, a_lambda_k1, a_lambda_q2, a_lambda_k2, a_subln_w, a_w_out, b_w_in, b_gn_w, b_w_out, c_w_in, c_conv_w, c_conv_b, c_dt_bias, c_a_log, c_d_skip, c_norm_w, c_w_out, d_w_in, d_q_norm_w, d_k_norm_w, d_w_out):
    B, S, D = x.shape
    xr = x.reshape(B * S, D)
    w1, w2 = mlp_w1.astype(BF16), mlp_w2.astype(BF16)
    ffn = lambda t, i: mlp(t, norm2_w[i], w1, w2, i)
    xr = mixer_a(xr, B, S, norm1_w[0], a_w_in, a_q_norm_w, a_k_norm_w, a_lambda_q1, a_lambda_k1,
                 a_lambda_q2, a_lambda_k2, a_subln_w, a_w_out, layer_idx=0)
    xr = ffn(xr, 0)
    xr = mixer_b(xr, B, S, norm1_w[1], b_w_in, b_gn_w, b_w_out)
    xr = ffn(xr, 1)
    xr = mixer_c(xr, B, S, norm1_w[2], c_w_in, c_conv_w, c_conv_b, c_dt_bias, c_a_log, c_d_skip, c_norm_w,
                 c_w_out)
    xr = ffn(xr, 2)
    xr = mixer_d(xr, B, S, norm1_w[3], d_w_in, d_q_norm_w, d_k_norm_w, d_w_out)
    xr = ffn(xr, 3)
    return xr.reshape(B, S, D)
```

```python
import functools
import math

import jax
import jax.numpy as jnp
from jax import lax
from jax.experimental import pallas as pl
from jax.experimental.pallas import tpu as pltpu

F32 = jnp.float32
BF16 = jnp.bfloat16

EPS = 1e-6
D_MODEL = 2048
D_FF = 4 * D_MODEL
LANES = 128
MXU_COLS = 256
ROW_BLOCK = 256

A_HEAD_DIM = 128
A_HEADS = 8
R_HEADS = 8
R_KDIM = 256
R_VDIM = 512
M_D_INNER = 4096
M_HEADDIM = 64
M_HEADS = 64
M_GROUPS = 8
M_DSTATE = 128
M_CONV = 4
M_GROUP_W = M_D_INNER // M_GROUPS
M_HPG = M_HEADS // M_GROUPS
D_HEADS = 16
D_HEAD_DIM = 128
D_PATTERNS = ((128, 1), (512, 4), (2048, 16))
D_SPAN = 128

VMEM_LIMIT_BYTES = 56 * 1024 * 1024
NEG_BIG = -1e30

NT_DIMS = (((1,), (1,)), ((), ()))
TN_DIMS = (((0,), (0,)), ((), ()))


def _params(*sem):
    return pltpu.CompilerParams(dimension_semantics=sem, vmem_limit_bytes=VMEM_LIMIT_BYTES)


def _silu(v):
    h = 0.5 * v
    return h + h * jnp.tanh(h)


def _rms_rows(v, w):
    ms = jnp.mean(v * v, axis=-1, keepdims=True)
    return v * lax.rsqrt(ms + EPS) * w


def _split3(v):
    p1 = v.astype(BF16)
    r1 = v - p1.astype(F32)
    p2 = r1.astype(BF16)
    return p1, p2, (r1 - p2.astype(F32)).astype(BF16)


def _row_blocks(tm):
    rb = min(tm, ROW_BLOCK)
    return [slice(r, r + rb) for r in range(0, tm, rb)]


def _norm_matmul_kernel(x_ref, nw_ref, w_ref, cw_ref, *rest, tm, tn, head_norm, period, count):
    if len(rest) == 4:
        ws_ref, o_ref, side_ref, h_sc = rest
    else:
        (o_ref, h_sc), ws_ref, side_ref = rest, None, None
    j = pl.program_id(1)
    every = slice(0, tm)

    def plain(rows):
        o_ref[rows, :] = jnp.dot(h_sc[rows, :], w_ref[...], preferred_element_type=F32).astype(o_ref.dtype)

    def normed(rows):
        sub = min(tn, MXU_COLS)
        for c in range(tn // sub):
            acc = jnp.dot(h_sc[rows, :], w_ref[:, c * sub:(c + 1) * sub], preferred_element_type=F32)
            for d in range(sub // LANES):
                sl = slice(c * sub + d * LANES, c * sub + (d + 1) * LANES)
                o_ref[rows, sl] = _rms_rows(acc[:, d * LANES:(d + 1) * LANES], cw_ref[:, sl]).astype(o_ref.dtype)

    @pl.when(j == 0)
    def _():
        for rows in _row_blocks(tm):
            h_sc[rows, :] = _rms_rows(x_ref[rows, :], nw_ref[...]).astype(BF16)
            (normed if head_norm else plain)(rows)
            if side_ref is not None:
                side_ref[rows, :] = jnp.dot(h_sc[rows, :], ws_ref[...], preferred_element_type=F32)

    if not head_norm:
        pl.when(j > 0)(functools.partial(plain, every))
        return
    is_norm = (j % period) < count

    @pl.when(jnp.logical_and(j > 0, is_norm))
    def _():
        for rows in _row_blocks(tm):
            normed(rows)

    pl.when(jnp.logical_not(is_norm))(functools.partial(plain, every))


def norm_matmul(x, nw, w, colw=None, side_w=None, *, tm=1024, tn=1024):
    T, K = x.shape
    N = w.shape[1]
    tm = min(tm, T)
    tn = min(tn, N)
    head_norm = colw is not None
    if colw is None:
        colw = jnp.ones((1, N), F32)
    period = 3 * D_MODEL // tn
    count = 2 * D_MODEL // tn
    kern = functools.partial(_norm_matmul_kernel, tm=tm, tn=tn, head_norm=head_norm, period=period, count=count)
    in_specs = [
        pl.BlockSpec((tm, K), lambda i, j: (i, 0)),
        pl.BlockSpec((1, K), lambda i, j: (0, 0)),
        pl.BlockSpec((K, tn), lambda i, j: (0, j)),
        pl.BlockSpec((1, tn), lambda i, j: (0, j)),
    ]
    out_specs = pl.BlockSpec((tm, tn), lambda i, j: (i, j))
    out_shape = jax.ShapeDtypeStruct((T, N), BF16)
    args = (x, nw.reshape(1, K), w, colw)
    if side_w is not None:
        in_specs.append(pl.BlockSpec((K, LANES), lambda i, j: (0, 0)))
        out_specs = [out_specs, pl.BlockSpec((tm, LANES), lambda i, j: (i, 0))]
        out_shape = [out_shape, jax.ShapeDtypeStruct((T, LANES), F32)]
        args += (side_w,)
    return pl.pallas_call(
        kern,
        grid=(T // tm, N // tn),
        in_specs=in_specs,
        out_specs=out_specs,
        out_shape=out_shape,
        scratch_shapes=[pltpu.VMEM((tm, K), BF16)],
        compiler_params=_params("parallel", "arbitrary"),
        name="norm_matmul",
    )(*args)


def _norm_matmul_t_kernel(x_ref, nw_ref, wt_ref, o_ref, h_sc, *, tm):
    j = pl.program_id(1)

    def emit(rows):
        o_ref[:, rows] = lax.dot_general(wt_ref[...], h_sc[rows, :], NT_DIMS,
                                         preferred_element_type=F32).astype(o_ref.dtype)

    @pl.when(j == 0)
    def _():
        for rows in _row_blocks(tm):
            h_sc[rows, :] = _rms_rows(x_ref[rows, :], nw_ref[...]).astype(BF16)
            emit(rows)

    pl.when(j > 0)(functools.partial(emit, slice(0, tm)))


def norm_matmul_t(x, nw, w_t, *, tm=1024, tn=1024):
    T, K = x.shape
    N = w_t.shape[0]
    tm = min(tm, T)
    return pl.pallas_call(
        functools.partial(_norm_matmul_t_kernel, tm=tm),
        grid=(T // tm, N // tn),
        in_specs=[
            pl.BlockSpec((tm, K), lambda i, j: (i, 0)),
            pl.BlockSpec((1, K), lambda i, j: (0, 0)),
            pl.BlockSpec((tn, K), lambda i, j: (j, 0)),
        ],
        out_specs=pl.BlockSpec((tn, tm), lambda i, j: (j, i)),
        out_shape=jax.ShapeDtypeStruct((N, T), BF16),
        scratch_shapes=[pltpu.VMEM((tm, K), BF16)],
        compiler_params=_params("parallel", "arbitrary"),
        name="norm_matmul_t",
    )(x, nw.reshape(1, K), w_t)


def _matmul_residual_kernel(a_ref, w_ref, r_ref, o_ref):
    o_ref[...] = r_ref[...] + jnp.dot(a_ref[...], w_ref[...], preferred_element_type=F32)


def matmul_residual(a, w, res, *, tm=1024, tn=1024):
    T, K = a.shape
    N = w.shape[1]
    tm = min(tm, T)
    return pl.pallas_call(
        _matmul_residual_kernel,
        grid=(T // tm, N // tn),
        in_specs=[
            pl.BlockSpec((tm, K), lambda i, j: (i, 0)),
            pl.BlockSpec((K, tn), lambda i, j: (0, j)),
            pl.BlockSpec((tm, tn), lambda i, j: (i, j)),
        ],
        out_specs=pl.BlockSpec((tm, tn), lambda i, j: (i, j)),
        out_shape=jax.ShapeDtypeStruct((T, N), F32),
        compiler_params=_params("parallel", "arbitrary"),
        name="matmul_residual",
    )(a, w, res)


def _mlp_kernel(x_ref, nw_ref, w1_ref, w2_ref, o_ref, h_sc, *, tm):
    f = pl.program_id(1)

    def accumulate(rows):
        u = jnp.dot(h_sc[rows, :], w1_ref[0], preferred_element_type=F32)
        u = jnp.square(jnp.maximum(u, 0.0)).astype(BF16)
        o_ref[rows, :] += jnp.dot(u, w2_ref[0], preferred_element_type=F32)

    @pl.when(f == 0)
    def _():
        for rows in _row_blocks(tm):
            x = x_ref[rows, :]
            h_sc[rows, :] = _rms_rows(x, nw_ref[...]).astype(BF16)
            o_ref[rows, :] = x
            accumulate(rows)

    pl.when(f > 0)(functools.partial(accumulate, slice(0, tm)))


def mlp(x, nw, w1, w2, layer, *, tm=512, tf=1024):
    T, D = x.shape
    FF = w1.shape[2]
    tm = min(tm, T)
    return pl.pallas_call(
        functools.partial(_mlp_kernel, tm=tm),
        grid=(T // tm, FF // tf),
        in_specs=[
            pl.BlockSpec((tm, D), lambda i, f: (i, 0)),
            pl.BlockSpec((1, D), lambda i, f: (0, 0)),
            pl.BlockSpec((1, D, tf), lambda i, f: (layer, 0, f)),
            pl.BlockSpec((1, tf, D), lambda i, f: (layer, f, 0)),
        ],
        out_specs=pl.BlockSpec((tm, D), lambda i, f: (i, 0)),
        out_shape=jax.ShapeDtypeStruct((T, D), F32),
        scratch_shapes=[pltpu.VMEM((tm, D), BF16)],
        compiler_params=_params("parallel", "arbitrary"),
        name="mlp",
    )(x, nw.reshape(1, D), w1, w2)


def _diff_attn_kernel(q_ref, k_ref, vt_ref, lq1_ref, lk1_ref, lq2_ref, lk2_ref, sw_ref, o_ref,
                      m_sc, l_sc, acc_sc, sa_sc, sb_sc, *, tq, lambda_init):
    qi = pl.program_id(2)
    m_sc[...] = jnp.full_like(m_sc, NEG_BIG)
    l_sc[...] = jnp.zeros_like(l_sc)
    acc_sc[...] = jnp.zeros_like(acc_sc)
    key = lax.broadcasted_iota(jnp.int32, (tq, tq), 0)
    qry = lax.broadcasted_iota(jnp.int32, (tq, tq), 1)

    def scores(m, start):
        hs = slice(m * A_HEAD_DIM, (m + 1) * A_HEAD_DIM)
        return lax.dot_general(k_ref[0, pl.ds(start, tq), hs], q_ref[0, :, hs], NT_DIMS,
                               preferred_element_type=F32)

    def update(m, s, start, masked):
        if masked:
            s = jnp.where(key <= qry, s, NEG_BIG)
        m_prev = m_sc[m]
        m_new = jnp.maximum(m_prev, jnp.max(s, axis=0, keepdims=True))
        alpha = jnp.exp2(m_prev - m_new)
        p = jnp.exp2(s - m_new)
        m_sc[m] = m_new
        l_sc[m] = alpha * l_sc[m] + jnp.sum(p, axis=0, keepdims=True)
        acc_sc[m] = alpha * acc_sc[m] + jnp.dot(vt_ref[:, pl.ds(start, tq)], p.astype(BF16),
                                                preferred_element_type=F32)

    def put_scores(s_sc, j):
        start = pl.multiple_of(j * tq, tq)
        s_sc[0] = scores(0, start)
        s_sc[1] = scores(1, start)

    def updates(s_sc, j, masked):
        start = pl.multiple_of(j * tq, tq)
        update(0, s_sc[0], start, masked)
        update(1, s_sc[1], start, masked)

    put_scores(sa_sc, 0)

    def body(jj, carry):
        j = 2 * jj
        put_scores(sb_sc, j + 1)
        updates(sa_sc, j, False)
        put_scores(sa_sc, j + 2)
        updates(sb_sc, j + 1, False)
        return carry

    lax.fori_loop(0, qi // 2, body, 0)

    @pl.when(qi % 2 == 0)
    def _():
        updates(sa_sc, qi, True)

    @pl.when(qi % 2 == 1)
    def _():
        put_scores(sb_sc, qi)
        updates(sa_sc, qi - 1, False)
        updates(sb_sc, qi, True)

    lam = (jnp.exp(jnp.sum(lq1_ref[...] * lk1_ref[...], axis=-1, keepdims=True))
           - jnp.exp(jnp.sum(lq2_ref[...] * lk2_ref[...], axis=-1, keepdims=True)) + lambda_init)
    o_t = acc_sc[0] * (1.0 / l_sc[0]) - lam * (acc_sc[1] * (1.0 / l_sc[1]))
    o_ref[0] = (_rms_rows(o_t.T, sw_ref[...]) * (1.0 - lambda_init)).astype(o_ref.dtype)


def diff_attention(qk, v_t, lq1, lk1, lq2, lk2, subln_w, *, layer_idx, tq=512):
    B, S, _ = qk.shape
    tq = min(tq, S)
    lambda_init = 0.8 - 0.6 * math.exp(-0.3 * layer_idx)
    pw = 2 * A_HEAD_DIM
    vec = lambda a: a.reshape(1, -1).astype(F32)
    small = lambda n: pl.BlockSpec((1, n), lambda b, h, i: (0, 0))
    kern = functools.partial(_diff_attn_kernel, tq=tq, lambda_init=lambda_init)
    return pl.pallas_call(
        kern,
        grid=(B, A_HEADS, S // tq),
        in_specs=[
            pl.BlockSpec((1, tq, pw), lambda b, h, i: (b, i, h)),
            pl.BlockSpec((1, S, pw), lambda b, h, i: (b, 0, A_HEADS + h)),
            pl.BlockSpec((pw, S), lambda b, h, i: (h, b)),
            small(A_HEAD_DIM), small(A_HEAD_DIM), small(A_HEAD_DIM), small(A_HEAD_DIM), small(pw),
        ],
        out_specs=pl.BlockSpec((1, tq, pw), lambda b, h, i: (b, i, h)),
        out_shape=jax.ShapeDtypeStruct((B, S, D_MODEL), BF16),
        scratch_shapes=[
            pltpu.VMEM((2, 1, tq), F32),
            pltpu.VMEM((2, 1, tq), F32),
            pltpu.VMEM((2, pw, tq), F32),
            pltpu.VMEM((2, tq, tq), F32),
            pltpu.VMEM((2, tq, tq), F32),
        ],
        compiler_params=_params("parallel", "parallel", "arbitrary"),
        name="diff_attention",
    )(qk, qk, v_t, vec(lq1), vec(lk1), vec(lq2), vec(lk2), vec(subln_w))


def _retention_kernel(q_ref, k_ref, v_ref, g_ref, cos_ref, sin_ref, dm_ref, xi_ref, zeta_ref, cd_ref,
                      gw_ref, o_ref, r_sc, *, tr, chunk):
    @pl.when(pl.program_id(2) == 0)
    def _():
        r_sc[...] = jnp.zeros_like(r_sc)

    half = R_KDIM // 2
    dmask = dm_ref[0]
    xi = xi_ref[0]
    zeta = zeta_ref[0]
    cdecay = cd_ref[0, 0:1, 0:1]
    gw = gw_ref[0]

    def rotate(t, c, s):
        te, to = t[:, :half], t[:, half:]
        return te * c - to * s, to * c + te * s

    for ci in range(tr // chunk):
        rows = slice(ci * chunk, (ci + 1) * chunk)
        c = cos_ref[rows, :]
        s = sin_ref[rows, :]
        qe, qo = rotate(q_ref[0, rows, :].astype(F32), c, s)
        ke, ko = rotate(k_ref[0, rows, :].astype(F32) * (R_KDIM ** -0.5), c, s)
        v = v_ref[0, rows, :]
        q_r = jnp.concatenate([qe, qo], axis=1).astype(BF16)
        k_r = jnp.concatenate([ke, ko], axis=1).astype(BF16)
        q_x = jnp.concatenate([qe * xi, qo * xi], axis=1).astype(BF16)
        k_z = jnp.concatenate([ke * zeta, ko * zeta], axis=1).astype(BF16)
        r_old = r_sc[...]
        sc = lax.dot_general(q_r, k_r, NT_DIMS, preferred_element_type=F32) * dmask
        o = (jnp.dot(sc.astype(BF16), v, preferred_element_type=F32)
             + jnp.dot(q_x, r_old.astype(BF16), preferred_element_type=F32))
        r_sc[...] = r_old * cdecay + lax.dot_general(k_z, v, TN_DIMS, preferred_element_type=F32)
        gate = _silu(g_ref[0, rows, :].astype(F32))
        o_ref[0, rows, :] = (gate * _rms_rows(o, gw)).astype(o_ref.dtype)


def retention(proj, gn_w, *, tr=1024, chunk=256):
    B, S, _ = proj.shape
    tr = min(tr, S)
    chunk = min(chunk, tr)
    half = R_KDIM // 2
    angle = 1.0 / (10000.0 ** jnp.linspace(0.0, 1.0, half, dtype=F32))
    ang = jnp.arange(S, dtype=F32)[:, None] * angle[None, :]
    cos, sin = jnp.cos(ang), jnp.sin(ang)
    log_g = jnp.log(1.0 - 2.0 ** (-5.0 - jnp.arange(R_HEADS, dtype=F32)))
    idx = jnp.arange(chunk, dtype=F32)
    rel = idx[:, None] - idx[None, :]
    dmask = jnp.where(rel[None] >= 0, jnp.exp(jnp.maximum(rel, 0.0)[None] * log_g[:, None, None]), 0.0)
    xi = jnp.exp((idx + 1.0)[None, :] * log_g[:, None])
    zeta = jnp.exp((chunk - 1.0 - idx)[None, :] * log_g[:, None])
    cdecay = jnp.exp(chunk * log_g)
    bc = lambda t: jnp.broadcast_to(t[:, :, None], (R_HEADS, chunk, half))
    cd = jnp.broadcast_to(cdecay[:, None, None], (R_HEADS, 8, LANES))
    nq = D_MODEL // R_KDIM
    nv = 2 * D_MODEL // R_VDIM
    kern = functools.partial(_retention_kernel, tr=tr, chunk=chunk)
    head = lambda shape: pl.BlockSpec(shape, lambda b, h, i: (h, 0, 0))
    return pl.pallas_call(
        kern,
        grid=(B, R_HEADS, S // tr),
        in_specs=[
            pl.BlockSpec((1, tr, R_KDIM), lambda b, h, i: (b, i, h)),
            pl.BlockSpec((1, tr, R_KDIM), lambda b, h, i: (b, i, nq + h)),
            pl.BlockSpec((1, tr, R_VDIM), lambda b, h, i: (b, i, nv + h)),
            pl.BlockSpec((1, tr, R_VDIM), lambda b, h, i: (b, i, nv + R_HEADS + h)),
            pl.BlockSpec((tr, half), lambda b, h, i: (i, 0)),
            pl.BlockSpec((tr, half), lambda b, h, i: (i, 0)),
            head((1, chunk, chunk)), head((1, chunk, half)), head((1, chunk, half)),
            head((1, 8, LANES)), head((1, 1, R_VDIM)),
        ],
        out_specs=pl.BlockSpec((1, tr, R_VDIM), lambda b, h, i: (b, i, h)),
        out_shape=jax.ShapeDtypeStruct((B, S, R_HEADS * R_VDIM), BF16),
        scratch_shapes=[pltpu.VMEM((R_KDIM, R_VDIM), F32)],
        compiler_params=_params("parallel", "parallel", "arbitrary"),
        name="retention",
    )(proj, proj, proj, proj, cos, sin, dmask, bc(xi), bc(zeta), cd, gn_w.reshape(R_HEADS, 1, R_VDIM).astype(F32))


def _mamba_kernel(z_ref, x_ref, b_ref, c_ref, dt_ref, tri_ref, conv_ref, cols_ref, heads_ref, o_ref,
                  st_sc, tx_sc, tb_sc, tc_sc, *, chunk):
    L = chunk
    W, N = M_GROUP_W, M_DSTATE
    x_cols, b_cols, c_cols = slice(0, W), slice(W, W + N), slice(W + N, W + 2 * N)

    @pl.when(pl.program_id(2) == 0)
    def _():
        st_sc[...] = jnp.zeros_like(st_sc)
        for ext_sc in (tx_sc, tb_sc, tc_sc):
            ext_sc[0:8, :] = jnp.zeros((8, ext_sc.shape[1]), F32)

    def conv_silu(cur_ref, ext_sc, cols):
        ext_sc[8:, :] = cur_ref[0].astype(F32)
        w = conv_ref[0, 0:M_CONV, cols]
        acc = conv_ref[0, M_CONV:M_CONV + 1, cols] + ext_sc[8:, :] * w[M_CONV - 1:M_CONV, :]
        for kk in range(M_CONV - 1):
            sh = M_CONV - 1 - kk
            acc = acc + ext_sc[8 - sh:8 - sh + L, :] * w[kk:kk + 1, :]
        ext_sc[0:8, :] = ext_sc[L:, :]
        return _silu(acc)

    raw = dt_ref[0, 0] + heads_ref[0, :, 0:1]
    dt_t = jnp.maximum(raw, 0.0) + jnp.log(1.0 + jnp.exp(-jnp.abs(raw)))
    a_t = dt_t * (-jnp.exp(heads_ref[0, :, 1:2])) * math.log2(math.e)

    tril = lax.broadcasted_iota(jnp.int32, (L, L), 1) <= lax.broadcasted_iota(jnp.int32, (L, L), 0)
    eye, lower, upper = tri_ref[0], tri_ref[1], tri_ref[2]

    def per_column(t):
        return jnp.concatenate([jnp.broadcast_to(t[r:r + 1, :], (M_HEADDIM, L)) for r in range(M_HPG)], axis=0)

    a_parts = _split3(a_t)
    acs_row = sum(jnp.dot(p, upper, preferred_element_type=F32) for p in a_parts)

    def expand_dot(mat01, parts):
        return sum(lax.dot_general(mat01, per_column(p.astype(F32)).astype(BF16), NT_DIMS,
                                   preferred_element_type=F32) for p in parts)

    acs_x = expand_dot(lower, a_parts)
    dt_x = expand_dot(eye, _split3(dt_t))

    xs = conv_silu(x_ref, tx_sc, x_cols)
    bm = conv_silu(b_ref, tb_sc, b_cols)
    cm = conv_silu(c_ref, tc_sc, c_cols)

    xdt = xs * dt_x
    cb = lax.dot_general(cm.astype(BF16), bm.astype(BF16), NT_DIMS, preferred_element_type=F32)
    xdt_b = xdt.astype(BF16)
    parts = []
    for r in range(M_HPG):
        diff = acs_x[:, r * M_HEADDIM:r * M_HEADDIM + 1] - acs_row[r:r + 1, :]
        lmat = jnp.exp2(jnp.where(tril, diff, NEG_BIG))
        mr = (cb * lmat).astype(BF16)
        parts.append(jnp.dot(mr, xdt_b[:, r * M_HEADDIM:(r + 1) * M_HEADDIM], preferred_element_type=F32))
    y = jnp.concatenate(parts, axis=1)

    state = st_sc[...]
    y = y + jnp.exp2(acs_x) * jnp.dot(cm.astype(BF16), state.astype(BF16), preferred_element_type=F32)
    last = acs_x[L - 1:L, :]
    decay_end = jnp.exp2(last - acs_x)
    st_sc[...] = state * jnp.exp2(last) + lax.dot_general(
        bm.astype(BF16), (xdt * decay_end).astype(BF16), TN_DIMS, preferred_element_type=F32)

    y = y + xs * cols_ref[0, 0:1, :]
    y = y * _silu(z_ref[0].astype(F32))
    o_ref[0] = _rms_rows(y, cols_ref[0, 1:2, :]).astype(o_ref.dtype)


def mamba_ssd(zx, dt_raw, conv_w, conv_b, dt_bias, a_log, d_skip, norm_w, *, chunk=256):
    B, S, _ = zx.shape
    chunk = min(chunk, S)
    G, W, N = M_GROUPS, M_GROUP_W, M_DSTATE
    dt_t = dt_raw[:, :, :M_HEADS].reshape(B, S, G, M_HPG).transpose(0, 2, 3, 1)
    xoff = M_D_INNER // W
    boff = 2 * M_D_INNER // N
    coff = boff + G
    taps = jnp.concatenate([conv_w.astype(F32), conv_b.reshape(1, -1).astype(F32)], axis=0)
    split = lambda lo, width: taps[:, lo:lo + G * width].reshape(M_CONV + 1, G, width)
    conv_p = jnp.concatenate([split(0, W), split(M_D_INNER, N), split(M_D_INNER + G * N, N)], axis=2)
    conv_p = jnp.pad(conv_p.transpose(1, 0, 2), ((0, 0), (0, 8 - (M_CONV + 1)), (0, 0)))
    cols_p = jnp.stack([jnp.repeat(d_skip.astype(F32), M_HEADDIM), norm_w.astype(F32)])
    cols_p = jnp.pad(cols_p.reshape(2, G, W).transpose(1, 0, 2), ((0, 0), (0, 6), (0, 0)))
    heads_p = jnp.stack([dt_bias.astype(F32), a_log.astype(F32)], axis=-1).reshape(G, M_HPG, 2)
    heads_p = jnp.pad(heads_p, ((0, 0), (0, 0), (0, LANES - 2)))
    idx = jnp.arange(chunk)
    tri = jnp.stack([idx[:, None] == idx[None, :], idx[:, None] >= idx[None, :],
                     idx[:, None] <= idx[None, :]]).astype(BF16)
    group = lambda shape: pl.BlockSpec(shape, lambda b, g, c: (g, 0, 0))
    kern = functools.partial(_mamba_kernel, chunk=chunk)
    return pl.pallas_call(
        kern,
        grid=(B, G, S // chunk),
        in_specs=[
            pl.BlockSpec((1, chunk, W), lambda b, g, c: (b, c, g)),
            pl.BlockSpec((1, chunk, W), lambda b, g, c: (b, c, xoff + g)),
            pl.BlockSpec((1, chunk, N), lambda b, g, c: (b, c, boff + g)),
            pl.BlockSpec((1, chunk, N), lambda b, g, c: (b, c, coff + g)),
            pl.BlockSpec((1, 1, M_HPG, chunk), lambda b, g, c: (b, g, 0, c)),
            pl.BlockSpec((3, chunk, chunk), lambda b, g, c: (0, 0, 0)),
            group((1, 8, W + 2 * N)), group((1, 8, W)), group((1, 8, LANES)),
        ],
        out_specs=pl.BlockSpec((1, chunk, W), lambda b, g, c: (b, c, g)),
        out_shape=jax.ShapeDtypeStruct((B, S, M_D_INNER), BF16),
        scratch_shapes=[
            pltpu.VMEM((N, W), F32),
            pltpu.VMEM((8 + chunk, W), F32),
            pltpu.VMEM((8 + chunk, N), F32),
            pltpu.VMEM((8 + chunk, N), F32),
        ],
        compiler_params=_params("parallel", "parallel", "arbitrary"),
        name="mamba_ssd",
    )(zx, zx, zx, zx, dt_t, tri, conv_p, cols_p, heads_p)


def _window_attend(operands, mxu_sums):
    T = D_SPAN
    ri = lax.broadcasted_iota(jnp.int32, (T, T), 0)
    ci = lax.broadcasted_iota(jnp.int32, (T, T), 1)
    ones = jnp.ones((T, LANES), BF16)
    scores = [(lax.dot_general(q, kp, NT_DIMS, preferred_element_type=F32),
               lax.dot_general(q, kc, NT_DIMS, preferred_element_type=F32))
              for q, kp, kc, _, _, _ in operands]
    probs = []
    for (sp, sc), (_, _, _, _, _, has_prev) in zip(scores, operands):
        sp = jnp.where((ci >= ri) & has_prev, sp, NEG_BIG)
        sc = jnp.where(ci <= ri, sc, NEG_BIG)
        mx = jnp.maximum(jnp.max(sp, axis=-1, keepdims=True), jnp.max(sc, axis=-1, keepdims=True))
        pp = jnp.exp(sp - mx)
        pc = jnp.exp(sc - mx)
        l = None if mxu_sums else jnp.sum(pp, axis=-1, keepdims=True) + jnp.sum(pc, axis=-1, keepdims=True)
        probs.append((pp.astype(BF16), pc.astype(BF16), mx, l))
    outs = []
    for (pp, pc, mx, l), (_, _, _, vp, vc, _) in zip(probs, operands):
        o = jnp.dot(pp, vp, preferred_element_type=F32) + jnp.dot(pc, vc, preferred_element_type=F32)
        if mxu_sums:
            l = jnp.dot(pp, ones, preferred_element_type=F32) + jnp.dot(pc, ones, preferred_element_type=F32)
        outs.append((o * (1.0 / l), mx + jnp.log(l)))
    return outs


def _dense_window_kernel(q_ref, kp_ref, kc_ref, vp_ref, vc_ref, o_ref, lse_ref, lse_sc, *, tiles, unroll, mxu_sums):
    n = pl.program_id(1)
    h = pl.program_id(2)
    T = D_SPAN

    @pl.when(h == 0)
    def _():
        lse_sc[...] = jnp.zeros_like(lse_sc)

    lane = lax.broadcasted_iota(jnp.int32, (T, LANES), 1)
    for i0 in range(0, tiles, unroll):
        operands = []
        for i in range(i0, i0 + unroll):
            cur = slice(i * T, (i + 1) * T)
            prev = slice((i - 1) * T, i * T)
            kp = kp_ref[0] if i == 0 else kc_ref[0, prev, :]
            vp = vp_ref[0] if i == 0 else vc_ref[0, prev, :]
            operands.append((q_ref[0, cur, :], kp, kc_ref[0, cur, :], vp, vc_ref[0, cur, :],
                             jnp.logical_or(n > 0, i > 0)))
        for i, (o, lse) in zip(range(i0, i0 + unroll), _window_attend(operands, mxu_sums)):
            cur = slice(i * T, (i + 1) * T)
            o_ref[0, cur, :] = o.astype(o_ref.dtype)
            lse_sc[cur, :] = jnp.where(lane == h, lse, lse_sc[cur, :])

    @pl.when(h == D_HEADS - 1)
    def _():
        lse_ref[0] = lse_sc[...]


def _strided_window_kernel(q_ref, k_ref, v_ref, o_ref, lse_ref, lse_sc, q_sc, k_sc, v_sc, o_sc, kprev_sc, vprev_sc,
                           *, dil, tiles, unroll, mxu_sums):
    n = pl.program_id(1)
    h = pl.program_id(2)
    T = D_SPAN
    span = T * dil
    q_sc[...] = q_ref[0].astype(F32)
    k_sc[...] = k_ref[0].astype(F32)
    v_sc[...] = v_ref[0].astype(F32)

    @pl.when(h == 0)
    def _():
        lse_sc[...] = jnp.zeros_like(lse_sc)

    @pl.when(n == 0)
    def _():
        kprev_sc[h] = jnp.zeros(kprev_sc.shape[1:], BF16)
        vprev_sc[h] = jnp.zeros(vprev_sc.shape[1:], BF16)

    lane = lax.broadcasted_iota(jnp.int32, (T, LANES), 1)
    strided = lambda ref, start: ref[pl.ds(start, T, stride=dil), :]

    def group(tile_ids):
        operands, places = [], []
        for idx in tile_ids:
            i, r = (0, idx) if tiles == 1 else divmod(idx, dil)
            start = i * span + r
            saved = pl.ds(pl.multiple_of(r * T, T), T)
            kc = strided(k_sc, start).astype(BF16)
            vc = strided(v_sc, start).astype(BF16)
            if i == 0:
                kp, vp = kprev_sc[h, saved, :], vprev_sc[h, saved, :]
            else:
                kp = strided(k_sc, start - span).astype(BF16)
                vp = strided(v_sc, start - span).astype(BF16)
            operands.append((strided(q_sc, start).astype(BF16), kp, kc, vp, vc, jnp.logical_or(n > 0, i > 0)))
            places.append((start, saved if i == tiles - 1 else None))
        for (start, saved), (_, _, kc, _, vc, _), (o, lse) in zip(places, operands,
                                                                   _window_attend(operands, mxu_sums)):
            o_sc[pl.ds(start, T, stride=dil), :] = o
            lse_sc[pl.ds(start, T, stride=dil), :] = jnp.where(lane == h, lse, strided(lse_sc, start))
            if saved is not None:
                kprev_sc[h, saved, :] = kc
                vprev_sc[h, saved, :] = vc

    for j0 in range(0, dil * tiles, unroll):
        group(list(range(j0, j0 + unroll)))
    o_ref[0] = o_sc[...].astype(o_ref.dtype)

    @pl.when(h == D_HEADS - 1)
    def _():
        lse_ref[0] = lse_sc[...]


def dilated_group(qkv, g, dil, *, tiles, unroll, mxu_sums):
    B, S, C = qkv.shape
    assert (dil * tiles) % unroll == 0
    span = D_SPAN * dil
    tb = span * tiles
    hd = D_HEAD_DIM
    col = lambda t: (g * 3 + t) * D_HEADS
    cur = lambda t: pl.BlockSpec((1, tb, hd), lambda b, n, h: (b, n, col(t) + h))
    prev = lambda t: pl.BlockSpec((1, span, hd), lambda b, n, h: (b, jnp.maximum(n * tiles - 1, 0), col(t) + h))
    if dil == 1:
        kern = functools.partial(_dense_window_kernel, tiles=tiles, unroll=unroll, mxu_sums=mxu_sums)
        in_specs = [cur(0), prev(1), cur(1), prev(2), cur(2)]
        scratch = []
    else:
        kern = functools.partial(_strided_window_kernel, dil=dil, tiles=tiles, unroll=unroll, mxu_sums=mxu_sums)
        in_specs = [cur(0), cur(1), cur(2)]
        scratch = [pltpu.VMEM((tb, hd), F32)] * 4 + [pltpu.VMEM((D_HEADS, span, hd), BF16)] * 2
    o, lse = pl.pallas_call(
        kern,
        grid=(B, S // tb, D_HEADS),
        in_specs=in_specs,
        out_specs=[
            pl.BlockSpec((1, tb, hd), lambda b, n, h: (b, n, h)),
            pl.BlockSpec((1, tb, LANES), lambda b, n, h: (b, n, 0)),
        ],
        out_shape=[
            jax.ShapeDtypeStruct((B, S, D_HEADS * hd), BF16),
            jax.ShapeDtypeStruct((B, S, LANES), F32),
        ],
        scratch_shapes=[pltpu.VMEM((tb, LANES), F32)] + scratch,
        compiler_params=_params("parallel", "arbitrary", "arbitrary"),
        name="dilated_attention",
    )(*([qkv] * len(in_specs)))
    return o.reshape(B * S, D_HEADS * hd), lse.reshape(B * S, LANES)


def _combine_kernel(o0_ref, o1_ref, o2_ref, l0_ref, l1_ref, l2_ref, o_ref):
    l0, l1, l2 = l0_ref[...], l1_ref[...], l2_ref[...]
    mx = jnp.maximum(jnp.maximum(l0, l1), l2)
    e0, e1, e2 = jnp.exp(l0 - mx), jnp.exp(l1 - mx), jnp.exp(l2 - mx)
    inv = 1.0 / (e0 + e1 + e2)
    w0, w1 = e0 * inv, e1 * inv
    for h in range(D_HEADS):
        hs = slice(h * D_HEAD_DIM, (h + 1) * D_HEAD_DIM)
        o2 = o2_ref[:, hs].astype(F32)
        acc = o2 + w0[:, h:h + 1] * (o0_ref[:, hs].astype(F32) - o2)
        acc = acc + w1[:, h:h + 1] * (o1_ref[:, hs].astype(F32) - o2)
        o_ref[:, hs] = acc.astype(o_ref.dtype)


def combine_groups(outs, lses, *, tm=512):
    T, W = outs[0].shape
    tm = min(tm, T)
    wide = pl.BlockSpec((tm, W), lambda i: (i, 0))
    narrow = pl.BlockSpec((tm, LANES), lambda i: (i, 0))
    return pl.pallas_call(
        _combine_kernel,
        grid=(T // tm,),
        in_specs=[wide, wide, wide, narrow, narrow, narrow],
        out_specs=wide,
        out_shape=jax.ShapeDtypeStruct((T, W), BF16),
        compiler_params=_params("parallel"),
        name="combine_groups",
    )(*outs, *lses)


def _deinterleave_heads(w, heads, dim):
    k = w.shape[0]
    return w.reshape(k, heads, dim // 2, 2).transpose(0, 1, 3, 2).reshape(k, heads * dim)


def mixer_a(xr, B, S, nw, a_w_in, a_q_norm_w, a_k_norm_w, lq1, lk1, lq2, lk2, a_subln_w, a_w_out, *, layer_idx):
    T, D = xr.shape
    colw = jnp.concatenate([
        jnp.tile(a_q_norm_w.astype(F32) * (A_HEAD_DIM ** -0.5 * math.log2(math.e)), 2 * A_HEADS),
        jnp.tile(a_k_norm_w.astype(F32), 2 * A_HEADS)]).reshape(1, 2 * D)
    qk = norm_matmul(xr, nw, a_w_in[:, :2 * D].astype(BF16), colw)
    v_t = norm_matmul_t(xr, nw, a_w_in[:, 2 * D:].T.astype(BF16))
    o = diff_attention(qk.reshape(B, S, 2 * D), v_t, lq1, lk1, lq2, lk2, a_subln_w, layer_idx=layer_idx)
    return matmul_residual(o.reshape(T, D), a_w_out.astype(BF16), xr)


def mixer_b(xr, B, S, nw, b_w_in, b_gn_w, b_w_out):
    T, D = xr.shape
    w_in = jnp.concatenate([
        _deinterleave_heads(b_w_in[:, :D], R_HEADS, R_KDIM),
        _deinterleave_heads(b_w_in[:, D:2 * D], R_HEADS, R_KDIM),
        b_w_in[:, 2 * D:]], axis=1)
    proj = norm_matmul(xr, nw, w_in.astype(BF16))
    o = retention(proj.reshape(B, S, -1), b_gn_w)
    return matmul_residual(o.reshape(T, -1), b_w_out.astype(BF16), xr)


def mixer_c(xr, B, S, nw, c_w_in, c_conv_w, c_conv_b, c_dt_bias, c_a_log, c_d_skip, c_norm_w, c_w_out):
    T, D = xr.shape
    n_main = 2 * M_D_INNER + 2 * M_GROUPS * M_DSTATE
    w_dt = jnp.pad(c_w_in[:, n_main:], ((0, 0), (0, LANES - M_HEADS)))
    zx, dt_raw = norm_matmul(xr, nw, c_w_in[:, :n_main].astype(BF16), side_w=w_dt.astype(BF16))
    y = mamba_ssd(zx.reshape(B, S, n_main), dt_raw.reshape(B, S, LANES), c_conv_w, c_conv_b, c_dt_bias,
                  c_a_log, c_d_skip, c_norm_w)
    return matmul_residual(y.reshape(T, M_D_INNER), c_w_out.astype(BF16), xr)


def mixer_d(xr, B, S, nw, d_w_in, d_q_norm_w, d_k_norm_w, d_w_out):
    T, D = xr.shape
    scale = D_HEAD_DIM ** -0.5
    colw = jnp.concatenate([
        jnp.concatenate([jnp.tile(d_q_norm_w[g].astype(F32) * scale, D_HEADS),
                         jnp.tile(d_k_norm_w[g].astype(F32), D_HEADS),
                         jnp.ones((D,), F32)])
        for g in range(len(D_PATTERNS))]).reshape(1, -1)
    qkv = norm_matmul(xr, nw, d_w_in.astype(BF16), colw).reshape(B, S, -1)
    outs, lses = zip(*[dilated_group(qkv, g, dil, tiles=max(1, 8 // dil), unroll=8 if dil < 16 else 4,
                                     mxu_sums=dil < 16)
                       for g, (_, dil) in enumerate(D_PATTERNS)])
    o = combine_groups(outs, lses)
    return matmul_residual(o, d_w_out.astype(BF16), xr)


@jax.jit
def kernel(x, norm1_w, norm2_w, mlp_w1, mlp_w2, a_w_in, a_q_norm_w, a_k_norm_w, a_lambda_q1, a_lambda_k1, a_lambda_q2, a_lambda_k2, a_subln_w, a_w_out, b_w_in, b_gn_w, b_w_out, c_w_in, c_conv_w, c_conv_b, c_dt_bias, c_a_log, c_d_skip, c_norm_w, c_w_out, d_w_in, d_q_norm_w, d_k_norm_w, d_w_out):
    B, S, D = x.shape
    xr = x.reshape(B * S, D)
    w1, w2 = mlp_w1.astype(BF16), mlp_w2.astype(BF16)
    ffn = lambda t, i: mlp(t, norm2_w[i], w1, w2, i)
    xr = mixer_a(xr, B, S, norm1_w[0], a_w_in, a_q_norm_w, a_k_norm_w, a_lambda_q1, a_lambda_k1,
                 a_lambda_q2, a_lambda_k2, a_subln_w, a_w_out, layer_idx=0)
    xr = ffn(xr, 0)
    xr = mixer_b(xr, B, S, norm1_w[1], b_w_in, b_gn_w, b_w_out)
    xr = ffn(xr, 1)
    xr = mixer_c(xr, B, S, norm1_w[2], c_w_in, c_conv_w, c_conv_b, c_dt_bias, c_a_log, c_d_skip, c_norm_w,
                 c_w_out)
    xr = ffn(xr, 2)
    xr = mixer_d(xr, B, S, norm1_w[3], d_w_in, d_q_norm_w, d_k_norm_w, d_w_out)
    xr = ffn(xr, 3)
    return xr.reshape(B, S, D)
```
